```python
import math, functools
import jax, jax.numpy as jnp
from jax import lax
import numpy as np

D_MODEL = 1024
BATCH = 16
SEQ = 2048
DEPTH = 4
DEC_BATCH = 8
DEC_SEQ = 32
PAST_LEN = 4096

CHUNK = 64
A_HEADS = 4
A_HEAD_DIM = 64
A_WIDTH = A_HEADS * A_HEAD_DIM
A_CHUNK = 128
B_HEADS = 8
B_KV_HEADS = 2
HEAD_DIM = 64
B_WIDTH = B_HEADS * HEAD_DIM
B_KV_WIDTH = B_KV_HEADS * HEAD_DIM
IDX_HEADS = 4
IDX_DIM = 64
TOPK_MAX = 256
Q_BLOCK = 128
ROPE_THETA = 10000.0
C_GROUPS = 4
C_WIDTH = 256
CONV_WIDTH = 31
MIX_WIDTH = A_WIDTH + B_WIDTH + C_WIDTH
D_FF = -(-(8 * D_MODEL) // (3 * 256)) * 256
N_MOD = 6
EPS = 1e-6
IN_SIZES = (2 * A_WIDTH, B_WIDTH, B_KV_WIDTH, B_KV_WIDTH, IDX_HEADS * IDX_DIM, IDX_DIM, IDX_HEADS, 2 * C_WIDTH)
N_IN = 2 * A_WIDTH + B_WIDTH + 2 * B_KV_WIDTH + IDX_HEADS * IDX_DIM + IDX_DIM + IDX_HEADS + 2 * C_WIDTH

kernel_name = 'hybrid_streaming_encoder_step'


def rmsnorm(x, g):
    xf = x.astype(jnp.float32)
    y = xf * lax.rsqrt(jnp.mean(xf * xf, axis=-1, keepdims=True) + EPS)
    return (y * g.astype(jnp.float32)).astype(x.dtype)


def group_standardize(x, groups):
    shp = x.shape
    xf = x.astype(jnp.float32).reshape(shp[:-1] + (groups, shp[-1] // groups))
    mu = jnp.mean(xf, axis=-1, keepdims=True)
    xc = xf - mu
    var = jnp.mean(xc * xc, axis=-1, keepdims=True)
    return (xc * lax.rsqrt(var + EPS)).reshape(shp).astype(x.dtype)


def rope(x, pos):
    d = x.shape[-1]
    half = d // 2
    inv = jnp.power(ROPE_THETA, -2.0 * jnp.arange(half, dtype=jnp.float32) / d)
    ang = pos.astype(jnp.float32)[:, None] * inv[None, :]
    cos = jnp.cos(ang)[None, :, None, :]
    sin = jnp.sin(ang)[None, :, None, :]
    xf = x.astype(jnp.float32)
    x1, x2 = xf[..., :half], xf[..., half:]
    return jnp.concatenate([x1 * cos - x2 * sin, x2 * cos + x1 * sin], axis=-1).astype(x.dtype)


def to_blocks(a):
    b, s = a.shape[:2]
    return jnp.swapaxes(a.reshape((b, s // Q_BLOCK, Q_BLOCK) + a.shape[2:]), 0, 1)


def from_blocks(a):
    nb, b, qb = a.shape[:3]
    return jnp.swapaxes(a, 0, 1).reshape((b, nb * qb) + a.shape[3:])


def select_and_attend(q, qi, wi, q_pos, k_all, v_all, ki_all, k_pos, topk):
    dots = jnp.einsum('bthd,bsd->bths', qi.astype(jnp.float32), ki_all.astype(jnp.float32))
    score = jnp.einsum('bth,bths->bts', wi.astype(jnp.float32), jax.nn.relu(dots))
    allowed = (k_pos[None, :] // CHUNK) <= (q_pos[:, None] // CHUNK)
    score = jnp.where(allowed[None], score, -jnp.inf)
    top_val, top_idx = lax.top_k(score, topk)
    valid = jnp.isfinite(top_val)
    gather = jax.vmap(lambda a, i: a[i])
    kg = gather(k_all, top_idx)
    vg = gather(v_all, top_idx)
    b, t = q.shape[:2]
    qg = q.reshape(b, t, B_KV_HEADS, B_HEADS // B_KV_HEADS, HEAD_DIM)
    logits = jnp.einsum('btngd,btknd->btngk', qg, kg).astype(jnp.float32) * (HEAD_DIM ** -0.5)
    logits = jnp.where(valid[:, :, None, None, :], logits, -jnp.inf)
    p = jax.nn.softmax(logits, axis=-1).astype(vg.dtype)
    o = jnp.einsum('btngk,btknd->btngd', p, vg)
    return o.reshape(b, t, B_HEADS, HEAD_DIM)


def sparse_attention_mixer(zq, zk, zv, zqi, zki, zwi, pos, kv_cache):
    b, t = zq.shape[:2]
    q = rope(zq.reshape(b, t, B_HEADS, HEAD_DIM), pos)
    k = rope(zk.reshape(b, t, B_KV_HEADS, HEAD_DIM), pos)
    v = zv.reshape(b, t, B_KV_HEADS, HEAD_DIM)
    qi = rope(zqi.reshape(b, t, IDX_HEADS, IDX_DIM), pos)
    ki = rope(zki[:, :, None, :], pos)[:, :, 0, :]
    if kv_cache is None:
        k_all, v_all, ki_all = k, v, ki
    else:
        ck, cv, cki = kv_cache
        k_all = jnp.concatenate([ck, k], axis=1)
        v_all = jnp.concatenate([cv, v], axis=1)
        ki_all = jnp.concatenate([cki, ki], axis=1)
    n_keys = k_all.shape[1]
    k_pos = jnp.arange(n_keys)
    topk = min(TOPK_MAX, n_keys // 4)

    def attend(blk):
        qb, qib, wib, pb = blk
        return select_and_attend(qb, qib, wib, pb, k_all, v_all, ki_all, k_pos, topk)

    if t >= Q_BLOCK:
        nb = t // Q_BLOCK
        o = from_blocks(lax.map(attend, (to_blocks(q), to_blocks(qi), to_blocks(zwi), pos.reshape(nb, Q_BLOCK))))
    else:
        o = attend((q, qi, zwi, pos))
    return o.reshape(b, t, B_WIDTH), k, v, ki


def chunk_gmlp_mixer(za, w_sp, b_sp):
    b, t = za.shape[:2]
    za = jax.nn.gelu(za, approximate=False)
    u, v = jnp.split(za, 2, axis=-1)
    v = group_standardize(v, A_HEADS)
    ln = min(t, A_CHUNK)
    n = t // ln
    mask = jnp.tril(jnp.ones((ln, ln), dtype=bool))
    w = jnp.where(mask[None], w_sp[:, :ln, :ln], 0.0)
    vb = v.reshape(b, n, ln, A_HEADS, A_HEAD_DIM)
    s = jnp.einsum('gts,bnsgd->bntgd', w, vb) + b_sp[:, :ln].T[:, :, None]
    return u * s.reshape(b, t, A_WIDTH), v


def conformer_conv_mixer(zc, conv_state, w_dw, b_dw, g_cn):
    b = zc.shape[0]
    a, g = jnp.split(zc, 2, axis=-1)
    h = a * jax.nn.sigmoid(g)
    if conv_state is None:
        conv_state = jnp.zeros((b, CONV_WIDTH - 1, C_WIDTH), h.dtype)
    hp = jnp.concatenate([conv_state, h], axis=1)
    y = lax.conv_general_dilated(hp, w_dw[:, None, :], window_strides=(1,), padding='VALID',
                                 dimension_numbers=('NWC', 'WIO', 'NWC'), feature_group_count=C_WIDTH) + b_dw
    y = jax.nn.silu(group_standardize(y, C_GROUPS) * g_cn)
    return y, hp[:, -(CONV_WIDTH - 1):]


def run_trunk(x, c, cache, w_ada, b_ada, g_norm1, g_norm2, w_in, w_spatial, b_spatial, w_dw, b_dw,
              g_cnorm, w_out, w_gate, w_up, w_down, g_final):
    t = x.shape[1]
    start = 0 if cache is None else cache[0].shape[2]
    pos = start + jnp.arange(t)
    cond = jax.nn.silu(c)
    offsets = np.cumsum(IN_SIZES)[:-1].tolist()
    ks, vs, kis, convs, avs = [], [], [], [], []
    for l in range(DEPTH):
        mod = cond @ w_ada[l] + b_ada[l]
        sh1, sc1, ga1, sh2, sc2, ga2 = [m[:, None, :] for m in jnp.split(mod, N_MOD, axis=-1)]
        h = rmsnorm(x, g_norm1[l]) * (1.0 + sc1) + sh1
        z = h @ w_in[l]
        za, zq, zk, zv, zqi, zki, zwi, zc = jnp.split(z, offsets, axis=-1)
        if cache is None:
            kv_cache, conv_state = None, None
        else:
            kv_cache = (cache[0][l], cache[1][l], cache[2][l])
            conv_state = cache[3][l]
        a_out, a_v = chunk_gmlp_mixer(za, w_spatial[l], b_spatial[l])
        b_out, k_new, v_new, ki_new = sparse_attention_mixer(zq, zk, zv, zqi, zki, zwi, pos, kv_cache)
        c_out, conv_new = conformer_conv_mixer(zc, conv_state, w_dw[l], b_dw[l], g_cnorm[l])
        mix = jnp.concatenate([a_out, b_out, c_out], axis=-1) @ w_out[l]
        x = x + ga1 * mix
        h2 = rmsnorm(x, g_norm2[l]) * (1.0 + sc2) + sh2
        x = x + ga2 * ((jax.nn.silu(h2 @ w_gate[l]) * (h2 @ w_up[l])) @ w_down[l])
        ks.append(k_new); vs.append(v_new); kis.append(ki_new); convs.append(conv_new); avs.append(a_v)
    y = rmsnorm(x, g_final)
    return y, jnp.stack(ks), jnp.stack(vs), jnp.stack(kis), jnp.stack(convs), jnp.stack(avs)


def setup_inputs(seed: int = 0) -> dict:
    key = jax.random.key(seed)
    kk = jax.random.split(key, 26)
    nrm = lambda k, shape, s: jax.random.normal(k, shape, jnp.float32) * s
    return {
        'x_prompt': nrm(kk[0], (BATCH, SEQ, D_MODEL), 1.0),
        'x_sample': nrm(kk[1], (DEC_BATCH, DEC_SEQ, D_MODEL), 1.0),
        'cache_k': nrm(kk[2], (DEPTH, DEC_BATCH, PAST_LEN, B_KV_HEADS, HEAD_DIM), 1.0),
        'cache_v': nrm(kk[3], (DEPTH, DEC_BATCH, PAST_LEN, B_KV_HEADS, HEAD_DIM), 1.0),
        'cache_kidx': nrm(kk[4], (DEPTH, DEC_BATCH, PAST_LEN, IDX_DIM), 1.0),
        'state_conv': nrm(kk[5], (DEPTH, DEC_BATCH, CONV_WIDTH - 1, C_WIDTH), 0.5),
        'c_prompt': nrm(kk[6], (BATCH, D_MODEL), 1.0),
        'c_sample': nrm(kk[7], (DEC_BATCH, D_MODEL), 1.0),
        'w_ada': nrm(kk[8], (DEPTH, D_MODEL, N_MOD * D_MODEL), 0.5 * D_MODEL ** -0.5),
        'b_ada': nrm(kk[9], (DEPTH, N_MOD * D_MODEL), 0.01),
        'g_norm1': 1.0 + nrm(kk[10], (DEPTH, D_MODEL), 0.01),
        'g_norm2': 1.0 + nrm(kk[11], (DEPTH, D_MODEL), 0.01),
        'w_in': nrm(kk[12], (DEPTH, D_MODEL, N_IN), D_MODEL ** -0.5),
        'w_spatial': nrm(kk[13], (DEPTH, A_HEADS, A_CHUNK, A_CHUNK), A_CHUNK ** -0.5),
        'b_spatial': 1.0 + nrm(kk[14], (DEPTH, A_HEADS, A_CHUNK), 0.1),
        'w_dw': nrm(kk[15], (DEPTH, CONV_WIDTH, C_WIDTH), CONV_WIDTH ** -0.5),
        'b_dw': nrm(kk[16], (DEPTH, C_WIDTH), 0.01),
        'g_cnorm': 1.0 + nrm(kk[17], (DEPTH, C_WIDTH), 0.01),
        'w_out': nrm(kk[18], (DEPTH, MIX_WIDTH, D_MODEL), MIX_WIDTH ** -0.5),
        'w_gate': nrm(kk[19], (DEPTH, D_MODEL, D_FF), D_MODEL ** -0.5),
        'w_up': nrm(kk[20], (DEPTH, D_MODEL, D_FF), D_MODEL ** -0.5),
        'w_down': nrm(kk[21], (DEPTH, D_FF, D_MODEL), D_FF ** -0.5),
        'g_final': 1.0 + nrm(kk[22], (D_MODEL,), 0.01),
    }


def reference(x_prompt, x_sample, cache_k, cache_v, cache_kidx, state_conv, c_prompt, c_sample,
              w_ada, b_ada, g_norm1, g_norm2, w_in, w_spatial, b_spatial, w_dw, b_dw, g_cnorm,
              w_out, w_gate, w_up, w_down, g_final):
    y_prompt, p_k, p_v, p_kidx, p_conv, _ = run_trunk(
        x_prompt, c_prompt, None, w_ada, b_ada, g_norm1, g_norm2, w_in, w_spatial, b_spatial,
        w_dw, b_dw, g_cnorm, w_out, w_gate, w_up, w_down, g_final)
    y_sample, s_k, s_v, s_kidx, s_conv, s_av = run_trunk(
        x_sample, c_sample, (cache_k, cache_v, cache_kidx, state_conv), w_ada, b_ada, g_norm1, g_norm2,
        w_in, w_spatial, b_spatial, w_dw, b_dw, g_cnorm, w_out, w_gate, w_up, w_down, g_final)
    return (y_prompt, y_sample, p_k, p_v, p_kidx, p_conv, s_k, s_v, s_kidx, s_conv, s_av)
```

```python
import functools

import jax
import jax.numpy as jnp
from jax import lax
from jax.experimental import pallas as pl
from jax.experimental.pallas import tpu as pltpu

CHUNK = 64
CHUNK_SHIFT = 6
assert 1 << CHUNK_SHIFT == CHUNK
A_HEADS = 4
A_HEAD_DIM = 64
A_WIDTH = A_HEADS * A_HEAD_DIM
A_CHUNK = 128
B_HEADS = 8
B_KV_HEADS = 2
HEAD_DIM = 64
B_WIDTH = B_HEADS * HEAD_DIM
B_KV_WIDTH = B_KV_HEADS * HEAD_DIM
IDX_HEADS = 4
IDX_DIM = 64
TOPK_MAX = 256
ROPE_THETA = 10000.0
C_GROUPS = 4
C_WIDTH = 256
CONV_WIDTH = 31
N_MOD = 6
EPS = 1e-6

LANES = 128
CTX_ROWS = 32
CTX_PAD = CTX_ROWS - (CONV_WIDTH - 1)
VMEM_LIMIT = 56 * 1024 * 1024

OFF_A = 0
OFF_Q = OFF_A + 2 * A_WIDTH
OFF_K = OFF_Q + B_WIDTH
OFF_V = OFF_K + B_KV_WIDTH
OFF_QI = OFF_V + B_KV_WIDTH
OFF_KI = OFF_QI + IDX_HEADS * IDX_DIM
OFF_C = OFF_KI + LANES
N_IN_PAD = OFF_C + 2 * C_WIDTH

F32 = jnp.float32
BF16 = jnp.bfloat16


def _dot(a, b):
    return jnp.dot(a, b, preferred_element_type=F32)


def _dot_t(a, b):
    return lax.dot_general(a, b, (((1,), (1,)), ((), ())), preferred_element_type=F32)


def _rms(x):
    return x * lax.rsqrt(jnp.mean(x * x, axis=-1, keepdims=True) + EPS)


def _lane_group(n, group):
    shift = group.bit_length() - 1
    assert 1 << shift == group
    return lax.shift_right_logical(lax.broadcasted_iota(jnp.int32, (1, n), 1), shift)


def _group_standardize(y, group):
    n = y.shape[-1]
    gid = _lane_group(n, group)
    inv = 1.0 / group
    mean = jnp.zeros_like(y)
    for g in range(n // group):
        m = gid == g
        s = jnp.sum(jnp.where(m, y, 0.0), axis=-1, keepdims=True) * inv
        mean = jnp.where(m, s, mean)
    yc = y - mean
    sq = yc * yc
    var = jnp.zeros_like(y)
    for g in range(n // group):
        m = gid == g
        s = jnp.sum(jnp.where(m, sq, 0.0), axis=-1, keepdims=True) * inv
        var = jnp.where(m, s, var)
    return yc * lax.rsqrt(var + EPS)


def _rope(x, cos, sin_lo, sin_hi):
    parts = []
    for c in range(x.shape[-1] // LANES):
        xc = x[:, c * LANES:(c + 1) * LANES]
        up = pltpu.roll(xc, LANES - HEAD_DIM // 2, 1)
        down = pltpu.roll(xc, HEAD_DIM // 2, 1)
        parts.append(xc * cos + up * sin_lo + down * sin_hi)
    return parts[0] if len(parts) == 1 else jnp.concatenate(parts, axis=-1)


def _mod_kernel(c_ref, w_ref, b_ref, o_ref):
    c = c_ref[...]
    cond = c * jax.nn.sigmoid(c)
    o_ref[0] = _dot(cond.astype(BF16), w_ref[0].astype(BF16)) + b_ref[0]


def _mod_call(c_all, w_ada, b_ada):
    depth, d, n = w_ada.shape
    rows = c_all.shape[0]
    tn = d
    return pl.pallas_call(
        _mod_kernel,
        out_shape=jax.ShapeDtypeStruct((depth, rows, n), F32),
        grid=(depth, n // tn),
        in_specs=[
            pl.BlockSpec((rows, d), lambda l, j: (0, 0)),
            pl.BlockSpec((1, d, tn), lambda l, j: (l, 0, j)),
            pl.BlockSpec((1, 1, tn), lambda l, j: (l, 0, j)),
        ],
        out_specs=pl.BlockSpec((1, rows, tn), lambda l, j: (l, 0, j)),
        compiler_params=pltpu.CompilerParams(
            dimension_semantics=("arbitrary", "arbitrary"), vmem_limit_bytes=VMEM_LIMIT),
        name="adaln_mod",
    )(c_all, w_ada, b_ada.reshape(depth, 1, n))


def _in_kernel(x_ref, mod_ref, g1_ref, w_ref, cos_ref, slo_ref, shi_ref, wsp_ref, bsp_ref,
               wdw_ref, bdw_ref, gcn_ref, cst_ref,
               q_ref, k_ref, v_ref, qi_ref, ki_ref, wi_ref, ac_ref, av_ref, cnew_ref,
               hp_ref, *, tm, ln, d_model):
    j = pl.program_id(1)
    x = x_ref[0]
    mod = mod_ref[0]
    sh1 = mod[:, 0:d_model]
    sc1 = mod[:, d_model:2 * d_model]
    h = (_rms(x) * g1_ref[...]) * (1.0 + sc1) + sh1
    z = _dot(h.astype(BF16), w_ref[...])

    cos = cos_ref[...]
    slo = slo_ref[...]
    shi = shi_ref[...]

    q = _rope(z[:, OFF_Q:OFF_Q + B_WIDTH], cos, slo, shi)
    q_ref[0] = (q * (HEAD_DIM ** -0.5)).astype(BF16)
    k_ref[0] = _rope(z[:, OFF_K:OFF_K + B_KV_WIDTH], cos, slo, shi)
    v_ref[0] = z[:, OFF_V:OFF_V + B_KV_WIDTH]
    qi_ref[0] = _rope(z[:, OFF_QI:OFF_QI + IDX_HEADS * IDX_DIM], cos, slo, shi).astype(BF16)
    kiwi = z[:, OFF_KI:OFF_KI + LANES]
    ki_ref[0] = _rope(kiwi, cos, slo, shi)[:, 0:IDX_DIM]
    wi_ref[0] = kiwi

    za = z[:, OFF_A:OFF_A + 2 * A_WIDTH]
    za = 0.5 * za * (1.0 + lax.erf(za * (2.0 ** -0.5)))
    u = za[:, 0:A_WIDTH]
    vn = _group_standardize(za[:, A_WIDTH:2 * A_WIDTH], A_HEAD_DIM)
    av_ref[0] = vn
    vb = vn.astype(BF16)
    row = lax.broadcasted_iota(jnp.int32, (ln, ln), 0)
    col = lax.broadcasted_iota(jnp.int32, (ln, ln), 1)
    head_of_lane = _lane_group(A_WIDTH, A_HEAD_DIM)
    wsp = [jnp.where(col <= row, wsp_ref[g], 0.0).astype(BF16) for g in range(A_HEADS)]
    gated = []
    for c in range(tm // ln):
        vc = vb[c * ln:(c + 1) * ln, :]
        s = jnp.zeros((ln, A_WIDTH), F32)
        for g in range(A_HEADS):
            s = jnp.where(head_of_lane == g, _dot(wsp[g], vc), s)
        gated.append(u[c * ln:(c + 1) * ln, :] * (s + bsp_ref[...]))
    a_out = gated[0] if len(gated) == 1 else jnp.concatenate(gated, axis=0)
    ac_ref[0, :, 0:A_WIDTH] = a_out.astype(BF16)

    zc = z[:, OFF_C:OFF_C + 2 * C_WIDTH]
    hc = zc[:, 0:C_WIDTH] * jax.nn.sigmoid(zc[:, C_WIDTH:2 * C_WIDTH])

    @pl.when(j == 0)
    def _():
        hp_ref[0:CTX_ROWS, :] = cst_ref[0]

    hp_ref[CTX_ROWS:CTX_ROWS + tm, :] = hc
    rb = min(tm, 64)
    wdw = wdw_ref[...]
    conv = []
    for r0 in range(0, tm, rb):
        acc = jnp.zeros((rb, C_WIDTH), F32)
        for t in range(CONV_WIDTH):
            acc = acc + hp_ref[CTX_PAD + r0 + t:CTX_PAD + r0 + t + rb, :] * wdw[t:t + 1, :]
        conv.append(acc)
    y = (conv[0] if len(conv) == 1 else jnp.concatenate(conv, axis=0)) + bdw_ref[...]
    y = _group_standardize(y, C_WIDTH // C_GROUPS) * gcn_ref[...]
    ac_ref[0, :, A_WIDTH:A_WIDTH + C_WIDTH] = (y * jax.nn.sigmoid(y)).astype(BF16)

    tail = hp_ref[tm:tm + CTX_ROWS, :]
    cnew_ref[0] = tail
    hp_ref[0:CTX_ROWS, :] = tail


def _in_call(x, mod, layer, mod_row0, g1, w_in_p, tabs, wsp, bsp_tab, wdw, bdw, gcn, cst, tm):
    b, t, d = x.shape
    ln = min(t, A_CHUNK)
    cos, slo, shi = tabs
    kern = functools.partial(_in_kernel, tm=tm, ln=ln, d_model=d)
    row3 = lambda bi, j: (bi, j, 0)
    full2 = lambda bi, j: (0, 0)
    lay3 = lambda bi, j: (layer, 0, 0)
    out_shape = (
        jax.ShapeDtypeStruct((b, t, B_WIDTH), BF16),
        jax.ShapeDtypeStruct((b, t, B_KV_WIDTH), F32),
        jax.ShapeDtypeStruct((b, t, B_KV_WIDTH), F32),
        jax.ShapeDtypeStruct((b, t, IDX_HEADS * IDX_DIM), BF16),
        jax.ShapeDtypeStruct((b, t, IDX_DIM), F32),
        jax.ShapeDtypeStruct((b, t, LANES), F32),
        jax.ShapeDtypeStruct((b, t, A_WIDTH + C_WIDTH), BF16),
        jax.ShapeDtypeStruct((b, t, A_WIDTH), F32),
        jax.ShapeDtypeStruct((b, CTX_ROWS, C_WIDTH), F32),
    )
    out_specs = (
        pl.BlockSpec((1, tm, B_WIDTH), row3),
        pl.BlockSpec((1, tm, B_KV_WIDTH), row3),
        pl.BlockSpec((1, tm, B_KV_WIDTH), row3),
        pl.BlockSpec((1, tm, IDX_HEADS * IDX_DIM), row3),
        pl.BlockSpec((1, tm, IDX_DIM), row3),
        pl.BlockSpec((1, tm, LANES), row3),
        pl.BlockSpec((1, tm, A_WIDTH + C_WIDTH), row3),
        pl.BlockSpec((1, tm, A_WIDTH), row3),
        pl.BlockSpec((1, CTX_ROWS, C_WIDTH), lambda bi, j: (bi, 0, 0)),
    )
    in_specs = [
        pl.BlockSpec((1, tm, d), row3),
        pl.BlockSpec((None, 1, 1, N_MOD * d), lambda bi, j: (layer, mod_row0 + bi, 0, 0)),
        pl.BlockSpec((None, 1, d), lay3),
        pl.BlockSpec((None, d, N_IN_PAD), lay3),
        pl.BlockSpec((tm, LANES), lambda bi, j: (j, 0)),
        pl.BlockSpec((tm, LANES), lambda bi, j: (j, 0)),
        pl.BlockSpec((tm, LANES), lambda bi, j: (j, 0)),
        pl.BlockSpec((None, A_HEADS, ln, ln), lambda bi, j: (layer, 0, 0, 0)),
        pl.BlockSpec((None, ln, A_WIDTH), lay3),
        pl.BlockSpec((None, CONV_WIDTH, C_WIDTH), lay3),
        pl.BlockSpec((None, 1, C_WIDTH), lay3),
        pl.BlockSpec((None, 1, C_WIDTH), lay3),
        pl.BlockSpec((1, CTX_ROWS, C_WIDTH), lambda bi, j: (bi, 0, 0)),
    ]
    del full2
    return pl.pallas_call(
        kern,
        out_shape=out_shape,
        grid=(b, t // tm),
        in_specs=in_specs,
        out_specs=out_specs,
        scratch_shapes=[pltpu.VMEM((tm + CTX_ROWS, C_WIDTH), F32)],
        compiler_params=pltpu.CompilerParams(
            dimension_semantics=("arbitrary", "arbitrary"), vmem_limit_bytes=VMEM_LIMIT),
        name="in_proj_mixers_ac",
    )(x, mod, g1, w_in_p, cos, slo, shi, wsp, bsp_tab, wdw, bdw, gcn, cst)


def _attn_kernel(q_ref, qi_ref, wi_ref, k_ref, v_ref, ki_ref, o_ref, key_ref, bias_ref,
                 *, tq, n_keys, q_start, topk):
    lk = k_ref.shape[1]
    j = pl.program_id(1)
    qchunk = lax.shift_right_logical(
        q_start + j * tq + lax.broadcasted_iota(jnp.int32, (tq, 1), 0), CHUNK_SHIFT)

    def allowed_at(k0, width):
        kpos = k0 + lax.broadcasted_iota(jnp.int32, (1, width), 1)
        return (lax.shift_right_logical(kpos, CHUNK_SHIFT) <= qchunk) & (kpos < n_keys)

    kib = ki_ref[0].astype(BF16)
    qi = qi_ref[0]
    wi = wi_ref[0]
    score = jnp.zeros((tq, lk), F32)
    for h in range(IDX_HEADS):
        d = _dot_t(qi[:, h * IDX_DIM:(h + 1) * IDX_DIM], kib)
        score = score + wi[:, IDX_DIM + h:IDX_DIM + h + 1] * jnp.maximum(d, 0.0)
    score = jnp.where(score == 0.0, 0.0, score)
    score = jnp.where(allowed_at(0, lk), score, -jnp.inf)

    bits = pltpu.bitcast(score, jnp.int32)
    key_ref[...] = bits ^ ((bits >> 31) & jnp.int32(0x7FFFFFFF))
    int_min = jnp.int32(-2 ** 31)

    def bit_step(i, thr):
        cand = thr + lax.shift_left(jnp.int32(1), jnp.int32(31) - i)
        cnt = jnp.sum((key_ref[...] >= cand).astype(F32), axis=-1, keepdims=True)
        return jnp.where(cnt >= topk, cand, thr)

    thr = lax.fori_loop(0, 32, bit_step, jnp.full((tq, 1), int_min, jnp.int32))

    need = topk - jnp.sum((key_ref[...] > thr).astype(F32), axis=-1, keepdims=True)
    tri = (lax.broadcasted_iota(jnp.int32, (LANES, LANES), 0)
           <= lax.broadcasted_iota(jnp.int32, (LANES, LANES), 1)).astype(F32).astype(BF16)
    seen = jnp.zeros((tq, 1), F32)
    for c in range(lk // LANES):
        sl = slice(c * LANES, (c + 1) * LANES)
        key_c = key_ref[:, sl]
        eq_c = key_c == thr
        eq_f = eq_c.astype(F32)
        rank = _dot(eq_f.astype(BF16), tri) + seen
        chosen = ((key_c > thr) | (eq_c & (rank <= need))) & allowed_at(c * LANES, LANES)
        bias_ref[:, sl] = jnp.where(chosen, 0.0, -jnp.inf)
        seen = seen + jnp.sum(eq_f, axis=-1, keepdims=True)

    q = q_ref[0]
    kb = k_ref[0].astype(BF16)
    vb = v_ref[0].astype(BF16)
    bias = bias_ref[...]
    group = B_HEADS // B_KV_HEADS
    outs = []
    for n in range(B_KV_HEADS):
        kn = kb[:, n * HEAD_DIM:(n + 1) * HEAD_DIM]
        vn = vb[:, n * HEAD_DIM:(n + 1) * HEAD_DIM]
        for g in range(group):
            hq = n * group + g
            logits = _dot_t(q[:, hq * HEAD_DIM:(hq + 1) * HEAD_DIM], kn) + bias
            p = jnp.exp(logits - jnp.max(logits, axis=-1, keepdims=True))
            den = jnp.sum(p, axis=-1, keepdims=True)
            outs.append(_dot(p.astype(BF16), vn) / den)
    o_ref[0] = jnp.concatenate(outs, axis=-1).astype(BF16)


def _attn_call(q, qi, wi, k_all, v_all, ki_all, n_keys, q_start, tq):
    b, t, _ = q.shape
    lk = k_all.shape[1]
    topk = min(TOPK_MAX, n_keys // 4)
    kern = functools.partial(_attn_kernel, tq=tq, n_keys=n_keys, q_start=q_start, topk=topk)
    row3 = lambda bi, j: (bi, j, 0)
    bat3 = lambda bi, j: (bi, 0, 0)
    return pl.pallas_call(
        kern,
        out_shape=jax.ShapeDtypeStruct((b, t, B_WIDTH), BF16),
        grid=(b, t // tq),
        in_specs=[
            pl.BlockSpec((1, tq, B_WIDTH), row3),
            pl.BlockSpec((1, tq, IDX_HEADS * IDX_DIM), row3),
            pl.BlockSpec((1, tq, LANES), row3),
            pl.BlockSpec((1, lk, B_KV_WIDTH), bat3),
            pl.BlockSpec((1, lk, B_KV_WIDTH), bat3),
            pl.BlockSpec((1, lk, IDX_DIM), bat3),
        ],
        out_specs=pl.BlockSpec((1, tq, B_WIDTH), row3),
        scratch_shapes=[pltpu.VMEM((tq, lk), jnp.int32), pltpu.VMEM((tq, lk), F32)],
        compiler_params=pltpu.CompilerParams(
            dimension_semantics=("arbitrary", "arbitrary"), vmem_limit_bytes=VMEM_LIMIT),
        name="dsa_attention",
    )(q, qi, wi, k_all, v_all, ki_all)


def _ffn_kernel(x_ref, ac_ref, bo_ref, mod_ref, g2_ref, woac_ref, wob_ref, wg_ref, wu_ref, wd_ref,
                gf_ref, o_ref, *, d_model, ff_chunk, final):
    x = x_ref[0]
    mod = mod_ref[0]
    ga1 =mod[:, 2 * d_model:3 * d_model]
    sh2 = mod[:, 3 * d_model:4 * d_model]
    sc2 = mod[:, 4 * d_model:5 * d_model]
    ga2 = mod[:, 5 * d_model:6 * d_model]
    mix = _dot(ac_ref[0], woac_ref[...]) + _dot(bo_ref[0], wob_ref[...])
    x1 = x + ga1 * mix
    h2 = ((_rms(x1) * g2_ref[...]) * (1.0 + sc2) + sh2).astype(BF16)
    d_ff = wg_ref.shape[1]
    acc = jnp.zeros_like(x1)
    for c0 in range(0, d_ff, ff_chunk):
        gate = _dot(h2, wg_ref[:, c0:c0 + ff_chunk])
        up = _dot(h2, wu_ref[:, c0:c0 + ff_chunk])
        act = (gate * jax.nn.sigmoid(gate)) * up
        acc = acc + _dot(act.astype(BF16), wd_ref[c0:c0 + ff_chunk, :])
    x2 = x1 + ga2 * acc
    if final:
        o_ref[0] = _rms(x2) * gf_ref[...]
    else:
        o_ref[0] = x2


def _ffn_call(x, ac, bo, mod, layer, mod_row0, g2, woac, wob, wg, wu, wd, gf, tm, final):
    b, t, d = x.shape
    d_ff = wg.shape[2]
    ff_chunk = d_ff // 2 if (d_ff // 2) % LANES == 0 else d_ff
    kern = functools.partial(_ffn_kernel, d_model=d, ff_chunk=ff_chunk, final=final)
    row3 = lambda bi, j: (bi, j, 0)
    lay3 = lambda bi, j: (layer, 0, 0)
    once = pl.Buffered(1)
    return pl.pallas_call(
        kern,
        out_shape=jax.ShapeDtypeStruct((b, t, d), F32),
        grid=(b, t // tm),
        in_specs=[
            pl.BlockSpec((1, tm, d), row3),
            pl.BlockSpec((1, tm, A_WIDTH + C_WIDTH), row3),
            pl.BlockSpec((1, tm, B_WIDTH), row3),
            pl.BlockSpec((None, 1, 1, N_MOD * d), lambda bi, j: (layer, mod_row0 + bi, 0, 0)),
            pl.BlockSpec((None, 1, d), lay3),
            pl.BlockSpec((None, A_WIDTH + C_WIDTH, d), lay3, pipeline_mode=once),
            pl.BlockSpec((None, B_WIDTH, d), lay3, pipeline_mode=once),
            pl.BlockSpec((None, d, d_ff), lay3, pipeline_mode=once),
            pl.BlockSpec((None, d, d_ff), lay3, pipeline_mode=once),
            pl.BlockSpec((None, d_ff, d), lay3, pipeline_mode=once),
            pl.BlockSpec((1, d), lambda bi, j: (0, 0)),
        ],
        out_specs=pl.BlockSpec((1, tm, d), row3),
        compiler_params=pltpu.CompilerParams(
            dimension_semantics=("arbitrary", "arbitrary"), vmem_limit_bytes=VMEM_LIMIT),
        name="out_proj_ffn",
    )(x, ac, bo, mod, g2, woac, wob, wg, wu, wd, gf)


def _rope_tables(start, t):
    half = HEAD_DIM // 2
    inv = jnp.power(ROPE_THETA, -2.0 * jnp.arange(half, dtype=F32) / HEAD_DIM)
    ang = (start + jnp.arange(t)).astype(F32)[:, None] * inv[None, :]
    cos = jnp.cos(ang)
    sin = jnp.sin(ang)
    zero = jnp.zeros_like(sin)
    reps = LANES // HEAD_DIM
    return (jnp.tile(jnp.concatenate([cos, cos], axis=-1), (1, reps)),
            jnp.tile(jnp.concatenate([-sin, zero], axis=-1), (1, reps)),
            jnp.tile(jnp.concatenate([zero, sin], axis=-1), (1, reps)))


def _pad_w_in(w_in):
    n_head = OFF_KI + IDX_DIM + IDX_HEADS
    depth, d, _ = w_in.shape
    pad = jnp.zeros((depth, d, OFF_C - n_head), w_in.dtype)
    return jnp.concatenate([w_in[:, :, :n_head], pad, w_in[:, :, n_head:]], axis=-1).astype(BF16)


def _trunk(x, mod, mod_row0, cache, params, q_start, tm, tq):
    (g_norm1, g_norm2, w_in_p, w_spatial, b_spatial, w_dw, b_dw, g_cnorm,
     woac, wob, wg, wu, wd, g_final) = params
    b, t, d = x.shape
    depth = w_in_p.shape[0]
    ln = min(t, A_CHUNK)
    tabs = _rope_tables(q_start, t)
    wsp = w_spatial[:, :, :ln, :ln]
    bsp_tab = jnp.repeat(jnp.swapaxes(b_spatial[:, :, :ln], 1, 2), A_HEAD_DIM, axis=-1)
    ks, vs, kis, convs, avs = [], [], [], [], []
    for l in range(depth):
        if cache is None:
            cst = jnp.zeros((b, CTX_ROWS, C_WIDTH), F32)
        else:
            cst = jnp.pad(cache[3][l], ((0, 0), (CTX_PAD, 0), (0, 0)))
        q, k, v, qi, ki, wi, ac, av, cnew = _in_call(
            x, mod, l, mod_row0, g_norm1, w_in_p, tabs, wsp, bsp_tab, w_dw, b_dw, g_cnorm, cst, tm)
        if cache is None:
            k_all, v_all, ki_all, n_keys = k, v, ki, t
        else:
            past = cache[0].shape[2]
            n_keys = past + t
            lk = -(-n_keys // LANES) * LANES
            grow = lambda old, new: jnp.pad(jnp.concatenate([old, new], axis=1),
                                            ((0, 0), (0, lk - n_keys), (0, 0)))
            k_all = grow(cache[0][l].reshape(b, past, B_KV_WIDTH), k)
            v_all = grow(cache[1][l].reshape(b, past, B_KV_WIDTH), v)
            ki_all = grow(cache[2][l], ki)
        bo = _attn_call(q, qi, wi, k_all, v_all, ki_all, n_keys, q_start, tq)
        x = _ffn_call(x, ac, bo, mod, l, mod_row0, g_norm2, woac, wob, wg, wu, wd, g_final,
                      tm, final=(l == depth - 1))
        ks.append(k.reshape(b, t, B_KV_HEADS, HEAD_DIM))
        vs.append(v.reshape(b, t, B_KV_HEADS, HEAD_DIM))
        kis.append(ki)
        convs.append(cnew[:, CTX_PAD:, :])
        avs.append(av)
    return x, jnp.stack(ks), jnp.stack(vs), jnp.stack(kis), jnp.stack(convs), jnp.stack(avs)


def kernel(x_prompt, x_sample, cache_k, cache_v, cache_kidx, state_conv, c_prompt, c_sample,
           w_ada, b_ada, g_norm1, g_norm2, w_in, w_spatial, b_spatial, w_dw, b_dw, g_cnorm,
           w_out, w_gate, w_up, w_down, g_final):
    depth, d = g_norm1.shape
    nb_p, t_p, _ = x_prompt.shape
    nb_s, t_s, _ = x_sample.shape
    past = cache_k.shape[2]

    rows = nb_p + nb_s
    rows_pad = -(-rows // 8) * 8
    c_all = jnp.pad(jnp.concatenate([c_prompt, c_sample], axis=0), ((0, rows_pad - rows), (0, 0)))
    mod = _mod_call(c_all, w_ada, b_ada).reshape(depth, rows_pad, 1, N_MOD * d)

    w_out_b = w_out.astype(BF16)
    params = (
        g_norm1.reshape(depth, 1, d), g_norm2.reshape(depth, 1, d), _pad_w_in(w_in),
        w_spatial, b_spatial, w_dw, b_dw.reshape(depth, 1, C_WIDTH), g_cnorm.reshape(depth, 1, C_WIDTH),
        jnp.concatenate([w_out_b[:, :A_WIDTH], w_out_b[:, A_WIDTH + B_WIDTH:]], axis=1),
        w_out_b[:, A_WIDTH:A_WIDTH + B_WIDTH],
        w_gate.astype(BF16), w_up.astype(BF16), w_down.astype(BF16), g_final.reshape(1, d),
    )

    y_p, p_k, p_v, p_ki, p_conv, _ = _trunk(
        x_prompt, mod, 0, None, params, 0, tm=min(t_p, 512), tq=min(t_p, 128))
    y_s, s_k, s_v, s_ki, s_conv, s_av = _trunk(
        x_sample, mod, nb_p, (cache_k, cache_v, cache_kidx, state_conv), params, past,
        tm=t_s, tq=t_s)
    return (y_p, y_s, p_k, p_v, p_ki, p_conv, s_k, s_v, s_ki, s_conv, s_av)
```

```python
import functools

import jax
import jax.numpy as jnp
from jax import lax
from jax.experimental import pallas as pl
from jax.experimental.pallas import tpu as pltpu

CHUNK = 64
CHUNK_SHIFT = 6
assert 1 << CHUNK_SHIFT == CHUNK
A_HEADS = 4
A_HEAD_DIM = 64
A_WIDTH = A_HEADS * A_HEAD_DIM
A_CHUNK = 128
B_HEADS = 8
B_KV_HEADS = 2
HEAD_DIM = 64
B_WIDTH = B_HEADS * HEAD_DIM
B_KV_WIDTH = B_KV_HEADS * HEAD_DIM
IDX_HEADS = 4
IDX_DIM = 64
TOPK_MAX = 256
ROPE_THETA = 10000.0
C_GROUPS = 4
C_WIDTH = 256
CONV_WIDTH = 31
N_MOD = 6
EPS = 1e-6

LANES = 128
CTX_ROWS = 32
CTX_PAD = CTX_ROWS - (CONV_WIDTH - 1)
VMEM_LIMIT = 56 * 1024 * 1024

OFF_A = 0
OFF_Q = OFF_A + 2 * A_WIDTH
OFF_K = OFF_Q + B_WIDTH
OFF_V = OFF_K + B_KV_WIDTH
OFF_QI = OFF_V + B_KV_WIDTH
OFF_KI = OFF_QI + IDX_HEADS * IDX_DIM
OFF_C = OFF_KI + LANES
N_IN_PAD = OFF_C + 2 * C_WIDTH

F32 = jnp.float32
BF16 = jnp.bfloat16


def _dot(a, b):
    return jnp.dot(a, b, preferred_element_type=F32)


def _dot_t(a, b):
    return lax.dot_general(a, b, (((1,), (1,)), ((), ())), preferred_element_type=F32)


def _rms(x):
    return x * lax.rsqrt(jnp.mean(x * x, axis=-1, keepdims=True) + EPS)


def _lane_group(n, group):
    shift = group.bit_length() - 1
    assert 1 << shift == group
    return lax.shift_right_logical(lax.broadcasted_iota(jnp.int32, (1, n), 1), shift)


def _group_standardize(y, group):
    n = y.shape[-1]
    gid = _lane_group(n, group)
    inv = 1.0 / group
    mean = jnp.zeros_like(y)
    for g in range(n // group):
        m = gid == g
        s = jnp.sum(jnp.where(m, y, 0.0), axis=-1, keepdims=True) * inv
        mean = jnp.where(m, s, mean)
    yc = y - mean
    sq = yc * yc
    var = jnp.zeros_like(y)
    for g in range(n // group):
        m = gid == g
        s = jnp.sum(jnp.where(m, sq, 0.0), axis=-1, keepdims=True) * inv
        var = jnp.where(m, s, var)
    return yc * lax.rsqrt(var + EPS)


def _rope(x, cos, sin_lo, sin_hi):
    parts = []
    for c in range(x.shape[-1] // LANES):
        xc = x[:, c * LANES:(c + 1) * LANES]
        up = pltpu.roll(xc, LANES - HEAD_DIM // 2, 1)
        down = pltpu.roll(xc, HEAD_DIM // 2, 1)
        parts.append(xc * cos + up * sin_lo + down * sin_hi)
    return parts[0] if len(parts) == 1 else jnp.concatenate(parts, axis=-1)


def _mod_kernel(c_ref, w_ref, b_ref, o_ref):
    c = c_ref[...]
    cond = c * jax.nn.sigmoid(c)
    o_ref[0] = _dot(cond.astype(BF16), w_ref[0].astype(BF16)) + b_ref[0]


def _mod_call(c_all, w_ada, b_ada):
    depth, d, n = w_ada.shape
    rows = c_all.shape[0]
    tn = d
    return pl.pallas_call(
        _mod_kernel,
        out_shape=jax.ShapeDtypeStruct((depth, rows, n), F32),
        grid=(depth, n // tn),
        in_specs=[
            pl.BlockSpec((rows, d), lambda l, j: (0, 0)),
            pl.BlockSpec((1, d, tn), lambda l, j: (l, 0, j)),
            pl.BlockSpec((1, 1, tn), lambda l, j: (l, 0, j)),
        ],
        out_specs=pl.BlockSpec((1, rows, tn), lambda l, j: (l, 0, j)),
        compiler_params=pltpu.CompilerParams(
            dimension_semantics=("arbitrary", "arbitrary"), vmem_limit_bytes=VMEM_LIMIT),
        name="adaln_mod",
    )(c_all, w_ada, b_ada.reshape(depth, 1, n))


def _in_kernel(x_ref, mod_ref, g1_ref, w_ref, cos_ref, slo_ref, shi_ref, wsp_ref, bsp_ref,
               wdw_ref, bdw_ref, gcn_ref, cst_ref,
               q_ref, k_ref, v_ref, qi_ref, ki_ref, wi_ref, ac_ref, av_ref, cnew_ref,
               hp_ref, *, tm, ln, d_model):
    j = pl.program_id(1)
    x = x_ref[0]
    mod = mod_ref[0]
    sh1 = mod[:, 0:d_model]
    sc1 = mod[:, d_model:2 * d_model]
    h = (_rms(x) * g1_ref[...]) * (1.0 + sc1) + sh1
    z = _dot(h.astype(BF16), w_ref[...])

    cos = cos_ref[...]
    slo = slo_ref[...]
    shi = shi_ref[...]

    q = _rope(z[:, OFF_Q:OFF_Q + B_WIDTH], cos, slo, shi)
    q_ref[0] = (q * (HEAD_DIM ** -0.5)).astype(BF16)
    k_ref[0] = _rope(z[:, OFF_K:OFF_K + B_KV_WIDTH], cos, slo, shi)
    v_ref[0] = z[:, OFF_V:OFF_V + B_KV_WIDTH]
    qi_ref[0] = _rope(z[:, OFF_QI:OFF_QI + IDX_HEADS * IDX_DIM], cos, slo, shi).astype(BF16)
    kiwi = z[:, OFF_KI:OFF_KI + LANES]
    ki_ref[0] = _rope(kiwi, cos, slo, shi)[:, 0:IDX_DIM]
    wi_ref[0] = kiwi

    za = z[:, OFF_A:OFF_A + 2 * A_WIDTH]
    za = 0.5 * za * (1.0 + lax.erf(za * (2.0 ** -0.5)))
    u = za[:, 0:A_WIDTH]
    vn = _group_standardize(za[:, A_WIDTH:2 * A_WIDTH], A_HEAD_DIM)
    av_ref[0] = vn
    vb = vn.astype(BF16)
    row = lax.broadcasted_iota(jnp.int32, (ln, ln), 0)
    col = lax.broadcasted_iota(jnp.int32, (ln, ln), 1)
    head_of_lane = _lane_group(A_WIDTH, A_HEAD_DIM)
    wsp = [jnp.where(col <= row, wsp_ref[g], 0.0).astype(BF16) for g in range(A_HEADS)]
    gated = []
    for c in range(tm // ln):
        vc = vb[c * ln:(c + 1) * ln, :]
        s = jnp.zeros((ln, A_WIDTH), F32)
        for g in range(A_HEADS):
            s = jnp.where(head_of_lane == g, _dot(wsp[g], vc), s)
        gated.append(u[c * ln:(c + 1) * ln, :] * (s + bsp_ref[...]))
    a_out = gated[0] if len(gated) == 1 else jnp.concatenate(gated, axis=0)
    ac_ref[0, :, 0:A_WIDTH] = a_out.astype(BF16)

    zc = z[:, OFF_C:OFF_C + 2 * C_WIDTH]
    hc = zc[:, 0:C_WIDTH] * jax.nn.sigmoid(zc[:, C_WIDTH:2 * C_WIDTH])

    @pl.when(j == 0)
    def _():
        hp_ref[0:CTX_ROWS, :] = cst_ref[0]

    hp_ref[CTX_ROWS:CTX_ROWS + tm, :] = hc
    rb = min(tm, 64)
    wdw = wdw_ref[...]
    conv = []
    for r0 in range(0, tm, rb):
        acc = jnp.zeros((rb, C_WIDTH), F32)
        for t in range(CONV_WIDTH):
            acc = acc + hp_ref[CTX_PAD + r0 + t:CTX_PAD + r0 + t + rb, :] * wdw[t:t + 1, :]
        conv.append(acc)
    y = (conv[0] if len(conv) == 1 else jnp.concatenate(conv, axis=0)) + bdw_ref[...]
    y = _group_standardize(y, C_WIDTH // C_GROUPS) * gcn_ref[...]
    ac_ref[0, :, A_WIDTH:A_WIDTH + C_WIDTH] = (y * jax.nn.sigmoid(y)).astype(BF16)

    tail = hp_ref[tm:tm + CTX_ROWS, :]
    cnew_ref[0] = tail
    hp_ref[0:CTX_ROWS, :] = tail


def _in_call(x, mod, layer, mod_row0, g1, w_in_p, tabs, wsp, bsp_tab, wdw, bdw, gcn, cst, tm):
    b, t, d = x.shape
    ln = min(t, A_CHUNK)
    cos, slo, shi = tabs
    kern = functools.partial(_in_kernel, tm=tm, ln=ln, d_model=d)
    row3 = lambda bi, j: (bi, j, 0)
    full2 = lambda bi, j: (0, 0)
    lay3 = lambda bi, j: (layer, 0, 0)
    out_shape = (
        jax.ShapeDtypeStruct((b, t, B_WIDTH), BF16),
        jax.ShapeDtypeStruct((b, t, B_KV_WIDTH), F32),
        jax.ShapeDtypeStruct((b, t, B_KV_WIDTH), F32),
        jax.ShapeDtypeStruct((b, t, IDX_HEADS * IDX_DIM), BF16),
        jax.ShapeDtypeStruct((b, t, IDX_DIM), F32),
        jax.ShapeDtypeStruct((b, t, LANES), F32),
        jax.ShapeDtypeStruct((b, t, A_WIDTH + C_WIDTH), BF16),
        jax.ShapeDtypeStruct((b, t, A_WIDTH), F32),
        jax.ShapeDtypeStruct((b, CTX_ROWS, C_WIDTH), F32),
    )
    out_specs = (
        pl.BlockSpec((1, tm, B_WIDTH), row3),
        pl.BlockSpec((1, tm, B_KV_WIDTH), row3),
        pl.BlockSpec((1, tm, B_KV_WIDTH), row3),
        pl.BlockSpec((1, tm, IDX_HEADS * IDX_DIM), row3),
        pl.BlockSpec((1, tm, IDX_DIM), row3),
        pl.BlockSpec((1, tm, LANES), row3),
        pl.BlockSpec((1, tm, A_WIDTH + C_WIDTH), row3),
        pl.BlockSpec((1, tm, A_WIDTH), row3),
        pl.BlockSpec((1, CTX_ROWS, C_WIDTH), lambda bi, j: (bi, 0, 0)),
    )
    in_specs = [
        pl.BlockSpec((1, tm, d), row3),
        pl.BlockSpec((None, 1, 1, N_MOD * d), lambda bi, j: (layer, mod_row0 + bi, 0, 0)),
        pl.BlockSpec((None, 1, d), lay3),
        pl.BlockSpec((None, d, N_IN_PAD), lay3),
        pl.BlockSpec((tm, LANES), lambda bi, j: (j, 0)),
        pl.BlockSpec((tm, LANES), lambda bi, j: (j, 0)),
        pl.BlockSpec((tm, LANES), lambda bi, j: (j, 0)),
        pl.BlockSpec((None, A_HEADS, ln, ln), lambda bi, j: (layer, 0, 0, 0)),
        pl.BlockSpec((None, ln, A_WIDTH), lay3),
        pl.BlockSpec((None, CONV_WIDTH, C_WIDTH), lay3),
        pl.BlockSpec((None, 1, C_WIDTH), lay3),
        pl.BlockSpec((None, 1, C_WIDTH), lay3),
        pl.BlockSpec((1, CTX_ROWS, C_WIDTH), lambda bi, j: (bi, 0, 0)),
    ]
    del full2
    return pl.pallas_call(
        kern,
        out_shape=out_shape,
        grid=(b, t // tm),
        in_specs=in_specs,
        out_specs=out_specs,
        scratch_shapes=[pltpu.VMEM((tm + CTX_ROWS, C_WIDTH), F32)],
        compiler_params=pltpu.CompilerParams(
            dimension_semantics=("arbitrary", "arbitrary"), vmem_limit_bytes=VMEM_LIMIT),
        name="in_proj_mixers_ac",
    )(x, mod, g1, w_in_p, cos, slo, shi, wsp, bsp_tab, wdw, bdw, gcn, cst)


def _attn_kernel(q_ref, qi_ref, wi_ref, k_ref, v_ref, ki_ref, o_ref, key_ref, bias_ref,
                 *, tq, n_keys, q_start, topk):
    lk = k_ref.shape[1]
    j = pl.program_id(1)
    qchunk = lax.shift_right_logical(
        q_start + j * tq + lax.broadcasted_iota(jnp.int32, (tq, 1), 0), CHUNK_SHIFT)

    def allowed_at(k0, width):
        kpos = k0 + lax.broadcasted_iota(jnp.int32, (1, width), 1)
        return (lax.shift_right_logical(kpos, CHUNK_SHIFT) <= qchunk) & (kpos < n_keys)

    kib = ki_ref[0].astype(BF16)
    qi = qi_ref[0]
    wi = wi_ref[0]
    score = jnp.zeros((tq, lk), F32)
    for h in range(IDX_HEADS):
        d = _dot_t(qi[:, h * IDX_DIM:(h + 1) * IDX_DIM], kib)
        score = score + wi[:, IDX_DIM + h:IDX_DIM + h + 1] * jnp.maximum(d, 0.0)
    score = jnp.where(score == 0.0, 0.0, score)
    score = jnp.where(allowed_at(0, lk), score, -jnp.inf)

    bits = pltpu.bitcast(score, jnp.int32)
    key_ref[...] = bits ^ ((bits >> 31) & jnp.int32(0x7FFFFFFF))
    int_min = jnp.int32(-2 ** 31)

    def bit_step(i, thr):
        cand = thr + lax.shift_left(jnp.int32(1), jnp.int32(31) - i)
        cnt = jnp.sum((key_ref[...] >= cand).astype(F32), axis=-1, keepdims=True)
        return jnp.where(cnt >= topk, cand, thr)

    thr = lax.fori_loop(0, 32, bit_step, jnp.full((tq, 1), int_min, jnp.int32))

    need = topk - jnp.sum((key_ref[...] > thr).astype(F32), axis=-1, keepdims=True)
    tri = (lax.broadcasted_iota(jnp.int32, (LANES, LANES), 0)
           <= lax.broadcasted_iota(jnp.int32, (LANES, LANES), 1)).astype(F32).astype(BF16)
    seen = jnp.zeros((tq, 1), F32)
    for c in range(lk // LANES):
        sl = slice(c * LANES, (c + 1) * LANES)
        key_c = key_ref[:, sl]
        eq_c = key_c == thr
        eq_f = eq_c.astype(F32)
        rank = _dot(eq_f.astype(BF16), tri) + seen
        chosen = ((key_c > thr) | (eq_c & (rank <= need))) & allowed_at(c * LANES, LANES)
        bias_ref[:, sl] = jnp.where(chosen, 0.0, -jnp.inf)
        seen = seen + jnp.sum(eq_f, axis=-1, keepdims=True)

    q = q_ref[0]
    kb = k_ref[0].astype(BF16)
    vb = v_ref[0].astype(BF16)
    bias = bias_ref[...]
    group = B_HEADS // B_KV_HEADS
    outs = []
    for n in range(B_KV_HEADS):
        kn = kb[:, n * HEAD_DIM:(n + 1) * HEAD_DIM]
        vn = vb[:, n * HEAD_DIM:(n + 1) * HEAD_DIM]
        for g in range(group):
            hq = n * group + g
            logits = _dot_t(q[:, hq * HEAD_DIM:(hq + 1) * HEAD_DIM], kn) + bias
            p = jnp.exp(logits - jnp.max(logits, axis=-1, keepdims=True))
            den = jnp.sum(p, axis=-1, keepdims=True)
            outs.append(_dot(p.astype(BF16), vn) / den)
    o_ref[0] = jnp.concatenate(outs, axis=-1).astype(BF16)


def _attn_call(q, qi, wi, k_all, v_all, ki_all, n_keys, q_start, tq):
    b, t, _ = q.shape
    lk = k_all.shape[1]
    topk = min(TOPK_MAX, n_keys // 4)
    kern = functools.partial(_attn_kernel, tq=tq, n_keys=n_keys, q_start=q_start, topk=topk)
    row3 = lambda bi, j: (bi, j, 0)
    bat3 = lambda bi, j: (bi, 0, 0)
    return pl.pallas_call(
        kern,
        out_shape=jax.ShapeDtypeStruct((b, t, B_WIDTH), BF16),
        grid=(b, t // tq),
        in_specs=[
            pl.BlockSpec((1, tq, B_WIDTH), row3),
            pl.BlockSpec((1, tq, IDX_HEADS * IDX_DIM), row3),
            pl.BlockSpec((1, tq, LANES), row3),
            pl.BlockSpec((1, lk, B_KV_WIDTH), bat3),
            pl.BlockSpec((1, lk, B_KV_WIDTH), bat3),
            pl.BlockSpec((1, lk, IDX_DIM), bat3),
        ],
        out_specs=pl.BlockSpec((1, tq, B_WIDTH), row3),
        scratch_shapes=[pltpu.VMEM((tq, lk), jnp.int32), pltpu.VMEM((tq, lk), F32)],
        compiler_params=pltpu.CompilerParams(
            dimension_semantics=("arbitrary", "arbitrary"), vmem_limit_bytes=VMEM_LIMIT),
        name="dsa_attention",
    )(q, qi, wi, k_all, v_all, ki_all)


FOLD_ROWS = 64


def _col_reduce(x, op):
    r = x.shape[0]
    if r > FOLD_ROWS and r % FOLD_ROWS == 0:
        acc = x[:FOLD_ROWS]
        for c in range(1, r // FOLD_ROWS):
            acc = op(acc, x[c * FOLD_ROWS:(c + 1) * FOLD_ROWS])
        x, r = acc, FOLD_ROWS
    while r % 16 == 0:
        x = op(x[:r // 2], x[r // 2:])
        r //= 2
    return (jnp.sum if op is jnp.add else jnp.max)(x, axis=0, keepdims=True)


def _attn_t_kernel(*refs, tq, n_keys, q_start, q_block0, topk, aliased):
    q_ref, qi_ref, wi_ref, k_ref, v_ref, ki_ref = refs[:6]
    o_ref, key_ref, bias_ref = refs[7:] if aliased else refs[6:]
    lk = k_ref.shape[1]
    j = pl.program_id(1)
    qchunk = lax.shift_right_logical(
        q_start + (q_block0 + j) * tq + lax.broadcasted_iota(jnp.int32, (1, tq), 1), CHUNK_SHIFT)

    def allowed_at(k0, rows):
        kpos = k0 + lax.broadcasted_iota(jnp.int32, (rows, 1), 0)
        return (lax.shift_right_logical(kpos, CHUNK_SHIFT) <= qchunk) & (kpos < n_keys)

    kib = ki_ref[0].astype(BF16)
    qi = qi_ref[0]
    wi_t = wi_ref[0].T
    score = jnp.zeros((lk, tq), F32)
    for h in range(IDX_HEADS):
        d = _dot_t(kib, qi[:, h * IDX_DIM:(h + 1) * IDX_DIM])
        score = score + wi_t[IDX_DIM + h:IDX_DIM + h + 1, :] * jnp.maximum(d, 0.0)
    score = jnp.where(score == 0.0, 0.0, score)
    score = jnp.where(allowed_at(0, lk), score, -jnp.inf)

    bits = pltpu.bitcast(score, jnp.int32)
    key_ref[...] = bits ^ ((bits >> 31) & jnp.int32(0x7FFFFFFF))
    int_min = jnp.int32(-2 ** 31)

    def bit_step(i, thr):
        cand = thr + lax.shift_left(jnp.int32(1), jnp.int32(31) - i)
        cnt = _col_reduce((key_ref[...] >= cand).astype(F32), jnp.add)
        return jnp.where(cnt >= topk, cand, thr)

    thr = lax.fori_loop(0, 32, bit_step, jnp.full((1, tq), int_min, jnp.int32))

    need = topk - _col_reduce((key_ref[...] > thr).astype(F32), jnp.add)
    tril = (lax.broadcasted_iota(jnp.int32, (LANES, LANES), 0)
            >= lax.broadcasted_iota(jnp.int32, (LANES, LANES), 1)).astype(F32).astype(BF16)
    seen = jnp.zeros((1, tq), F32)
    for c in range(lk // LANES):
        sl = slice(c * LANES, (c + 1) * LANES)
        key_c = key_ref[sl, :]
        eq_c = key_c == thr
        eq_f = eq_c.astype(F32)
        rank = _dot(tril, eq_f.astype(BF16)) + seen
        chosen = ((key_c > thr) | (eq_c & (rank <= need))) & allowed_at(c * LANES, LANES)
        bias_ref[sl, :] = jnp.where(chosen, 0.0, -jnp.inf)
        seen = seen + _col_reduce(eq_f, jnp.add)

    q = q_ref[0]
    kb = k_ref[0].astype(BF16)
    vt = v_ref[0].T.astype(BF16)
    bias = bias_ref[...]
    group = B_HEADS // B_KV_HEADS
    outs = []
    for n in range(B_KV_HEADS):
        kn = kb[:, n * HEAD_DIM:(n + 1) * HEAD_DIM]
        vtn = vt[n * HEAD_DIM:(n + 1) * HEAD_DIM, :]
        for g in range(group):
            hq = n * group + g
            logits = _dot_t(kn, q[:, hq * HEAD_DIM:(hq + 1) * HEAD_DIM]) + bias
            p = jnp.exp(logits - _col_reduce(logits, jnp.maximum))
            den = _col_reduce(p, jnp.add)
            outs.append(_dot(vtn, p.astype(BF16)) / den)
    o_ref[0] = jnp.concatenate(outs, axis=0).T.astype(BF16)


def _attn_t_call(q, qi, wi, k, v, ki, prev, n_keys, q_start, q_block0, n_q_blocks, lk, tq):
    b, t, _ = q.shape
    topk = min(TOPK_MAX, n_keys // 4)
    aliased = prev is not None
    kern = functools.partial(_attn_t_kernel, tq=tq, n_keys=n_keys, q_start=q_start,
                             q_block0=q_block0, topk=topk, aliased=aliased)
    row3 = lambda bi, j: (bi, q_block0 + j, 0)
    bat3 = lambda bi, j: (bi, 0, 0)
    in_specs = [
        pl.BlockSpec((1, tq, B_WIDTH), row3),
        pl.BlockSpec((1, tq, IDX_HEADS * IDX_DIM), row3),
        pl.BlockSpec((1, tq, LANES), row3),
        pl.BlockSpec((1, lk, B_KV_WIDTH), bat3),
        pl.BlockSpec((1, lk, B_KV_WIDTH), bat3),
        pl.BlockSpec((1, lk, IDX_DIM), bat3),
    ]
    args = [q, qi, wi, k, v, ki]
    if aliased:
        in_specs.append(pl.BlockSpec(memory_space=pl.ANY))
        args.append(prev)
    return pl.pallas_call(
        kern,
        out_shape=jax.ShapeDtypeStruct((b, t, B_WIDTH), BF16),
        grid=(b, n_q_blocks),
        in_specs=in_specs,
        out_specs=pl.BlockSpec((1, tq, B_WIDTH), row3),
        scratch_shapes=[pltpu.VMEM((lk, tq), jnp.int32), pltpu.VMEM((lk, tq), F32)],
        input_output_aliases={6: 0} if aliased else {},
        compiler_params=pltpu.CompilerParams(
            dimension_semantics=("arbitrary", "arbitrary"), vmem_limit_bytes=VMEM_LIMIT),
        name="dsa_attention_t",
    )(*args)


def _attn_prompt(q, qi, wi, k, v, ki, tq):
    t = q.shape[1]
    nqb = t // tq
    per_seg = 2 if nqb % 2 == 0 else 1
    bo = None
    for s in range(nqb // per_seg):
        bo = _attn_t_call(q, qi, wi, k, v, ki, bo, t, 0, s * per_seg, per_seg,
                          (s + 1) * per_seg * tq, tq)
    return bo


def _ffn_kernel(x_ref, ac_ref, bo_ref, mod_ref, g2_ref, woac_ref, wob_ref, wg_ref, wu_ref, wd_ref,
                gf_ref, o_ref, *, d_model, ff_chunk, final):
    x = x_ref[0]
    mod = mod_ref[0]
    ga1 =mod[:, 2 * d_model:3 * d_model]
    sh2 = mod[:, 3 * d_model:4 * d_model]
    sc2 = mod[:, 4 * d_model:5 * d_model]
    ga2 = mod[:, 5 * d_model:6 * d_model]
    mix = _dot(ac_ref[0], woac_ref[...]) + _dot(bo_ref[0], wob_ref[...])
    x1 = x + ga1 * mix
    h2 = ((_rms(x1) * g2_ref[...]) * (1.0 + sc2) + sh2).astype(BF16)
    d_ff = wg_ref.shape[1]
    acc = jnp.zeros_like(x1)
    for c0 in range(0, d_ff, ff_chunk):
        gate = _dot(h2, wg_ref[:, c0:c0 + ff_chunk])
        up = _dot(h2, wu_ref[:, c0:c0 + ff_chunk])
        act = (gate * jax.nn.sigmoid(gate)) * up
        acc = acc + _dot(act.astype(BF16), wd_ref[c0:c0 + ff_chunk, :])
    x2 = x1 + ga2 * acc
    if final:
        o_ref[0] = _rms(x2) * gf_ref[...]
    else:
        o_ref[0] = x2


def _ffn_call(x, ac, bo, mod, layer, mod_row0, g2, woac, wob, wg, wu, wd, gf, tm, final):
    b, t, d = x.shape
    d_ff = wg.shape[2]
    ff_chunk = d_ff // 2 if (d_ff // 2) % LANES == 0 else d_ff
    kern = functools.partial(_ffn_kernel, d_model=d, ff_chunk=ff_chunk, final=final)
    row3 = lambda bi, j: (bi, j, 0)
    lay3 = lambda bi, j: (layer, 0, 0)
    once = pl.Buffered(1)
    return pl.pallas_call(
        kern,
        out_shape=jax.ShapeDtypeStruct((b, t, d), F32),
        grid=(b, t // tm),
        in_specs=[
            pl.BlockSpec((1, tm, d), row3),
            pl.BlockSpec((1, tm, A_WIDTH + C_WIDTH), row3),
            pl.BlockSpec((1, tm, B_WIDTH), row3),
            pl.BlockSpec((None, 1, 1, N_MOD * d), lambda bi, j: (layer, mod_row0 + bi, 0, 0)),
            pl.BlockSpec((None, 1, d), lay3),
            pl.BlockSpec((None, A_WIDTH + C_WIDTH, d), lay3, pipeline_mode=once),
            pl.BlockSpec((None, B_WIDTH, d), lay3, pipeline_mode=once),
            pl.BlockSpec((None, d, d_ff), lay3, pipeline_mode=once),
            pl.BlockSpec((None, d, d_ff), lay3, pipeline_mode=once),
            pl.BlockSpec((None, d_ff, d), lay3, pipeline_mode=once),
            pl.BlockSpec((1, d), lambda bi, j: (0, 0)),
        ],
        out_specs=pl.BlockSpec((1, tm, d), row3),
        compiler_params=pltpu.CompilerParams(
            dimension_semantics=("arbitrary", "arbitrary"), vmem_limit_bytes=VMEM_LIMIT),
        name="out_proj_ffn",
    )(x, ac, bo, mod, g2, woac, wob, wg, wu, wd, gf)


def _rope_tables(start, t):
    half = HEAD_DIM // 2
    inv = jnp.power(ROPE_THETA, -2.0 * jnp.arange(half, dtype=F32) / HEAD_DIM)
    ang = (start + jnp.arange(t)).astype(F32)[:, None] * inv[None, :]
    cos = jnp.cos(ang)
    sin = jnp.sin(ang)
    zero = jnp.zeros_like(sin)
    reps = LANES // HEAD_DIM
    return (jnp.tile(jnp.concatenate([cos, cos], axis=-1), (1, reps)),
            jnp.tile(jnp.concatenate([-sin, zero], axis=-1), (1, reps)),
            jnp.tile(jnp.concatenate([zero, sin], axis=-1), (1, reps)))


def _pad_w_in(w_in):
    n_head = OFF_KI + IDX_DIM + IDX_HEADS
    depth, d, _ = w_in.shape
    pad = jnp.zeros((depth, d, OFF_C - n_head), w_in.dtype)
    return jnp.concatenate([w_in[:, :, :n_head], pad, w_in[:, :, n_head:]], axis=-1).astype(BF16)


def _trunk(x, mod, mod_row0, cache, params, q_start, tm, tq):
    (g_norm1, g_norm2, w_in_p, w_spatial, b_spatial, w_dw, b_dw, g_cnorm,
     woac, wob, wg, wu, wd, g_final) = params
    b, t, d = x.shape
    depth = w_in_p.shape[0]
    ln = min(t, A_CHUNK)
    tabs = _rope_tables(q_start, t)
    wsp = w_spatial[:, :, :ln, :ln]
    bsp_tab = jnp.repeat(jnp.swapaxes(b_spatial[:, :, :ln], 1, 2), A_HEAD_DIM, axis=-1)
    ks, vs, kis, convs, avs = [], [], [], [], []
    for l in range(depth):
        if cache is None:
            cst = jnp.zeros((b, CTX_ROWS, C_WIDTH), F32)
        else:
            cst = jnp.pad(cache[3][l], ((0, 0), (CTX_PAD, 0), (0, 0)))
        q, k, v, qi, ki, wi, ac, av, cnew = _in_call(
            x, mod, l, mod_row0, g_norm1, w_in_p, tabs, wsp, bsp_tab, w_dw, b_dw, g_cnorm, cst, tm)
        if cache is None:
            bo = _attn_prompt(q, qi, wi, k, v, ki, tq)
        else:
            past = cache[0].shape[2]
            n_keys = past + t
            lk = -(-n_keys // LANES) * LANES
            grow = lambda old, new: jnp.pad(jnp.concatenate([old, new], axis=1),
                                            ((0, 0), (0, lk - n_keys), (0, 0)))
            k_all = grow(cache[0][l].reshape(b, past, B_KV_WIDTH), k)
            v_all = grow(cache[1][l].reshape(b, past, B_KV_WIDTH), v)
            ki_all = grow(cache[2][l], ki)
            bo = _attn_call(q, qi, wi, k_all, v_all, ki_all, n_keys, q_start, tq)
        x = _ffn_call(x, ac, bo, mod, l, mod_row0, g_norm2, woac, wob, wg, wu, wd, g_final,
                      tm, final=(l == depth - 1))
        ks.append(k.reshape(b, t, B_KV_HEADS, HEAD_DIM))
        vs.append(v.reshape(b, t, B_KV_HEADS, HEAD_DIM))
        kis.append(ki)
        convs.append(cnew[:, CTX_PAD:, :])
        avs.append(av)
    return x, jnp.stack(ks), jnp.stack(vs), jnp.stack(kis), jnp.stack(convs), jnp.stack(avs)


def kernel(x_prompt, x_sample, cache_k, cache_v, cache_kidx, state_conv, c_prompt, c_sample,
           w_ada, b_ada, g_norm1, g_norm2, w_in, w_spatial, b_spatial, w_dw, b_dw, g_cnorm,
           w_out, w_gate, w_up, w_down, g_final):
    depth, d = g_norm1.shape
    nb_p, t_p, _ = x_prompt.shape
    nb_s, t_s, _ = x_sample.shape
    past = cache_k.shape[2]

    rows = nb_p + nb_s
    rows_pad = -(-rows // 8) * 8
    c_all = jnp.pad(jnp.concatenate([c_prompt, c_sample], axis=0), ((0, rows_pad - rows), (0, 0)))
    mod = _mod_call(c_all, w_ada, b_ada).reshape(depth, rows_pad, 1, N_MOD * d)

    w_out_b = w_out.astype(BF16)
    params = (
        g_norm1.reshape(depth, 1, d), g_norm2.reshape(depth, 1, d), _pad_w_in(w_in),
        w_spatial, b_spatial, w_dw, b_dw.reshape(depth, 1, C_WIDTH), g_cnorm.reshape(depth, 1, C_WIDTH),
        jnp.concatenate([w_out_b[:, :A_WIDTH], w_out_b[:, A_WIDTH + B_WIDTH:]], axis=1),
        w_out_b[:, A_WIDTH:A_WIDTH + B_WIDTH],
        w_gate.astype(BF16), w_up.astype(BF16), w_down.astype(BF16), g_final.reshape(1, d),
    )

    y_p, p_k, p_v, p_ki, p_conv, _ = _trunk(
        x_prompt, mod, 0, None, params, 0, tm=min(t_p, 512), tq=min(t_p, 128))
    y_s, s_k, s_v, s_ki, s_conv, s_av = _trunk(
        x_sample, mod, nb_p, (cache_k, cache_v, cache_kidx, state_conv), params, past,
        tm=t_s, tq=t_s)
    return (y_p, y_s, p_k, p_v, p_ki, p_conv, s_k, s_v, s_ki, s_conv, s_av)
```

```python
import functools

import jax
import jax.numpy as jnp
from jax import lax
from jax.experimental import pallas as pl
from jax.experimental.pallas import tpu as pltpu

CHUNK = 64
CHUNK_SHIFT = 6
assert 1 << CHUNK_SHIFT == CHUNK
A_HEADS = 4
A_HEAD_DIM = 64
A_WIDTH = A_HEADS * A_HEAD_DIM
A_CHUNK = 128
B_HEADS = 8
B_KV_HEADS = 2
HEAD_DIM = 64
B_WIDTH = B_HEADS * HEAD_DIM
B_KV_WIDTH = B_KV_HEADS * HEAD_DIM
IDX_HEADS = 4
IDX_DIM = 64
TOPK_MAX = 256
ROPE_THETA = 10000.0
C_GROUPS = 4
C_WIDTH = 256
CONV_WIDTH = 31
N_MOD = 6
EPS = 1e-6

LANES = 128
SUBLANES = 8
CTX_ROWS = 32
CTX_PAD = CTX_ROWS - (CONV_WIDTH - 1)
VMEM_LIMIT = 56 * 1024 * 1024

OFF_A = 0
OFF_Q = OFF_A + 2 * A_WIDTH
OFF_K = OFF_Q + B_WIDTH
OFF_V = OFF_K + B_KV_WIDTH
OFF_QI = OFF_V + B_KV_WIDTH
OFF_KI = OFF_QI + IDX_HEADS * IDX_DIM
OFF_C = OFF_KI + LANES
N_IN_PAD = OFF_C + 2 * C_WIDTH

LOG2E = 1.4426950408889634
I16 = jnp.int16
I16_MIN = -2 ** 15
KEY_MIN_FINITE = -2139095040

F32 = jnp.float32
BF16 = jnp.bfloat16


def _dot(a, b):
    return jnp.dot(a, b, preferred_element_type=F32)


def _dot_t(a, b):
    return lax.dot_general(a, b, (((1,), (1,)), ((), ())), preferred_element_type=F32)


def _rms(x):
    return x * lax.rsqrt(jnp.mean(x * x, axis=-1, keepdims=True) + EPS)


def _lane_group(n, group):
    shift = group.bit_length() - 1
    assert 1 << shift == group
    return lax.shift_right_logical(lax.broadcasted_iota(jnp.int32, (1, n), 1), shift)


def _group_standardize(y, group):
    n = y.shape[-1]
    gid = _lane_group(n, group)
    inv = 1.0 / group
    mean = jnp.zeros_like(y)
    for g in range(n // group):
        m = gid == g
        s = jnp.sum(jnp.where(m, y, 0.0), axis=-1, keepdims=True) * inv
        mean = jnp.where(m, s, mean)
    yc = y - mean
    sq = yc * yc
    var = jnp.zeros_like(y)
    for g in range(n // group):
        m = gid == g
        s = jnp.sum(jnp.where(m, sq, 0.0), axis=-1, keepdims=True) * inv
        var = jnp.where(m, s, var)
    return yc * lax.rsqrt(var + EPS)


def _rope(x, cos, sin_lo, sin_hi):
    parts = []
    for c in range(x.shape[-1] // LANES):
        xc = x[:, c * LANES:(c + 1) * LANES]
        up = pltpu.roll(xc, LANES - HEAD_DIM // 2, 1)
        down = pltpu.roll(xc, HEAD_DIM // 2, 1)
        parts.append(xc * cos + up * sin_lo + down * sin_hi)
    return parts[0] if len(parts) == 1 else jnp.concatenate(parts, axis=-1)


def _mod_kernel(c_ref, w_ref, b_ref, o_ref):
    c = c_ref[...]
    cond = c * jax.nn.sigmoid(c)
    o_ref[0] = _dot(cond.astype(BF16), w_ref[0].astype(BF16)) + b_ref[0]


def _mod_call(c_all, w_ada, b_ada):
    depth, d, n = w_ada.shape
    rows = c_all.shape[0]
    tn = d
    return pl.pallas_call(
        _mod_kernel,
        out_shape=jax.ShapeDtypeStruct((depth, rows, n), F32),
        grid=(depth, n // tn),
        in_specs=[
            pl.BlockSpec((rows, d), lambda l, j: (0, 0)),
            pl.BlockSpec((1, d, tn), lambda l, j: (l, 0, j)),
            pl.BlockSpec((1, 1, tn), lambda l, j: (l, 0, j)),
        ],
        out_specs=pl.BlockSpec((1, rows, tn), lambda l, j: (l, 0, j)),
        compiler_params=pltpu.CompilerParams(
            dimension_semantics=("arbitrary", "arbitrary"), vmem_limit_bytes=VMEM_LIMIT),
        name="adaln_mod",
    )(c_all, w_ada, b_ada.reshape(depth, 1, n))


def _in_kernel(x_ref, mod_ref, g1_ref, w_ref, cos_ref, slo_ref, shi_ref, wsp_ref, bsp_ref,
               wdw_ref, bdw_ref, gcn_ref, cst_ref,
               q_ref, k_ref, v_ref, qi_ref, ki_ref, wi_ref, ac_ref, av_ref, cnew_ref,
               hp_ref, sh_ref, *, tm, ln, d_model):
    j = pl.program_id(1)
    x = x_ref[0]
    mod = mod_ref[0]
    sh1 = mod[:, 0:d_model]
    sc1 = mod[:, d_model:2 * d_model]
    h = (_rms(x) * g1_ref[...]) * (1.0 + sc1) + sh1
    z = _dot(h.astype(BF16), w_ref[...])

    cos = cos_ref[...]
    slo = slo_ref[...]
    shi = shi_ref[...]

    q = _rope(z[:, OFF_Q:OFF_Q + B_WIDTH], cos, slo, shi)
    q_ref[0] = (q * (LOG2E * HEAD_DIM ** -0.5)).astype(BF16)
    k_ref[0] = _rope(z[:, OFF_K:OFF_K + B_KV_WIDTH], cos, slo, shi)
    v_ref[0] = z[:, OFF_V:OFF_V + B_KV_WIDTH]
    qi_ref[0] = _rope(z[:, OFF_QI:OFF_QI + IDX_HEADS * IDX_DIM], cos, slo, shi).astype(BF16)
    kiwi = z[:, OFF_KI:OFF_KI + LANES]
    ki_ref[0] = _rope(kiwi, cos, slo, shi)[:, 0:IDX_DIM]
    wi_ref[0] = kiwi

    za = z[:, OFF_A:OFF_A + 2 * A_WIDTH]
    za = 0.5 * za * (1.0 + lax.erf(za * (2.0 ** -0.5)))
    u = za[:, 0:A_WIDTH]
    vn = _group_standardize(za[:, A_WIDTH:2 * A_WIDTH], A_HEAD_DIM)
    av_ref[0] = vn
    vb = vn.astype(BF16)
    row = lax.broadcasted_iota(jnp.int32, (ln, ln), 0)
    col = lax.broadcasted_iota(jnp.int32, (ln, ln), 1)
    head_of_lane = _lane_group(A_WIDTH, A_HEAD_DIM)
    wsp = [jnp.where(col <= row, wsp_ref[g], 0.0).astype(BF16) for g in range(A_HEADS)]
    gated = []
    for c in range(tm // ln):
        vc = vb[c * ln:(c + 1) * ln, :]
        s = jnp.zeros((ln, A_WIDTH), F32)
        for g in range(A_HEADS):
            s = jnp.where(head_of_lane == g, _dot(wsp[g], vc), s)
        gated.append(u[c * ln:(c + 1) * ln, :] * (s + bsp_ref[...]))
    a_out = gated[0] if len(gated) == 1 else jnp.concatenate(gated, axis=0)
    ac_ref[0, :, 0:A_WIDTH] = a_out.astype(BF16)

    zc = z[:, OFF_C:OFF_C + 2 * C_WIDTH]
    hc = zc[:, 0:C_WIDTH] * jax.nn.sigmoid(zc[:, C_WIDTH:2 * C_WIDTH])

    @pl.when(j == 0)
    def _():
        hp_ref[0:CTX_ROWS, :] = cst_ref[0]

    hp_ref[CTX_ROWS:CTX_ROWS + tm, :] = hc
    span = tm + CTX_ROWS - SUBLANES
    for s in range(1, SUBLANES):
        sh_ref[s - 1, 0:span, :] = hp_ref[s:s + span, :]
    rb = min(tm, 64)
    wdw = wdw_ref[...]
    conv = []
    for r0 in range(0, tm, rb):
        acc = jnp.zeros((rb, C_WIDTH), F32)
        for t in range(CONV_WIDTH):
            phase = (CTX_PAD + t) % SUBLANES
            base = CTX_PAD + t - phase + r0
            rows = hp_ref[base:base + rb, :] if phase == 0 else sh_ref[phase - 1, base:base + rb, :]
            acc = acc + rows * wdw[t:t + 1, :]
        conv.append(acc)
    y = (conv[0] if len(conv) == 1 else jnp.concatenate(conv, axis=0)) + bdw_ref[...]
    y = _group_standardize(y, C_WIDTH // C_GROUPS) * gcn_ref[...]
    ac_ref[0, :, A_WIDTH:A_WIDTH + C_WIDTH] = (y * jax.nn.sigmoid(y)).astype(BF16)

    tail = hp_ref[tm:tm + CTX_ROWS, :]
    cnew_ref[0] = tail
    hp_ref[0:CTX_ROWS, :] = tail


def _in_call(x, mod, layer, mod_row0, g1, w_in_p, tabs, wsp, bsp_tab, wdw, bdw, gcn, cst, tm):
    b, t, d = x.shape
    ln = min(t, A_CHUNK)
    cos, slo, shi = tabs
    kern = functools.partial(_in_kernel, tm=tm, ln=ln, d_model=d)
    row3 = lambda bi, j: (bi, j, 0)
    full2 = lambda bi, j: (0, 0)
    lay3 = lambda bi, j: (layer, 0, 0)
    out_shape = (
        jax.ShapeDtypeStruct((b, t, B_WIDTH), BF16),
        jax.ShapeDtypeStruct((b, t, B_KV_WIDTH), F32),
        jax.ShapeDtypeStruct((b, t, B_KV_WIDTH), F32),
        jax.ShapeDtypeStruct((b, t, IDX_HEADS * IDX_DIM), BF16),
        jax.ShapeDtypeStruct((b, t, IDX_DIM), F32),
        jax.ShapeDtypeStruct((b, t, LANES), F32),
        jax.ShapeDtypeStruct((b, t, A_WIDTH + C_WIDTH), BF16),
        jax.ShapeDtypeStruct((b, t, A_WIDTH), F32),
        jax.ShapeDtypeStruct((b, CTX_ROWS, C_WIDTH), F32),
    )
    out_specs = (
        pl.BlockSpec((1, tm, B_WIDTH), row3),
        pl.BlockSpec((1, tm, B_KV_WIDTH), row3),
        pl.BlockSpec((1, tm, B_KV_WIDTH), row3),
        pl.BlockSpec((1, tm, IDX_HEADS * IDX_DIM), row3),
        pl.BlockSpec((1, tm, IDX_DIM), row3),
        pl.BlockSpec((1, tm, LANES), row3),
        pl.BlockSpec((1, tm, A_WIDTH + C_WIDTH), row3),
        pl.BlockSpec((1, tm, A_WIDTH), row3),
        pl.BlockSpec((1, CTX_ROWS, C_WIDTH), lambda bi, j: (bi, 0, 0)),
    )
    in_specs = [
        pl.BlockSpec((1, tm, d), row3),
        pl.BlockSpec((None, 1, 1, N_MOD * d), lambda bi, j: (layer, mod_row0 + bi, 0, 0)),
        pl.BlockSpec((None, 1, d), lay3),
        pl.BlockSpec((None, d, N_IN_PAD), lay3),
        pl.BlockSpec((tm, LANES), lambda bi, j: (j, 0)),
        pl.BlockSpec((tm, LANES), lambda bi, j: (j, 0)),
        pl.BlockSpec((tm, LANES), lambda bi, j: (j, 0)),
        pl.BlockSpec((None, A_HEADS, ln, ln), lambda bi, j: (layer, 0, 0, 0)),
        pl.BlockSpec((None, ln, A_WIDTH), lay3),
        pl.BlockSpec((None, CONV_WIDTH, C_WIDTH), lay3),
        pl.BlockSpec((None, 1, C_WIDTH), lay3),
        pl.BlockSpec((None, 1, C_WIDTH), lay3),
        pl.BlockSpec((1, CTX_ROWS, C_WIDTH), lambda bi, j: (bi, 0, 0)),
    ]
    del full2
    return pl.pallas_call(
        kern,
        out_shape=out_shape,
        grid=(b, t // tm),
        in_specs=in_specs,
        out_specs=out_specs,
        scratch_shapes=[pltpu.VMEM((tm + CTX_ROWS, C_WIDTH), F32),
                        pltpu.VMEM((SUBLANES - 1, tm + CTX_ROWS, C_WIDTH), F32)],
        compiler_params=pltpu.CompilerParams(
            dimension_semantics=("arbitrary", "arbitrary"), vmem_limit_bytes=VMEM_LIMIT),
        name="in_proj_mixers_ac",
    )(x, mod, g1, w_in_p, cos, slo, shi, wsp, bsp_tab, wdw, bdw, gcn, cst)


def _attn_kernel(q_ref, qi_ref, wi_ref, k_ref, v_ref, ki_ref, o_ref, key_ref, bias_ref,
                 *, tq, n_keys, q_start, topk):
    lk = k_ref.shape[1]
    j = pl.program_id(1)
    qchunk = lax.shift_right_logical(
        q_start + j * tq + lax.broadcasted_iota(jnp.int32, (tq, 1), 0), CHUNK_SHIFT)

    def allowed_at(k0, width):
        kpos = k0 + lax.broadcasted_iota(jnp.int32, (1, width), 1)
        return (lax.shift_right_logical(kpos, CHUNK_SHIFT) <= qchunk) & (kpos < n_keys)

    kib = ki_ref[0].astype(BF16)
    qi = qi_ref[0]
    wi = wi_ref[0]
    score = jnp.zeros((tq, lk), F32)
    for h in range(IDX_HEADS):
        d = _dot_t(qi[:, h * IDX_DIM:(h + 1) * IDX_DIM], kib)
        score = score + wi[:, IDX_DIM + h:IDX_DIM + h + 1] * jnp.maximum(d, 0.0)
    score = jnp.where(score == 0.0, 0.0, score)
    score = jnp.where(allowed_at(0, lk), score, -jnp.inf)

    bits = pltpu.bitcast(score, jnp.int32)
    key_ref[...] = bits ^ ((bits >> 31) & jnp.int32(0x7FFFFFFF))
    int_min = jnp.int32(-2 ** 31)

    def bit_step(i, thr):
        cand = thr + lax.shift_left(jnp.int32(1), jnp.int32(31) - i)
        cnt = jnp.sum((key_ref[...] >= cand).astype(F32), axis=-1, keepdims=True)
        return jnp.where(cnt >= topk, cand, thr)

    thr = lax.fori_loop(0, 32, bit_step, jnp.full((tq, 1), int_min, jnp.int32))

    need = topk - jnp.sum((key_ref[...] > thr).astype(F32), axis=-1, keepdims=True)
    tri = (lax.broadcasted_iota(jnp.int32, (LANES, LANES), 0)
           <= lax.broadcasted_iota(jnp.int32, (LANES, LANES), 1)).astype(F32).astype(BF16)
    seen = jnp.zeros((tq, 1), F32)
    for c in range(lk // LANES):
        sl = slice(c * LANES, (c + 1) * LANES)
        key_c = key_ref[:, sl]
        eq_c = key_c == thr
        eq_f = eq_c.astype(F32)
        rank = _dot(eq_f.astype(BF16), tri) + seen
        chosen = ((key_c > thr) | (eq_c & (rank <= need))) & allowed_at(c * LANES, LANES)
        bias_ref[:, sl] = jnp.where(chosen, 0.0, -jnp.inf)
        seen = seen + jnp.sum(eq_f, axis=-1, keepdims=True)

    q = q_ref[0]
    kb = k_ref[0].astype(BF16)
    vb = v_ref[0].astype(BF16)
    bias = bias_ref[...]
    group = B_HEADS // B_KV_HEADS
    outs = []
    for n in range(B_KV_HEADS):
        kn = kb[:, n * HEAD_DIM:(n + 1) * HEAD_DIM]
        vn = vb[:, n * HEAD_DIM:(n + 1) * HEAD_DIM]
        for g in range(group):
            hq = n * group + g
            logits = _dot_t(q[:, hq * HEAD_DIM:(hq + 1) * HEAD_DIM], kn) + bias
            p = jnp.exp2(logits - jnp.max(logits, axis=-1, keepdims=True))
            den = jnp.sum(p, axis=-1, keepdims=True)
            outs.append(_dot(p.astype(BF16), vn) / den)
    o_ref[0] = jnp.concatenate(outs, axis=-1).astype(BF16)


def _attn_call(q, qi, wi, k_all, v_all, ki_all, n_keys, q_start, tq):
    b, t, _ = q.shape
    lk = k_all.shape[1]
    topk = min(TOPK_MAX, n_keys // 4)
    kern = functools.partial(_attn_kernel, tq=tq, n_keys=n_keys, q_start=q_start, topk=topk)
    row3 = lambda bi, j: (bi, j, 0)
    bat3 = lambda bi, j: (bi, 0, 0)
    return pl.pallas_call(
        kern,
        out_shape=jax.ShapeDtypeStruct((b, t, B_WIDTH), BF16),
        grid=(b, t // tq),
        in_specs=[
            pl.BlockSpec((1, tq, B_WIDTH), row3),
            pl.BlockSpec((1, tq, IDX_HEADS * IDX_DIM), row3),
            pl.BlockSpec((1, tq, LANES), row3),
            pl.BlockSpec((1, lk, B_KV_WIDTH), bat3),
            pl.BlockSpec((1, lk, B_KV_WIDTH), bat3),
            pl.BlockSpec((1, lk, IDX_DIM), bat3),
        ],
        out_specs=pl.BlockSpec((1, tq, B_WIDTH), row3),
        scratch_shapes=[pltpu.VMEM((tq, lk), jnp.int32), pltpu.VMEM((tq, lk), F32)],
        compiler_params=pltpu.CompilerParams(
            dimension_semantics=("arbitrary", "arbitrary"), vmem_limit_bytes=VMEM_LIMIT),
        name="dsa_attention",
    )(q, qi, wi, k_all, v_all, ki_all)


FOLD_ROWS = 64


def _col_reduce(x, op):
    r = x.shape[0]
    if r > FOLD_ROWS and r % FOLD_ROWS == 0:
        acc = x[:FOLD_ROWS]
        for c in range(1, r // FOLD_ROWS):
            acc = op(acc, x[c * FOLD_ROWS:(c + 1) * FOLD_ROWS])
        x, r = acc, FOLD_ROWS
    while r % 16 == 0:
        x = op(x[:r // 2], x[r // 2:])
        r //= 2
    return (jnp.sum if op is jnp.add else jnp.max)(x, axis=0, keepdims=True)


def _count16(mask):
    ones = jnp.where(mask, jnp.int16(1), jnp.int16(0))
    r = ones.shape[0]
    if r > FOLD_ROWS and r % FOLD_ROWS == 0:
        acc = ones[:FOLD_ROWS]
        for c in range(1, r // FOLD_ROWS):
            acc = acc + ones[c * FOLD_ROWS:(c + 1) * FOLD_ROWS]
        ones = acc
    return _col_reduce(ones.astype(jnp.int32), jnp.add)


def _radix_select16(ref, k, tq):
    def bit_step(i, thr):
        cand = thr + lax.shift_left(jnp.int32(1), jnp.int32(15) - i)
        cnt = _count16(ref[...] >= cand.astype(I16))
        return jnp.where(cnt >= k, cand, thr)

    return lax.fori_loop(0, 16, bit_step, jnp.full((1, tq), I16_MIN, jnp.int32))


def _attn_t_kernel(*refs, tq, n_keys, q_start, q_block0, topk, aliased):
    q_ref, qi_ref, wi_ref, k_ref, v_ref, ki_ref = refs[:6]
    o_ref, key_ref, bias_ref, hi_ref, lo_ref = refs[7:] if aliased else refs[6:]
    lk = k_ref.shape[1]
    j = pl.program_id(1)
    qchunk = lax.shift_right_logical(
        q_start + (q_block0 + j) * tq + lax.broadcasted_iota(jnp.int32, (1, tq), 1), CHUNK_SHIFT)

    def allowed_at(k0, rows):
        kpos = k0 + lax.broadcasted_iota(jnp.int32, (rows, 1), 0)
        return (lax.shift_right_logical(kpos, CHUNK_SHIFT) <= qchunk) & (kpos < n_keys)

    kib = ki_ref[0].astype(BF16)
    qi = qi_ref[0]
    wi_t = wi_ref[0].T
    score = jnp.zeros((lk, tq), F32)
    for h in range(IDX_HEADS):
        d = _dot_t(kib, qi[:, h * IDX_DIM:(h + 1) * IDX_DIM])
        score = score + wi_t[IDX_DIM + h:IDX_DIM + h + 1, :] * jnp.maximum(d, 0.0)
    score = jnp.where(score == 0.0, 0.0, score)
    score = jnp.where(allowed_at(0, lk), score, -jnp.inf)

    bits = pltpu.bitcast(score, jnp.int32)
    key = bits ^ ((bits >> 31) & jnp.int32(0x7FFFFFFF))
    key_ref[...] = key
    hi_ref[...] = (key >> 16).astype(I16)
    thr_hi = _radix_select16(hi_ref, topk, tq)
    thr_hi16 = thr_hi.astype(I16)
    hi = hi_ref[...]
    left = topk - _count16(hi > thr_hi16)
    low = ((key_ref[...] & 0xFFFF) + I16_MIN).astype(I16)
    lo_ref[...] = jnp.where(hi == thr_hi16, low, jnp.int16(I16_MIN))
    thr_lo = _radix_select16(lo_ref, left, tq)
    thr = lax.shift_left(thr_hi, 16) | (thr_lo - I16_MIN)
    thr = jnp.maximum(thr, KEY_MIN_FINITE)

    need = topk - _col_reduce((key_ref[...] > thr).astype(F32), jnp.add)
    tril = (lax.broadcasted_iota(jnp.int32, (LANES, LANES), 0)
            >= lax.broadcasted_iota(jnp.int32, (LANES, LANES), 1)).astype(F32).astype(BF16)
    seen = jnp.zeros((1, tq), F32)
    for c in range(lk // LANES):
        sl = slice(c * LANES, (c + 1) * LANES)
        key_c = key_ref[sl, :]
        eq_c = key_c == thr
        eq_f = eq_c.astype(F32)
        rank = _dot(tril, eq_f.astype(BF16)) + seen
        chosen = (key_c > thr) | (eq_c & (rank <= need))
        bias_ref[sl, :] = jnp.where(chosen, 0.0, -jnp.inf)
        seen = seen + _col_reduce(eq_f, jnp.add)

    q = q_ref[0]
    kb = k_ref[0].astype(BF16)
    vt = v_ref[0].T.astype(BF16)
    bias = bias_ref[...]
    group = B_HEADS // B_KV_HEADS
    outs = []
    for n in range(B_KV_HEADS):
        kn = kb[:, n * HEAD_DIM:(n + 1) * HEAD_DIM]
        vtn = vt[n * HEAD_DIM:(n + 1) * HEAD_DIM, :]
        qg = jnp.concatenate([q[:, (n * group + g) * HEAD_DIM:(n * group + g + 1) * HEAD_DIM]
                              for g in range(group)], axis=0)
        logits = _dot_t(kn, qg)
        ps, dens = [], []
        for g in range(group):
            lg = logits[:, g * tq:(g + 1) * tq] + bias
            p = jnp.exp2(lg - _col_reduce(lg, jnp.maximum))
            dens.append(_col_reduce(p, jnp.add))
            ps.append(p.astype(BF16))
        og = _dot(vtn, jnp.concatenate(ps, axis=1)) / jnp.concatenate(dens, axis=1)
        outs.extend(og[:, g * tq:(g + 1) * tq] for g in range(group))
    o_ref[0] = jnp.concatenate(outs, axis=0).T.astype(BF16)


def _attn_t_call(q, qi, wi, k, v, ki, prev, n_keys, q_start, q_block0, n_q_blocks, lk, tq):
    b, t, _ = q.shape
    topk = min(TOPK_MAX, n_keys // 4)
    aliased = prev is not None
    kern = functools.partial(_attn_t_kernel, tq=tq, n_keys=n_keys, q_start=q_start,
                             q_block0=q_block0, topk=topk, aliased=aliased)
    row3 = lambda bi, j: (bi, q_block0 + j, 0)
    bat3 = lambda bi, j: (bi, 0, 0)
    in_specs = [
        pl.BlockSpec((1, tq, B_WIDTH), row3),
        pl.BlockSpec((1, tq, IDX_HEADS * IDX_DIM), row3),
        pl.BlockSpec((1, tq, LANES), row3),
        pl.BlockSpec((1, lk, B_KV_WIDTH), bat3),
        pl.BlockSpec((1, lk, B_KV_WIDTH), bat3),
        pl.BlockSpec((1, lk, IDX_DIM), bat3),
    ]
    args = [q, qi, wi, k, v, ki]
    if aliased:
        in_specs.append(pl.BlockSpec(memory_space=pl.ANY))
        args.append(prev)
    return pl.pallas_call(
        kern,
        out_shape=jax.ShapeDtypeStruct((b, t, B_WIDTH), BF16),
        grid=(b, n_q_blocks),
        in_specs=in_specs,
        out_specs=pl.BlockSpec((1, tq, B_WIDTH), row3),
        scratch_shapes=[pltpu.VMEM((lk, tq), jnp.int32), pltpu.VMEM((lk, tq), F32),
                        pltpu.VMEM((lk, tq), I16), pltpu.VMEM((lk, tq), I16)],
        input_output_aliases={6: 0} if aliased else {},
        compiler_params=pltpu.CompilerParams(
            dimension_semantics=("arbitrary", "arbitrary"), vmem_limit_bytes=VMEM_LIMIT),
        name="dsa_attention_t",
    )(*args)


def _attn_prompt(q, qi, wi, k, v, ki, tq):
    t = q.shape[1]
    nqb = t // tq
    per_seg = 2 if nqb % 2 == 0 else 1
    bo = None
    for s in range(nqb // per_seg):
        bo = _attn_t_call(q, qi, wi, k, v, ki, bo, t, 0, s * per_seg, per_seg,
                          (s + 1) * per_seg * tq, tq)
    return bo


def _ffn_kernel(x_ref, ac_ref, bo_ref, mod_ref, g2_ref, woac_ref, wob_ref, wg_ref, wu_ref, wd_ref,
                gf_ref, o_ref, *, d_model, ff_chunk, final):
    x = x_ref[0]
    mod = mod_ref[0]
    ga1 =mod[:, 2 * d_model:3 * d_model]
    sh2 = mod[:, 3 * d_model:4 * d_model]
    sc2 = mod[:, 4 * d_model:5 * d_model]
    ga2 = mod[:, 5 * d_model:6 * d_model]
    mix = _dot(ac_ref[0], woac_ref[...]) + _dot(bo_ref[0], wob_ref[...])
    x1 = x + ga1 * mix
    h2 = ((_rms(x1) * g2_ref[...]) * (1.0 + sc2) + sh2).astype(BF16)
    d_ff = wg_ref.shape[1]
    acc = jnp.zeros_like(x1)
    for c0 in range(0, d_ff, ff_chunk):
        gate = _dot(h2, wg_ref[:, c0:c0 + ff_chunk])
        up = _dot(h2, wu_ref[:, c0:c0 + ff_chunk])
        act = (gate * jax.nn.sigmoid(gate)) * up
        acc = acc + _dot(act.astype(BF16), wd_ref[c0:c0 + ff_chunk, :])
    x2 = x1 + ga2 * acc
    if final:
        o_ref[0] = _rms(x2) * gf_ref[...]
    else:
        o_ref[0] = x2


def _ffn_call(x, ac, bo, mod, layer, mod_row0, g2, woac, wob, wg, wu, wd, gf, tm, final):
    b, t, d = x.shape
    d_ff = wg.shape[2]
    ff_chunk = d_ff // 2 if (d_ff // 2) % LANES == 0 else d_ff
    kern = functools.partial(_ffn_kernel, d_model=d, ff_chunk=ff_chunk, final=final)
    row3 = lambda bi, j: (bi, j, 0)
    lay3 = lambda bi, j: (layer, 0, 0)
    once = pl.Buffered(1)
    return pl.pallas_call(
        kern,
        out_shape=jax.ShapeDtypeStruct((b, t, d), F32),
        grid=(b, t // tm),
        in_specs=[
            pl.BlockSpec((1, tm, d), row3),
            pl.BlockSpec((1, tm, A_WIDTH + C_WIDTH), row3),
            pl.BlockSpec((1, tm, B_WIDTH), row3),
            pl.BlockSpec((None, 1, 1, N_MOD * d), lambda bi, j: (layer, mod_row0 + bi, 0, 0)),
            pl.BlockSpec((None, 1, d), lay3),
            pl.BlockSpec((None, A_WIDTH + C_WIDTH, d), lay3, pipeline_mode=once),
            pl.BlockSpec((None, B_WIDTH, d), lay3, pipeline_mode=once),
            pl.BlockSpec((None, d, d_ff), lay3, pipeline_mode=once),
            pl.BlockSpec((None, d, d_ff), lay3, pipeline_mode=once),
            pl.BlockSpec((None, d_ff, d), lay3, pipeline_mode=once),
            pl.BlockSpec((1, d), lambda bi, j: (0, 0)),
        ],
        out_specs=pl.BlockSpec((1, tm, d), row3),
        compiler_params=pltpu.CompilerParams(
            dimension_semantics=("arbitrary", "arbitrary"), vmem_limit_bytes=VMEM_LIMIT),
        name="out_proj_ffn",
    )(x, ac, bo, mod, g2, woac, wob, wg, wu, wd, gf)


def _rope_tables(start, t):
    half = HEAD_DIM // 2
    inv = jnp.power(ROPE_THETA, -2.0 * jnp.arange(half, dtype=F32) / HEAD_DIM)
    ang = (start + jnp.arange(t)).astype(F32)[:, None] * inv[None, :]
    cos = jnp.cos(ang)
    sin = jnp.sin(ang)
    zero = jnp.zeros_like(sin)
    reps = LANES // HEAD_DIM
    return (jnp.tile(jnp.concatenate([cos, cos], axis=-1), (1, reps)),
            jnp.tile(jnp.concatenate([-sin, zero], axis=-1), (1, reps)),
            jnp.tile(jnp.concatenate([zero, sin], axis=-1), (1, reps)))


def _pad_w_in(w_in):
    n_head = OFF_KI + IDX_DIM + IDX_HEADS
    depth, d, _ = w_in.shape
    pad = jnp.zeros((depth, d, OFF_C - n_head), w_in.dtype)
    return jnp.concatenate([w_in[:, :, :n_head], pad, w_in[:, :, n_head:]], axis=-1).astype(BF16)


def _trunk(x, mod, mod_row0, cache, params, q_start, tm, tq):
    (g_norm1, g_norm2, w_in_p, w_spatial, b_spatial, w_dw, b_dw, g_cnorm,
     woac, wob, wg, wu, wd, g_final) = params
    b, t, d = x.shape
    depth = w_in_p.shape[0]
    ln = min(t, A_CHUNK)
    tabs = _rope_tables(q_start, t)
    wsp = w_spatial[:, :, :ln, :ln]
    bsp_tab = jnp.repeat(jnp.swapaxes(b_spatial[:, :, :ln], 1, 2), A_HEAD_DIM, axis=-1)
    ks, vs, kis, convs, avs = [], [], [], [], []
    for l in range(depth):
        if cache is None:
            cst = jnp.zeros((b, CTX_ROWS, C_WIDTH), F32)
        else:
            cst = jnp.pad(cache[3][l], ((0, 0), (CTX_PAD, 0), (0, 0)))
        q, k, v, qi, ki, wi, ac, av, cnew = _in_call(
            x, mod, l, mod_row0, g_norm1, w_in_p, tabs, wsp, bsp_tab, w_dw, b_dw, g_cnorm, cst, tm)
        if cache is None:
            bo = _attn_prompt(q, qi, wi, k, v, ki, tq)
        else:
            past = cache[0].shape[2]
            n_keys = past + t
            lk = -(-n_keys // LANES) * LANES
            grow = lambda old, new: jnp.pad(jnp.concatenate([old, new], axis=1),
                                            ((0, 0), (0, lk - n_keys), (0, 0)))
            k_all = grow(cache[0][l].reshape(b, past, B_KV_WIDTH), k)
            v_all = grow(cache[1][l].reshape(b, past, B_KV_WIDTH), v)
            ki_all = grow(cache[2][l], ki)
            bo = _attn_call(q, qi, wi, k_all, v_all, ki_all, n_keys, q_start, tq)
        x = _ffn_call(x, ac, bo, mod, l, mod_row0, g_norm2, woac, wob, wg, wu, wd, g_final,
                      tm, final=(l == depth - 1))
        ks.append(k.reshape(b, t, B_KV_HEADS, HEAD_DIM))
        vs.append(v.reshape(b, t, B_KV_HEADS, HEAD_DIM))
        kis.append(ki)
        convs.append(cnew[:, CTX_PAD:, :])
        avs.append(av)
    return x, jnp.stack(ks), jnp.stack(vs), jnp.stack(kis), jnp.stack(convs), jnp.stack(avs)


def kernel(x_prompt, x_sample, cache_k, cache_v, cache_kidx, state_conv, c_prompt, c_sample,
           w_ada, b_ada, g_norm1, g_norm2, w_in, w_spatial, b_spatial, w_dw, b_dw, g_cnorm,
           w_out, w_gate, w_up, w_down, g_final):
    depth, d = g_norm1.shape
    nb_p, t_p, _ = x_prompt.shape
    nb_s, t_s, _ = x_sample.shape
    past = cache_k.shape[2]

    rows = nb_p + nb_s
    rows_pad = -(-rows // 8) * 8
    c_all = jnp.pad(jnp.concatenate([c_prompt, c_sample], axis=0), ((0, rows_pad - rows), (0, 0)))
    mod = _mod_call(c_all, w_ada, b_ada).reshape(depth, rows_pad, 1, N_MOD * d)

    w_out_b = w_out.astype(BF16)
    params = (
        g_norm1.reshape(depth, 1, d), g_norm2.reshape(depth, 1, d), _pad_w_in(w_in),
        w_spatial, b_spatial, w_dw, b_dw.reshape(depth, 1, C_WIDTH), g_cnorm.reshape(depth, 1, C_WIDTH),
        jnp.concatenate([w_out_b[:, :A_WIDTH], w_out_b[:, A_WIDTH + B_WIDTH:]], axis=1),
        w_out_b[:, A_WIDTH:A_WIDTH + B_WIDTH],
        w_gate.astype(BF16), w_up.astype(BF16), w_down.astype(BF16), g_final.reshape(1, d),
    )

    y_p, p_k, p_v, p_ki, p_conv, _ = _trunk(
        x_prompt, mod, 0, None, params, 0, tm=min(t_p, 512), tq=min(t_p, 128))
    y_s, s_k, s_v, s_ki, s_conv, s_av = _trunk(
        x_sample, mod, nb_p, (cache_k, cache_v, cache_kidx, state_conv), params, past,
        tm=t_s, tq=t_s)
    return (y_p, y_s, p_k, p_v, p_ki, p_conv, s_k, s_v, s_ki, s_conv, s_av)
```

```python
import functools

import jax
import jax.numpy as jnp
from jax import lax
from jax.experimental import pallas as pl
from jax.experimental.pallas import tpu as pltpu

CHUNK = 64
CHUNK_SHIFT = 6
assert 1 << CHUNK_SHIFT == CHUNK
A_HEADS = 4
A_HEAD_DIM = 64
A_WIDTH = A_HEADS * A_HEAD_DIM
A_CHUNK = 128
B_HEADS = 8
B_KV_HEADS = 2
HEAD_DIM = 64
B_WIDTH = B_HEADS * HEAD_DIM
B_KV_WIDTH = B_KV_HEADS * HEAD_DIM
IDX_HEADS = 4
IDX_DIM = 64
TOPK_MAX = 256
ROPE_THETA = 10000.0
C_GROUPS = 4
C_WIDTH = 256
CONV_WIDTH = 31
N_MOD = 6
EPS = 1e-6

LANES = 128
SUBLANES = 8
CTX_ROWS = 32
CTX_PAD = CTX_ROWS - (CONV_WIDTH - 1)
VMEM_LIMIT = 56 * 1024 * 1024

OFF_A = 0
OFF_Q = OFF_A + 2 * A_WIDTH
OFF_K = OFF_Q + B_WIDTH
OFF_V = OFF_K + B_KV_WIDTH
OFF_QI = OFF_V + B_KV_WIDTH
OFF_KI = OFF_QI + IDX_HEADS * IDX_DIM
OFF_C = OFF_KI + LANES
N_IN_PAD = OFF_C + 2 * C_WIDTH

LOG2E = 1.4426950408889634
KEY_MIN_FINITE = -2139095040

F32 = jnp.float32
BF16 = jnp.bfloat16


def _dot(a, b):
    return jnp.dot(a, b, preferred_element_type=F32)


def _dot_t(a, b):
    return lax.dot_general(a, b, (((1,), (1,)), ((), ())), preferred_element_type=F32)


def _rms(x):
    return x * lax.rsqrt(jnp.mean(x * x, axis=-1, keepdims=True) + EPS)


def _lane_group(n, group):
    shift = group.bit_length() - 1
    assert 1 << shift == group
    return lax.shift_right_logical(lax.broadcasted_iota(jnp.int32, (1, n), 1), shift)


def _group_standardize(y, group):
    n = y.shape[-1]
    gid = _lane_group(n, group)
    inv = 1.0 / group
    mean = jnp.zeros_like(y)
    for g in range(n // group):
        m = gid == g
        s = jnp.sum(jnp.where(m, y, 0.0), axis=-1, keepdims=True) * inv
        mean = jnp.where(m, s, mean)
    yc = y - mean
    sq = yc * yc
    var = jnp.zeros_like(y)
    for g in range(n // group):
        m = gid == g
        s = jnp.sum(jnp.where(m, sq, 0.0), axis=-1, keepdims=True) * inv
        var = jnp.where(m, s, var)
    return yc * lax.rsqrt(var + EPS)


def _rope(x, cos, sin_lo, sin_hi):
    parts = []
    for c in range(x.shape[-1] // LANES):
        xc = x[:, c * LANES:(c + 1) * LANES]
        up = pltpu.roll(xc, LANES - HEAD_DIM // 2, 1)
        down = pltpu.roll(xc, HEAD_DIM // 2, 1)
        parts.append(xc * cos + up * sin_lo + down * sin_hi)
    return parts[0] if len(parts) == 1 else jnp.concatenate(parts, axis=-1)


def _mod_kernel(c_ref, w_ref, b_ref, o_ref):
    c = c_ref[...]
    cond = c * jax.nn.sigmoid(c)
    o_ref[0] = _dot(cond.astype(BF16), w_ref[0].astype(BF16)) + b_ref[0]


def _mod_call(c_all, w_ada, b_ada):
    depth, d, n = w_ada.shape
    rows = c_all.shape[0]
    tn = d
    return pl.pallas_call(
        _mod_kernel,
        out_shape=jax.ShapeDtypeStruct((depth, rows, n), F32),
        grid=(depth, n // tn),
        in_specs=[
            pl.BlockSpec((rows, d), lambda l, j: (0, 0)),
            pl.BlockSpec((1, d, tn), lambda l, j: (l, 0, j)),
            pl.BlockSpec((1, 1, tn), lambda l, j: (l, 0, j)),
        ],
        out_specs=pl.BlockSpec((1, rows, tn), lambda l, j: (l, 0, j)),
        compiler_params=pltpu.CompilerParams(
            dimension_semantics=("arbitrary", "arbitrary"), vmem_limit_bytes=VMEM_LIMIT),
        name="adaln_mod",
    )(c_all, w_ada, b_ada.reshape(depth, 1, n))


def _in_kernel(x_ref, mod_ref, g1_ref, w_ref, cos_ref, slo_ref, shi_ref, wsp_ref, bsp_ref,
               wdw_ref, bdw_ref, gcn_ref, cst_ref,
               q_ref, k_ref, v_ref, qi_ref, ki_ref, wi_ref, ac_ref, av_ref, cnew_ref,
               hp_ref, sh_ref, *, tm, ln, d_model):
    j = pl.program_id(1)
    x = x_ref[0]
    mod = mod_ref[0]
    sh1 = mod[:, 0:d_model]
    sc1 = mod[:, d_model:2 * d_model]
    h = (_rms(x) * g1_ref[...]) * (1.0 + sc1) + sh1
    z = _dot(h.astype(BF16), w_ref[...])

    cos = cos_ref[...]
    slo = slo_ref[...]
    shi = shi_ref[...]

    q = _rope(z[:, OFF_Q:OFF_Q + B_WIDTH], cos, slo, shi)
    q_ref[0] = (q * (LOG2E * HEAD_DIM ** -0.5)).astype(BF16)
    k_ref[0] = _rope(z[:, OFF_K:OFF_K + B_KV_WIDTH], cos, slo, shi)
    v_ref[0] = z[:, OFF_V:OFF_V + B_KV_WIDTH]
    qi_ref[0] = _rope(z[:, OFF_QI:OFF_QI + IDX_HEADS * IDX_DIM], cos, slo, shi).astype(BF16)
    kiwi = z[:, OFF_KI:OFF_KI + LANES]
    ki_ref[0] = _rope(kiwi, cos, slo, shi)[:, 0:IDX_DIM]
    wi_ref[0] = kiwi

    za = z[:, OFF_A:OFF_A + 2 * A_WIDTH]
    za = 0.5 * za * (1.0 + lax.erf(za * (2.0 ** -0.5)))
    u = za[:, 0:A_WIDTH]
    vn = _group_standardize(za[:, A_WIDTH:2 * A_WIDTH], A_HEAD_DIM)
    av_ref[0] = vn
    vb = vn.astype(BF16)
    row = lax.broadcasted_iota(jnp.int32, (ln, ln), 0)
    col = lax.broadcasted_iota(jnp.int32, (ln, ln), 1)
    head_of_lane = _lane_group(A_WIDTH, A_HEAD_DIM)
    wsp = [jnp.where(col <= row, wsp_ref[g], 0.0).astype(BF16) for g in range(A_HEADS)]
    gated = []
    for c in range(tm // ln):
        vc = vb[c * ln:(c + 1) * ln, :]
        s = jnp.zeros((ln, A_WIDTH), F32)
        for g in range(A_HEADS):
            s = jnp.where(head_of_lane == g, _dot(wsp[g], vc), s)
        gated.append(u[c * ln:(c + 1) * ln, :] * (s + bsp_ref[...]))
    a_out = gated[0] if len(gated) == 1 else jnp.concatenate(gated, axis=0)
    ac_ref[0, :, 0:A_WIDTH] = a_out.astype(BF16)

    zc = z[:, OFF_C:OFF_C + 2 * C_WIDTH]
    hc = zc[:, 0:C_WIDTH] * jax.nn.sigmoid(zc[:, C_WIDTH:2 * C_WIDTH])

    @pl.when(j == 0)
    def _():
        hp_ref[0:CTX_ROWS, :] = cst_ref[0]

    hp_ref[CTX_ROWS:CTX_ROWS + tm, :] = hc
    span = tm + CTX_ROWS - SUBLANES
    for s in range(1, SUBLANES):
        sh_ref[s - 1, 0:span, :] = hp_ref[s:s + span, :]
    rb = min(tm, 64)
    wdw = wdw_ref[...]
    conv = []
    for r0 in range(0, tm, rb):
        acc = jnp.zeros((rb, C_WIDTH), F32)
        for t in range(CONV_WIDTH):
            phase = (CTX_PAD + t) % SUBLANES
            base = CTX_PAD + t - phase + r0
            rows = hp_ref[base:base + rb, :] if phase == 0 else sh_ref[phase - 1, base:base + rb, :]
            acc = acc + rows * wdw[t:t + 1, :]
        conv.append(acc)
    y = (conv[0] if len(conv) == 1 else jnp.concatenate(conv, axis=0)) + bdw_ref[...]
    y = _group_standardize(y, C_WIDTH // C_GROUPS) * gcn_ref[...]
    ac_ref[0, :, A_WIDTH:A_WIDTH + C_WIDTH] = (y * jax.nn.sigmoid(y)).astype(BF16)

    tail = hp_ref[tm:tm + CTX_ROWS, :]
    cnew_ref[0] = tail
    hp_ref[0:CTX_ROWS, :] = tail


def _in_call(x, mod, layer, mod_row0, g1, w_in_p, tabs, wsp, bsp_tab, wdw, bdw, gcn, cst, tm):
    b, t, d = x.shape
    ln = min(t, A_CHUNK)
    cos, slo, shi = tabs
    kern = functools.partial(_in_kernel, tm=tm, ln=ln, d_model=d)
    row3 = lambda bi, j: (bi, j, 0)
    full2 = lambda bi, j: (0, 0)
    lay3 = lambda bi, j: (layer, 0, 0)
    out_shape = (
        jax.ShapeDtypeStruct((b, t, B_WIDTH), BF16),
        jax.ShapeDtypeStruct((b, t, B_KV_WIDTH), F32),
        jax.ShapeDtypeStruct((b, t, B_KV_WIDTH), F32),
        jax.ShapeDtypeStruct((b, t, IDX_HEADS * IDX_DIM), BF16),
        jax.ShapeDtypeStruct((b, t, IDX_DIM), F32),
        jax.ShapeDtypeStruct((b, t, LANES), F32),
        jax.ShapeDtypeStruct((b, t, A_WIDTH + C_WIDTH), BF16),
        jax.ShapeDtypeStruct((b, t, A_WIDTH), F32),
        jax.ShapeDtypeStruct((b, CTX_ROWS, C_WIDTH), F32),
    )
    out_specs = (
        pl.BlockSpec((1, tm, B_WIDTH), row3),
        pl.BlockSpec((1, tm, B_KV_WIDTH), row3),
        pl.BlockSpec((1, tm, B_KV_WIDTH), row3),
        pl.BlockSpec((1, tm, IDX_HEADS * IDX_DIM), row3),
        pl.BlockSpec((1, tm, IDX_DIM), row3),
        pl.BlockSpec((1, tm, LANES), row3),
        pl.BlockSpec((1, tm, A_WIDTH + C_WIDTH), row3),
        pl.BlockSpec((1, tm, A_WIDTH), row3),
        pl.BlockSpec((1, CTX_ROWS, C_WIDTH), lambda bi, j: (bi, 0, 0)),
    )
    in_specs = [
        pl.BlockSpec((1, tm, d), row3),
        pl.BlockSpec((None, 1, 1, N_MOD * d), lambda bi, j: (layer, mod_row0 + bi, 0, 0)),
        pl.BlockSpec((None, 1, d), lay3),
        pl.BlockSpec((None, d, N_IN_PAD), lay3),
        pl.BlockSpec((tm, LANES), lambda bi, j: (j, 0)),
        pl.BlockSpec((tm, LANES), lambda bi, j: (j, 0)),
        pl.BlockSpec((tm, LANES), lambda bi, j: (j, 0)),
        pl.BlockSpec((None, A_HEADS, ln, ln), lambda bi, j: (layer, 0, 0, 0)),
        pl.BlockSpec((None, ln, A_WIDTH), lay3),
        pl.BlockSpec((None, CONV_WIDTH, C_WIDTH), lay3),
        pl.BlockSpec((None, 1, C_WIDTH), lay3),
        pl.BlockSpec((None, 1, C_WIDTH), lay3),
        pl.BlockSpec((1, CTX_ROWS, C_WIDTH), lambda bi, j: (bi, 0, 0)),
    ]
    del full2
    return pl.pallas_call(
        kern,
        out_shape=out_shape,
        grid=(b, t // tm),
        in_specs=in_specs,
        out_specs=out_specs,
        scratch_shapes=[pltpu.VMEM((tm + CTX_ROWS, C_WIDTH), F32),
                        pltpu.VMEM((SUBLANES - 1, tm + CTX_ROWS, C_WIDTH), F32)],
        compiler_params=pltpu.CompilerParams(
            dimension_semantics=("arbitrary", "arbitrary"), vmem_limit_bytes=VMEM_LIMIT),
        name="in_proj_mixers_ac",
    )(x, mod, g1, w_in_p, cos, slo, shi, wsp, bsp_tab, wdw, bdw, gcn, cst)


def _attn_kernel(q_ref, qi_ref, wi_ref, k_ref, v_ref, ki_ref, o_ref, key_ref, bias_ref,
                 *, tq, n_keys, q_start, topk):
    lk = k_ref.shape[1]
    j = pl.program_id(1)
    qchunk = lax.shift_right_logical(
        q_start + j * tq + lax.broadcasted_iota(jnp.int32, (tq, 1), 0), CHUNK_SHIFT)

    def allowed_at(k0, width):
        kpos = k0 + lax.broadcasted_iota(jnp.int32, (1, width), 1)
        return (lax.shift_right_logical(kpos, CHUNK_SHIFT) <= qchunk) & (kpos < n_keys)

    kib = ki_ref[0].astype(BF16)
    qi = qi_ref[0]
    wi = wi_ref[0]
    score = jnp.zeros((tq, lk), F32)
    for h in range(IDX_HEADS):
        d = _dot_t(qi[:, h * IDX_DIM:(h + 1) * IDX_DIM], kib)
        score = score + wi[:, IDX_DIM + h:IDX_DIM + h + 1] * jnp.maximum(d, 0.0)
    score = jnp.where(score == 0.0, 0.0, score)
    score = jnp.where(allowed_at(0, lk), score, -jnp.inf)

    bits = pltpu.bitcast(score, jnp.int32)
    key_ref[...] = bits ^ ((bits >> 31) & jnp.int32(0x7FFFFFFF))
    int_min = jnp.int32(-2 ** 31)

    def bit_step(i, thr):
        cand = thr + lax.shift_left(jnp.int32(1), jnp.int32(31) - i)
        cnt = jnp.sum((key_ref[...] >= cand).astype(F32), axis=-1, keepdims=True)
        return jnp.where(cnt >= topk, cand, thr)

    thr = lax.fori_loop(0, 32, bit_step, jnp.full((tq, 1), int_min, jnp.int32))

    need = topk - jnp.sum((key_ref[...] > thr).astype(F32), axis=-1, keepdims=True)
    tri = (lax.broadcasted_iota(jnp.int32, (LANES, LANES), 0)
           <= lax.broadcasted_iota(jnp.int32, (LANES, LANES), 1)).astype(F32).astype(BF16)
    seen = jnp.zeros((tq, 1), F32)
    for c in range(lk // LANES):
        sl = slice(c * LANES, (c + 1) * LANES)
        key_c = key_ref[:, sl]
        eq_c = key_c == thr
        eq_f = eq_c.astype(F32)
        rank = _dot(eq_f.astype(BF16), tri) + seen
        chosen = ((key_c > thr) | (eq_c & (rank <= need))) & allowed_at(c * LANES, LANES)
        bias_ref[:, sl] = jnp.where(chosen, 0.0, -jnp.inf)
        seen = seen + jnp.sum(eq_f, axis=-1, keepdims=True)

    q = q_ref[0]
    kb = k_ref[0].astype(BF16)
    vb = v_ref[0].astype(BF16)
    bias = bias_ref[...]
    group = B_HEADS // B_KV_HEADS
    outs = []
    for n in range(B_KV_HEADS):
        kn = kb[:, n * HEAD_DIM:(n + 1) * HEAD_DIM]
        vn = vb[:, n * HEAD_DIM:(n + 1) * HEAD_DIM]
        for g in range(group):
            hq = n * group + g
            logits = _dot_t(q[:, hq * HEAD_DIM:(hq + 1) * HEAD_DIM], kn) + bias
            p = jnp.exp2(logits - jnp.max(logits, axis=-1, keepdims=True))
            den = jnp.sum(p, axis=-1, keepdims=True)
            outs.append(_dot(p.astype(BF16), vn) / den)
    o_ref[0] = jnp.concatenate(outs, axis=-1).astype(BF16)


def _attn_call(q, qi, wi, k_all, v_all, ki_all, n_keys, q_start, tq):
    b, t, _ = q.shape
    lk = k_all.shape[1]
    topk = min(TOPK_MAX, n_keys // 4)
    kern = functools.partial(_attn_kernel, tq=tq, n_keys=n_keys, q_start=q_start, topk=topk)
    row3 = lambda bi, j: (bi, j, 0)
    bat3 = lambda bi, j: (bi, 0, 0)
    return pl.pallas_call(
        kern,
        out_shape=jax.ShapeDtypeStruct((b, t, B_WIDTH), BF16),
        grid=(b, t // tq),
        in_specs=[
            pl.BlockSpec((1, tq, B_WIDTH), row3),
            pl.BlockSpec((1, tq, IDX_HEADS * IDX_DIM), row3),
            pl.BlockSpec((1, tq, LANES), row3),
            pl.BlockSpec((1, lk, B_KV_WIDTH), bat3),
            pl.BlockSpec((1, lk, B_KV_WIDTH), bat3),
            pl.BlockSpec((1, lk, IDX_DIM), bat3),
        ],
        out_specs=pl.BlockSpec((1, tq, B_WIDTH), row3),
        scratch_shapes=[pltpu.VMEM((tq, lk), jnp.int32), pltpu.VMEM((tq, lk), F32)],
        compiler_params=pltpu.CompilerParams(
            dimension_semantics=("arbitrary", "arbitrary"), vmem_limit_bytes=VMEM_LIMIT),
        name="dsa_attention",
    )(q, qi, wi, k_all, v_all, ki_all)


FOLD_ROWS = 64


def _col_reduce(x, op):
    r = x.shape[0]
    if r > FOLD_ROWS and r % FOLD_ROWS == 0:
        acc = x[:FOLD_ROWS]
        for c in range(1, r // FOLD_ROWS):
            acc = op(acc, x[c * FOLD_ROWS:(c + 1) * FOLD_ROWS])
        x, r = acc, FOLD_ROWS
    while r % 16 == 0:
        x = op(x[:r // 2], x[r // 2:])
        r //= 2
    return (jnp.sum if op is jnp.add else jnp.max)(x, axis=0, keepdims=True)


WORD_BITS = 32
INT_MIN = -2 ** 31


def _bit_transpose(words):
    a = list(words)
    j, m = WORD_BITS // 2, 0x0000FFFF
    while j:
        k = 0
        while k < WORD_BITS:
            t = (a[k] ^ lax.shift_right_logical(a[k + j], j)) & m
            a[k] = a[k] ^ t
            a[k + j] = a[k + j] ^ lax.shift_left(t, j)
            k = (k + j + 1) & ~j
        j >>= 1
        m = (m ^ (m << j)) & 0x7FFFFFFF if j else m
    return a


def _select_threshold(key_ref, planes_ref, topk, tq):
    lk = key_ref.shape[0]
    slab = lk // WORD_BITS
    for r0 in range(0, slab, SUBLANES):
        words = [key_ref[i * slab + r0:i * slab + r0 + SUBLANES, :] for i in range(WORD_BITS)]
        for b, plane in enumerate(_bit_transpose(words)):
            planes_ref[b, r0:r0 + SUBLANES, :] = plane
    alive = jnp.full((slab, tq), -1, jnp.int32)
    left = jnp.full((1, tq), topk, jnp.int32)
    thr = jnp.zeros((1, tq), jnp.int32)
    for bit in range(WORD_BITS - 1, -1, -1):
        plane = planes_ref[WORD_BITS - 1 - bit]
        if bit == WORD_BITS - 1:
            plane = ~plane
        upper = alive & plane
        cnt = _col_reduce(lax.population_count(upper), jnp.add)
        take = cnt >= left
        alive = jnp.where(take, upper, alive ^ upper)
        left = jnp.where(take, left, left - cnt)
        thr = thr | jnp.where(take, jnp.int32(1 << bit if bit < WORD_BITS - 1 else INT_MIN), 0)
    return thr ^ INT_MIN, left


def _attn_t_kernel(*refs, tq, n_keys, q_start, q_block0, topk, aliased):
    q_ref, qi_ref, wi_ref, k_ref, v_ref, ki_ref = refs[:6]
    o_ref, key_ref, bias_ref, planes_ref = refs[7:] if aliased else refs[6:]
    lk = k_ref.shape[1]
    j = pl.program_id(1)
    qchunk = lax.shift_right_logical(
        q_start + (q_block0 + j) * tq + lax.broadcasted_iota(jnp.int32, (1, tq), 1), CHUNK_SHIFT)

    def allowed_at(k0, rows):
        kpos = k0 + lax.broadcasted_iota(jnp.int32, (rows, 1), 0)
        return (lax.shift_right_logical(kpos, CHUNK_SHIFT) <= qchunk) & (kpos < n_keys)

    kib = ki_ref[0].astype(BF16)
    qi = qi_ref[0]
    wi_t = wi_ref[0].T
    score = jnp.zeros((lk, tq), F32)
    for h in range(IDX_HEADS):
        d = _dot_t(kib, qi[:, h * IDX_DIM:(h + 1) * IDX_DIM])
        score = score + wi_t[IDX_DIM + h:IDX_DIM + h + 1, :] * jnp.maximum(d, 0.0)
    score = jnp.where(score == 0.0, 0.0, score)
    score = jnp.where(allowed_at(0, lk), score, -jnp.inf)

    bits = pltpu.bitcast(score, jnp.int32)
    key_ref[...] = bits ^ ((bits >> 31) & jnp.int32(0x7FFFFFFF))
    thr, left = _select_threshold(key_ref, planes_ref, topk, tq)

    clamped = thr < KEY_MIN_FINITE
    thr = jnp.where(clamped, KEY_MIN_FINITE, thr)
    need = jnp.where(clamped, topk, left).astype(F32)
    tril = (lax.broadcasted_iota(jnp.int32, (LANES, LANES), 0)
            >= lax.broadcasted_iota(jnp.int32, (LANES, LANES), 1)).astype(F32).astype(BF16)
    seen = jnp.zeros((1, tq), F32)
    for c in range(lk // LANES):
        sl = slice(c * LANES, (c + 1) * LANES)
        key_c = key_ref[sl, :]
        eq_c = key_c == thr
        eq_f = eq_c.astype(F32)
        rank = _dot(tril, eq_f.astype(BF16)) + seen
        chosen = (key_c > thr) | (eq_c & (rank <= need))
        bias_ref[sl, :] = jnp.where(chosen, 0.0, -jnp.inf)
        seen = seen + _col_reduce(eq_f, jnp.add)

    q = q_ref[0]
    kb = k_ref[0].astype(BF16)
    vt = v_ref[0].T.astype(BF16)
    bias = bias_ref[...]
    group = B_HEADS // B_KV_HEADS
    outs = []
    for n in range(B_KV_HEADS):
        kn = kb[:, n * HEAD_DIM:(n + 1) * HEAD_DIM]
        vtn = vt[n * HEAD_DIM:(n + 1) * HEAD_DIM, :]
        qg = jnp.concatenate([q[:, (n * group + g) * HEAD_DIM:(n * group + g + 1) * HEAD_DIM]
                              for g in range(group)], axis=0)
        logits = _dot_t(kn, qg)
        ps, dens = [], []
        for g in range(group):
            lg = logits[:, g * tq:(g + 1) * tq] + bias
            p = jnp.exp2(lg - _col_reduce(lg, jnp.maximum))
            dens.append(_col_reduce(p, jnp.add))
            ps.append(p.astype(BF16))
        og = _dot(vtn, jnp.concatenate(ps, axis=1)) / jnp.concatenate(dens, axis=1)
        outs.extend(og[:, g * tq:(g + 1) * tq] for g in range(group))
    o_ref[0] = jnp.concatenate(outs, axis=0).T.astype(BF16)


def _attn_t_call(q, qi, wi, k, v, ki, prev, n_keys, q_start, q_block0, n_q_blocks, lk, tq):
    b, t, _ = q.shape
    topk = min(TOPK_MAX, n_keys // 4)
    aliased = prev is not None
    kern = functools.partial(_attn_t_kernel, tq=tq, n_keys=n_keys, q_start=q_start,
                             q_block0=q_block0, topk=topk, aliased=aliased)
    row3 = lambda bi, j: (bi, q_block0 + j, 0)
    bat3 = lambda bi, j: (bi, 0, 0)
    in_specs = [
        pl.BlockSpec((1, tq, B_WIDTH), row3),
        pl.BlockSpec((1, tq, IDX_HEADS * IDX_DIM), row3),
        pl.BlockSpec((1, tq, LANES), row3),
        pl.BlockSpec((1, lk, B_KV_WIDTH), bat3),
        pl.BlockSpec((1, lk, B_KV_WIDTH), bat3),
        pl.BlockSpec((1, lk, IDX_DIM), bat3),
    ]
    args = [q, qi, wi, k, v, ki]
    if aliased:
        in_specs.append(pl.BlockSpec(memory_space=pl.ANY))
        args.append(prev)
    return pl.pallas_call(
        kern,
        out_shape=jax.ShapeDtypeStruct((b, t, B_WIDTH), BF16),
        grid=(b, n_q_blocks),
        in_specs=in_specs,
        out_specs=pl.BlockSpec((1, tq, B_WIDTH), row3),
        scratch_shapes=[pltpu.VMEM((lk, tq), jnp.int32), pltpu.VMEM((lk, tq), F32),
                        pltpu.VMEM((WORD_BITS, lk // WORD_BITS, tq), jnp.int32)],
        input_output_aliases={6: 0} if aliased else {},
        compiler_params=pltpu.CompilerParams(
            dimension_semantics=("arbitrary", "arbitrary"), vmem_limit_bytes=VMEM_LIMIT),
        name="dsa_attention_t",
    )(*args)


def _attn_prompt(q, qi, wi, k, v, ki, tq):
    t = q.shape[1]
    nqb = t // tq
    per_seg = WORD_BITS * SUBLANES // tq
    assert per_seg >= 1 and nqb % per_seg == 0
    bo = None
    for s in range(nqb // per_seg):
        bo = _attn_t_call(q, qi, wi, k, v, ki, bo, t, 0, s * per_seg, per_seg,
                          (s + 1) * per_seg * tq, tq)
    return bo


def _ffn_kernel(x_ref, ac_ref, bo_ref, mod_ref, g2_ref, woac_ref, wob_ref, wg_ref, wu_ref, wd_ref,
                gf_ref, o_ref, *, d_model, ff_chunk, final):
    x = x_ref[0]
    mod = mod_ref[0]
    ga1 =mod[:, 2 * d_model:3 * d_model]
    sh2 = mod[:, 3 * d_model:4 * d_model]
    sc2 = mod[:, 4 * d_model:5 * d_model]
    ga2 = mod[:, 5 * d_model:6 * d_model]
    mix = _dot(ac_ref[0], woac_ref[...]) + _dot(bo_ref[0], wob_ref[...])
    x1 = x + ga1 * mix
    h2 = ((_rms(x1) * g2_ref[...]) * (1.0 + sc2) + sh2).astype(BF16)
    d_ff = wg_ref.shape[1]
    acc = jnp.zeros_like(x1)
    for c0 in range(0, d_ff, ff_chunk):
        gate = _dot(h2, wg_ref[:, c0:c0 + ff_chunk])
        up = _dot(h2, wu_ref[:, c0:c0 + ff_chunk])
        act = (gate * jax.nn.sigmoid(gate)) * up
        acc = acc + _dot(act.astype(BF16), wd_ref[c0:c0 + ff_chunk, :])
    x2 = x1 + ga2 * acc
    if final:
        o_ref[0] = _rms(x2) * gf_ref[...]
    else:
        o_ref[0] = x2


def _ffn_call(x, ac, bo, mod, layer, mod_row0, g2, woac, wob, wg, wu, wd, gf, tm, final):
    b, t, d = x.shape
    d_ff = wg.shape[2]
    ff_chunk = d_ff // 2 if (d_ff // 2) % LANES == 0 else d_ff
    kern = functools.partial(_ffn_kernel, d_model=d, ff_chunk=ff_chunk, final=final)
    row3 = lambda bi, j: (bi, j, 0)
    lay3 = lambda bi, j: (layer, 0, 0)
    once = pl.Buffered(1)
    return pl.pallas_call(
        kern,
        out_shape=jax.ShapeDtypeStruct((b, t, d), F32),
        grid=(b, t // tm),
        in_specs=[
            pl.BlockSpec((1, tm, d), row3),
            pl.BlockSpec((1, tm, A_WIDTH + C_WIDTH), row3),
            pl.BlockSpec((1, tm, B_WIDTH), row3),
            pl.BlockSpec((None, 1, 1, N_MOD * d), lambda bi, j: (layer, mod_row0 + bi, 0, 0)),
            pl.BlockSpec((None, 1, d), lay3),
            pl.BlockSpec((None, A_WIDTH + C_WIDTH, d), lay3, pipeline_mode=once),
            pl.BlockSpec((None, B_WIDTH, d), lay3, pipeline_mode=once),
            pl.BlockSpec((None, d, d_ff), lay3, pipeline_mode=once),
            pl.BlockSpec((None, d, d_ff), lay3, pipeline_mode=once),
            pl.BlockSpec((None, d_ff, d), lay3, pipeline_mode=once),
            pl.BlockSpec((1, d), lambda bi, j: (0, 0)),
        ],
        out_specs=pl.BlockSpec((1, tm, d), row3),
        compiler_params=pltpu.CompilerParams(
            dimension_semantics=("arbitrary", "arbitrary"), vmem_limit_bytes=VMEM_LIMIT),
        name="out_proj_ffn",
    )(x, ac, bo, mod, g2, woac, wob, wg, wu, wd, gf)


def _rope_tables(start, t):
    half = HEAD_DIM // 2
    inv = jnp.power(ROPE_THETA, -2.0 * jnp.arange(half, dtype=F32) / HEAD_DIM)
    ang = (start + jnp.arange(t)).astype(F32)[:, None] * inv[None, :]
    cos = jnp.cos(ang)
    sin = jnp.sin(ang)
    zero = jnp.zeros_like(sin)
    reps = LANES // HEAD_DIM
    return (jnp.tile(jnp.concatenate([cos, cos], axis=-1), (1, reps)),
            jnp.tile(jnp.concatenate([-sin, zero], axis=-1), (1, reps)),
            jnp.tile(jnp.concatenate([zero, sin], axis=-1), (1, reps)))


def _pad_w_in(w_in):
    n_head = OFF_KI + IDX_DIM + IDX_HEADS
    depth, d, _ = w_in.shape
    pad = jnp.zeros((depth, d, OFF_C - n_head), w_in.dtype)
    return jnp.concatenate([w_in[:, :, :n_head], pad, w_in[:, :, n_head:]], axis=-1).astype(BF16)


def _trunk(x, mod, mod_row0, cache, params, q_start, tm, tq):
    (g_norm1, g_norm2, w_in_p, w_spatial, b_spatial, w_dw, b_dw, g_cnorm,
     woac, wob, wg, wu, wd, g_final) = params
    b, t, d = x.shape
    depth = w_in_p.shape[0]
    ln = min(t, A_CHUNK)
    tabs = _rope_tables(q_start, t)
    wsp = w_spatial[:, :, :ln, :ln]
    bsp_tab = jnp.repeat(jnp.swapaxes(b_spatial[:, :, :ln], 1, 2), A_HEAD_DIM, axis=-1)
    ks, vs, kis, convs, avs = [], [], [], [], []
    for l in range(depth):
        if cache is None:
            cst = jnp.zeros((b, CTX_ROWS, C_WIDTH), F32)
        else:
            cst = jnp.pad(cache[3][l], ((0, 0), (CTX_PAD, 0), (0, 0)))
        q, k, v, qi, ki, wi, ac, av, cnew = _in_call(
            x, mod, l, mod_row0, g_norm1, w_in_p, tabs, wsp, bsp_tab, w_dw, b_dw, g_cnorm, cst, tm)
        if cache is None:
            bo = _attn_prompt(q, qi, wi, k, v, ki, tq)
        else:
            past = cache[0].shape[2]
            n_keys = past + t
            lk = -(-n_keys // LANES) * LANES
            grow = lambda old, new: jnp.pad(jnp.concatenate([old, new], axis=1),
                                            ((0, 0), (0, lk - n_keys), (0, 0)))
            k_all = grow(cache[0][l].reshape(b, past, B_KV_WIDTH), k)
            v_all = grow(cache[1][l].reshape(b, past, B_KV_WIDTH), v)
            ki_all = grow(cache[2][l], ki)
            bo = _attn_call(q, qi, wi, k_all, v_all, ki_all, n_keys, q_start, tq)
        x = _ffn_call(x, ac, bo, mod, l, mod_row0, g_norm2, woac, wob, wg, wu, wd, g_final,
                      tm, final=(l == depth - 1))
        ks.append(k.reshape(b, t, B_KV_HEADS, HEAD_DIM))
        vs.append(v.reshape(b, t, B_KV_HEADS, HEAD_DIM))
        kis.append(ki)
        convs.append(cnew[:, CTX_PAD:, :])
        avs.append(av)
    return x, jnp.stack(ks), jnp.stack(vs), jnp.stack(kis), jnp.stack(convs), jnp.stack(avs)


def kernel(x_prompt, x_sample, cache_k, cache_v, cache_kidx, state_conv, c_prompt, c_sample,
           w_ada, b_ada, g_norm1, g_norm2, w_in, w_spatial, b_spatial, w_dw, b_dw, g_cnorm,
           w_out, w_gate, w_up, w_down, g_final):
    depth, d = g_norm1.shape
    nb_p, t_p, _ = x_prompt.shape
    nb_s, t_s, _ = x_sample.shape
    past = cache_k.shape[2]

    rows = nb_p + nb_s
    rows_pad = -(-rows // 8) * 8
    c_all = jnp.pad(jnp.concatenate([c_prompt, c_sample], axis=0), ((0, rows_pad - rows), (0, 0)))
    mod = _mod_call(c_all, w_ada, b_ada).reshape(depth, rows_pad, 1, N_MOD * d)

    w_out_b = w_out.astype(BF16)
    params = (
        g_norm1.reshape(depth, 1, d), g_norm2.reshape(depth, 1, d), _pad_w_in(w_in),
        w_spatial, b_spatial, w_dw, b_dw.reshape(depth, 1, C_WIDTH), g_cnorm.reshape(depth, 1, C_WIDTH),
        jnp.concatenate([w_out_b[:, :A_WIDTH], w_out_b[:, A_WIDTH + B_WIDTH:]], axis=1),
        w_out_b[:, A_WIDTH:A_WIDTH + B_WIDTH],
        w_gate.astype(BF16), w_up.astype(BF16), w_down.astype(BF16), g_final.reshape(1, d),
    )

    y_p, p_k, p_v, p_ki, p_conv, _ = _trunk(
        x_prompt, mod, 0, None, params, 0, tm=min(t_p, 512), tq=min(t_p, 128))
    y_s, s_k, s_v, s_ki, s_conv, s_av = _trunk(
        x_sample, mod, nb_p, (cache_k, cache_v, cache_kidx, state_conv), params, past,
        tm=t_s, tq=t_s)
    return (y_p, y_s, p_k, p_v, p_ki, p_conv, s_k, s_v, s_ki, s_conv, s_av)
```

```python
import functools

import jax
import jax.numpy as jnp
from jax import lax
from jax.experimental import pallas as pl
from jax.experimental.pallas import tpu as pltpu

CHUNK = 64
CHUNK_SHIFT = 6
assert 1 << CHUNK_SHIFT == CHUNK
A_HEADS = 4
A_HEAD_DIM = 64
A_WIDTH = A_HEADS * A_HEAD_DIM
A_CHUNK = 128
B_HEADS = 8
B_KV_HEADS = 2
HEAD_DIM = 64
B_WIDTH = B_HEADS * HEAD_DIM
B_KV_WIDTH = B_KV_HEADS * HEAD_DIM
IDX_HEADS = 4
IDX_DIM = 64
TOPK_MAX = 256
ROPE_THETA = 10000.0
C_GROUPS = 4
C_WIDTH = 256
CONV_WIDTH = 31
N_MOD = 6
EPS = 1e-6

LANES = 128
SUBLANES = 8
BF16_SUBLANES = 16
CTX_ROWS = 32
CTX_PAD = CTX_ROWS - (CONV_WIDTH - 1)
VMEM_LIMIT = 56 * 1024 * 1024

OFF_A = 0
OFF_Q = OFF_A + 2 * A_WIDTH
OFF_K = OFF_Q + B_WIDTH
OFF_V = OFF_K + B_KV_WIDTH
OFF_QI = OFF_V + B_KV_WIDTH
OFF_KI = OFF_QI + IDX_HEADS * IDX_DIM
OFF_C = OFF_KI + LANES
N_IN_PAD = OFF_C + 2 * C_WIDTH

LOG2E = 1.4426950408889634
KEY_MIN_FINITE = -2139095040

F32 = jnp.float32
BF16 = jnp.bfloat16


def _dot(a, b):
    return jnp.dot(a, b, preferred_element_type=F32)


def _dot_t(a, b):
    return lax.dot_general(a, b, (((1,), (1,)), ((), ())), preferred_element_type=F32)


def _rms(x):
    return x * lax.rsqrt(jnp.mean(x * x, axis=-1, keepdims=True) + EPS)


def _lane_group(n, group):
    shift = group.bit_length() - 1
    assert 1 << shift == group
    return lax.shift_right_logical(lax.broadcasted_iota(jnp.int32, (1, n), 1), shift)


def _group_standardize(y, group):
    n = y.shape[-1]
    gid = _lane_group(n, group)
    inv = 1.0 / group
    mean = jnp.zeros_like(y)
    for g in range(n // group):
        m = gid == g
        s = jnp.sum(jnp.where(m, y, 0.0), axis=-1, keepdims=True) * inv
        mean = jnp.where(m, s, mean)
    yc = y - mean
    sq = yc * yc
    var = jnp.zeros_like(y)
    for g in range(n // group):
        m = gid == g
        s = jnp.sum(jnp.where(m, sq, 0.0), axis=-1, keepdims=True) * inv
        var = jnp.where(m, s, var)
    return yc * lax.rsqrt(var + EPS)


def _rope(x, cos, sin_lo, sin_hi):
    parts = []
    for c in range(x.shape[-1] // LANES):
        xc = x[:, c * LANES:(c + 1) * LANES]
        up = pltpu.roll(xc, LANES - HEAD_DIM // 2, 1)
        down = pltpu.roll(xc, HEAD_DIM // 2, 1)
        parts.append(xc * cos + up * sin_lo + down * sin_hi)
    return parts[0] if len(parts) == 1 else jnp.concatenate(parts, axis=-1)


def _mod_kernel(c_ref, w_ref, b_ref, o_ref):
    c = c_ref[...]
    cond = c * jax.nn.sigmoid(c)
    o_ref[0] = _dot(cond.astype(BF16), w_ref[0].astype(BF16)) + b_ref[0]


def _mod_call(c_all, w_ada, b_ada):
    depth, d, n = w_ada.shape
    rows = c_all.shape[0]
    tn = d
    return pl.pallas_call(
        _mod_kernel,
        out_shape=jax.ShapeDtypeStruct((depth, rows, n), F32),
        grid=(depth, n // tn),
        in_specs=[
            pl.BlockSpec((rows, d), lambda l, j: (0, 0)),
            pl.BlockSpec((1, d, tn), lambda l, j: (l, 0, j)),
            pl.BlockSpec((1, 1, tn), lambda l, j: (l, 0, j)),
        ],
        out_specs=pl.BlockSpec((1, rows, tn), lambda l, j: (l, 0, j)),
        compiler_params=pltpu.CompilerParams(
            dimension_semantics=("arbitrary", "arbitrary"), vmem_limit_bytes=VMEM_LIMIT),
        name="adaln_mod",
    )(c_all, w_ada, b_ada.reshape(depth, 1, n))


N_IN_OPERANDS = 13


def _in_kernel(*refs, tm, ln, d_model, n_aliased):
    (x_ref, mod_ref, g1_ref, w_ref, cos_ref, slo_ref, shi_ref, wsp_ref, bsp_ref,
     wdw_ref, bdw_ref, gcn_ref, cst_ref) = refs[:N_IN_OPERANDS]
    (q_ref, k_ref, v_ref, qi_ref, ki_ref, wi_ref, ac_ref, av_ref, cnew_ref,
     hp_ref, sh_ref) = refs[N_IN_OPERANDS + n_aliased:]
    j = pl.program_id(1)
    x = x_ref[0]
    mod = mod_ref[0]
    sh1 = mod[:, 0:d_model]
    sc1 = mod[:, d_model:2 * d_model]
    h = (_rms(x) * g1_ref[...]) * (1.0 + sc1) + sh1
    z = _dot(h.astype(BF16), w_ref[...])

    cos = cos_ref[...]
    slo = slo_ref[...]
    shi = shi_ref[...]

    q = _rope(z[:, OFF_Q:OFF_Q + B_WIDTH], cos, slo, shi)
    q_ref[0] = (q * (LOG2E * HEAD_DIM ** -0.5)).astype(BF16)
    k_ref[0] = _rope(z[:, OFF_K:OFF_K + B_KV_WIDTH], cos, slo, shi)
    v_ref[0] = z[:, OFF_V:OFF_V + B_KV_WIDTH]
    qi_ref[0] = _rope(z[:, OFF_QI:OFF_QI + IDX_HEADS * IDX_DIM], cos, slo, shi).astype(BF16)
    kiwi = z[:, OFF_KI:OFF_KI + LANES]
    ki_ref[0] = _rope(kiwi, cos, slo, shi)[:, 0:IDX_DIM]
    wi_ref[0] = kiwi

    za = z[:, OFF_A:OFF_A + 2 * A_WIDTH]
    za = 0.5 * za * (1.0 + lax.erf(za * (2.0 ** -0.5)))
    u = za[:, 0:A_WIDTH]
    vn = _group_standardize(za[:, A_WIDTH:2 * A_WIDTH], A_HEAD_DIM)
    av_ref[0] = vn
    vb = vn.astype(BF16)
    row = lax.broadcasted_iota(jnp.int32, (ln, ln), 0)
    col = lax.broadcasted_iota(jnp.int32, (ln, ln), 1)
    head_of_lane = _lane_group(A_WIDTH, A_HEAD_DIM)
    wsp = [jnp.where(col <= row, wsp_ref[g], 0.0).astype(BF16) for g in range(A_HEADS)]
    gated = []
    for c in range(tm // ln):
        vc = vb[c * ln:(c + 1) * ln, :]
        s = jnp.zeros((ln, A_WIDTH), F32)
        for g in range(A_HEADS):
            s = jnp.where(head_of_lane == g, _dot(wsp[g], vc), s)
        gated.append(u[c * ln:(c + 1) * ln, :] * (s + bsp_ref[...]))
    a_out = gated[0] if len(gated) == 1 else jnp.concatenate(gated, axis=0)
    ac_ref[0, :, 0:A_WIDTH] = a_out.astype(BF16)

    zc = z[:, OFF_C:OFF_C + 2 * C_WIDTH]
    hc = zc[:, 0:C_WIDTH] * jax.nn.sigmoid(zc[:, C_WIDTH:2 * C_WIDTH])

    @pl.when(j == 0)
    def _():
        hp_ref[0:CTX_ROWS, :] = cst_ref[0]

    hp_ref[CTX_ROWS:CTX_ROWS + tm, :] = hc
    span = tm + CTX_ROWS - SUBLANES
    for s in range(1, SUBLANES):
        sh_ref[s - 1, 0:span, :] = hp_ref[s:s + span, :]
    rb = min(tm, 64)
    wdw = wdw_ref[...]
    conv = []
    for r0 in range(0, tm, rb):
        acc = jnp.zeros((rb, C_WIDTH), F32)
        for t in range(CONV_WIDTH):
            phase = (CTX_PAD + t) % SUBLANES
            base = CTX_PAD + t - phase + r0
            rows = hp_ref[base:base + rb, :] if phase == 0 else sh_ref[phase - 1, base:base + rb, :]
            acc = acc + rows * wdw[t:t + 1, :]
        conv.append(acc)
    y = (conv[0] if len(conv) == 1 else jnp.concatenate(conv, axis=0)) + bdw_ref[...]
    y = _group_standardize(y, C_WIDTH // C_GROUPS) * gcn_ref[...]
    ac_ref[0, :, A_WIDTH:A_WIDTH + C_WIDTH] = (y * jax.nn.sigmoid(y)).astype(BF16)

    tail = hp_ref[tm:tm + CTX_ROWS, :]
    cnew_ref[0] = tail
    hp_ref[0:CTX_ROWS, :] = tail


def _in_call(x, mod, layer, mod_row0, g1, w_in_p, tabs, wsp, bsp_tab, wdw, bdw, gcn, cst, kv_stack, tm):
    b, t, d = x.shape
    depth = w_in_p.shape[0]
    ln = min(t, A_CHUNK)
    cos, slo, shi = tabs
    n_aliased = 0 if kv_stack is None else len(kv_stack)
    kern = functools.partial(_in_kernel, tm=tm, ln=ln, d_model=d, n_aliased=n_aliased)
    row3 = lambda bi, j: (bi, j, 0)
    lay_row4 = lambda bi, j: (layer, bi, j, 0)
    lay3 = lambda bi, j: (layer, 0, 0)
    out_shape = (
        jax.ShapeDtypeStruct((b, t, B_WIDTH), BF16),
        jax.ShapeDtypeStruct((depth, b, t, B_KV_WIDTH), F32),
        jax.ShapeDtypeStruct((depth, b, t, B_KV_WIDTH), F32),
        jax.ShapeDtypeStruct((b, t, IDX_HEADS * IDX_DIM), BF16),
        jax.ShapeDtypeStruct((depth, b, t, IDX_DIM), F32),
        jax.ShapeDtypeStruct((b, t, LANES), F32),
        jax.ShapeDtypeStruct((b, t, A_WIDTH + C_WIDTH), BF16),
        jax.ShapeDtypeStruct((b, t, A_WIDTH), F32),
        jax.ShapeDtypeStruct((b, CTX_ROWS, C_WIDTH), F32),
    )
    out_specs = (
        pl.BlockSpec((1, tm, B_WIDTH), row3),
        pl.BlockSpec((None, 1, tm, B_KV_WIDTH), lay_row4),
        pl.BlockSpec((None, 1, tm, B_KV_WIDTH), lay_row4),
        pl.BlockSpec((1, tm, IDX_HEADS * IDX_DIM), row3),
        pl.BlockSpec((None, 1, tm, IDX_DIM), lay_row4),
        pl.BlockSpec((1, tm, LANES), row3),
        pl.BlockSpec((1, tm, A_WIDTH + C_WIDTH), row3),
        pl.BlockSpec((1, tm, A_WIDTH), row3),
        pl.BlockSpec((1, CTX_ROWS, C_WIDTH), lambda bi, j: (bi, 0, 0)),
    )
    kv_outputs = (1, 2, 4)
    in_specs = [
        pl.BlockSpec((1, tm, d), row3),
        pl.BlockSpec((None, 1, 1, N_MOD * d), lambda bi, j: (layer, mod_row0 + bi, 0, 0)),
        pl.BlockSpec((None, 1, d), lay3),
        pl.BlockSpec((None, d, N_IN_PAD), lay3),
        pl.BlockSpec((tm, LANES), lambda bi, j: (j, 0)),
        pl.BlockSpec((tm, LANES), lambda bi, j: (j, 0)),
        pl.BlockSpec((tm, LANES), lambda bi, j: (j, 0)),
        pl.BlockSpec((None, A_HEADS, ln, ln), lambda bi, j: (layer, 0, 0, 0)),
        pl.BlockSpec((None, ln, A_WIDTH), lay3),
        pl.BlockSpec((None, CONV_WIDTH, C_WIDTH), lay3),
        pl.BlockSpec((None, 1, C_WIDTH), lay3),
        pl.BlockSpec((None, 1, C_WIDTH), lay3),
        pl.BlockSpec((1, CTX_ROWS, C_WIDTH), lambda bi, j: (bi, 0, 0)),
    ]
    args = [x, mod, g1, w_in_p, cos, slo, shi, wsp, bsp_tab, wdw, bdw, gcn, cst]
    assert len(args) == N_IN_OPERANDS
    aliases = {}
    if kv_stack is not None:
        for arr, out_pos in zip(kv_stack, kv_outputs):
            aliases[len(args)] = out_pos
            in_specs.append(pl.BlockSpec(memory_space=pl.ANY))
            args.append(arr)
    return pl.pallas_call(
        kern,
        out_shape=out_shape,
        grid=(b, t // tm),
        in_specs=in_specs,
        out_specs=out_specs,
        scratch_shapes=[pltpu.VMEM((tm + CTX_ROWS, C_WIDTH), F32),
                        pltpu.VMEM((SUBLANES - 1, tm + CTX_ROWS, C_WIDTH), F32)],
        input_output_aliases=aliases,
        compiler_params=pltpu.CompilerParams(
            dimension_semantics=("arbitrary", "arbitrary"), vmem_limit_bytes=VMEM_LIMIT),
        name="in_proj_mixers_ac",
    )(*args)


def _attn_kernel(q_ref, qi_ref, wi_ref, k_ref, v_ref, ki_ref, o_ref, key_ref, bias_ref,
                 *, tq, n_keys, q_start, topk):
    lk = k_ref.shape[1]
    j = pl.program_id(1)
    qchunk = lax.shift_right_logical(
        q_start + j * tq + lax.broadcasted_iota(jnp.int32, (tq, 1), 0), CHUNK_SHIFT)

    def allowed_at(k0, width):
        kpos = k0 + lax.broadcasted_iota(jnp.int32, (1, width), 1)
        return (lax.shift_right_logical(kpos, CHUNK_SHIFT) <= qchunk) & (kpos < n_keys)

    kib = ki_ref[0].astype(BF16)
    qi = qi_ref[0]
    wi = wi_ref[0]
    score = jnp.zeros((tq, lk), F32)
    for h in range(IDX_HEADS):
        d = _dot_t(qi[:, h * IDX_DIM:(h + 1) * IDX_DIM], kib)
        score = score + wi[:, IDX_DIM + h:IDX_DIM + h + 1] * jnp.maximum(d, 0.0)
    score = jnp.where(score == 0.0, 0.0, score)
    score = jnp.where(allowed_at(0, lk), score, -jnp.inf)

    bits = pltpu.bitcast(score, jnp.int32)
    key_ref[...] = bits ^ ((bits >> 31) & jnp.int32(0x7FFFFFFF))
    int_min = jnp.int32(-2 ** 31)

    def bit_step(i, thr):
        cand = thr + lax.shift_left(jnp.int32(1), jnp.int32(31) - i)
        cnt = jnp.sum((key_ref[...] >= cand).astype(F32), axis=-1, keepdims=True)
        return jnp.where(cnt >= topk, cand, thr)

    thr = lax.fori_loop(0, 32, bit_step, jnp.full((tq, 1), int_min, jnp.int32))

    need = topk - jnp.sum((key_ref[...] > thr).astype(F32), axis=-1, keepdims=True)
    tri = (lax.broadcasted_iota(jnp.int32, (LANES, LANES), 0)
           <= lax.broadcasted_iota(jnp.int32, (LANES, LANES), 1)).astype(F32).astype(BF16)
    seen = jnp.zeros((tq, 1), F32)
    for c in range(lk // LANES):
        sl = slice(c * LANES, (c + 1) * LANES)
        key_c = key_ref[:, sl]
        eq_c = key_c == thr
        eq_f = eq_c.astype(F32)
        rank = _dot(eq_f.astype(BF16), tri) + seen
        chosen = ((key_c > thr) | (eq_c & (rank <= need))) & allowed_at(c * LANES, LANES)
        bias_ref[:, sl] = jnp.where(chosen, 0.0, -jnp.inf)
        seen = seen + jnp.sum(eq_f, axis=-1, keepdims=True)

    q = q_ref[0]
    kb = k_ref[0].astype(BF16)
    vb = v_ref[0].astype(BF16)
    bias = bias_ref[...]
    group = B_HEADS // B_KV_HEADS
    outs = []
    for n in range(B_KV_HEADS):
        kn = kb[:, n * HEAD_DIM:(n + 1) * HEAD_DIM]
        vn = vb[:, n * HEAD_DIM:(n + 1) * HEAD_DIM]
        for g in range(group):
            hq = n * group + g
            logits = _dot_t(q[:, hq * HEAD_DIM:(hq + 1) * HEAD_DIM], kn) + bias
            p = jnp.exp2(logits - jnp.max(logits, axis=-1, keepdims=True))
            den = jnp.sum(p, axis=-1, keepdims=True)
            outs.append(_dot(p.astype(BF16), vn) / den)
    o_ref[0] = jnp.concatenate(outs, axis=-1).astype(BF16)


def _attn_call(q, qi, wi, k_all, v_all, ki_all, n_keys, q_start, tq):
    b, t, _ = q.shape
    lk = k_all.shape[1]
    topk = min(TOPK_MAX, n_keys // 4)
    kern = functools.partial(_attn_kernel, tq=tq, n_keys=n_keys, q_start=q_start, topk=topk)
    row3 = lambda bi, j: (bi, j, 0)
    bat3 = lambda bi, j: (bi, 0, 0)
    return pl.pallas_call(
        kern,
        out_shape=jax.ShapeDtypeStruct((b, t, B_WIDTH), BF16),
        grid=(b, t // tq),
        in_specs=[
            pl.BlockSpec((1, tq, B_WIDTH), row3),
            pl.BlockSpec((1, tq, IDX_HEADS * IDX_DIM), row3),
            pl.BlockSpec((1, tq, LANES), row3),
            pl.BlockSpec((1, lk, B_KV_WIDTH), bat3),
            pl.BlockSpec((1, lk, B_KV_WIDTH), bat3),
            pl.BlockSpec((1, lk, IDX_DIM), bat3),
        ],
        out_specs=pl.BlockSpec((1, tq, B_WIDTH), row3),
        scratch_shapes=[pltpu.VMEM((tq, lk), jnp.int32), pltpu.VMEM((tq, lk), F32)],
        compiler_params=pltpu.CompilerParams(
            dimension_semantics=("arbitrary", "arbitrary"), vmem_limit_bytes=VMEM_LIMIT),
        name="dsa_attention",
    )(q, qi, wi, k_all, v_all, ki_all)


FOLD_ROWS = 64


def _col_reduce(x, op):
    r = x.shape[0]
    if r > FOLD_ROWS and r % FOLD_ROWS == 0:
        acc = x[:FOLD_ROWS]
        for c in range(1, r // FOLD_ROWS):
            acc = op(acc, x[c * FOLD_ROWS:(c + 1) * FOLD_ROWS])
        x, r = acc, FOLD_ROWS
    while r % 16 == 0:
        x = op(x[:r // 2], x[r // 2:])
        r //= 2
    return (jnp.sum if op is jnp.add else jnp.max)(x, axis=0, keepdims=True)


WORD_BITS = 32
INT_MIN = -2 ** 31


def _bit_transpose(words):
    a = list(words)
    j, m = WORD_BITS // 2, 0x0000FFFF
    while j:
        k = 0
        while k < WORD_BITS:
            t = (a[k] ^ lax.shift_right_logical(a[k + j], j)) & m
            a[k] = a[k] ^ t
            a[k + j] = a[k + j] ^ lax.shift_left(t, j)
            k = (k + j + 1) & ~j
        j >>= 1
        m = (m ^ (m << j)) & 0x7FFFFFFF if j else m
    return a


def _select_threshold(key_ref, planes_ref, topk, tq):
    lk = key_ref.shape[0]
    slab = lk // WORD_BITS
    for r0 in range(0, slab, SUBLANES):
        words = [key_ref[i * slab + r0:i * slab + r0 + SUBLANES, :] for i in range(WORD_BITS)]
        for b, plane in enumerate(_bit_transpose(words)):
            planes_ref[b, r0:r0 + SUBLANES, :] = plane
    alive = jnp.full((slab, tq), -1, jnp.int32)
    left = jnp.full((1, tq), topk, jnp.int32)
    thr = jnp.zeros((1, tq), jnp.int32)
    for bit in range(WORD_BITS - 1, -1, -1):
        plane = planes_ref[WORD_BITS - 1 - bit]
        if bit == WORD_BITS - 1:
            plane = ~plane
        upper = alive & plane
        cnt = _col_reduce(lax.population_count(upper), jnp.add)
        take = cnt >= left
        alive = jnp.where(take, upper, alive ^ upper)
        left = jnp.where(take, left, left - cnt)
        thr = thr | jnp.where(take, jnp.int32(1 << bit if bit < WORD_BITS - 1 else INT_MIN), 0)
    return thr ^ INT_MIN, left


def _attn_t_kernel(*refs, tq, n_keys, q_start, q_block0, topk, aliased):
    q_ref, qi_ref, wi_ref, k_ref, v_ref, ki_ref = refs[:6]
    o_ref, key_ref, bias_ref, planes_ref = refs[7:] if aliased else refs[6:]
    lk = k_ref.shape[1]
    j = pl.program_id(1)
    qchunk = lax.shift_right_logical(
        q_start + (q_block0 + j) * tq + lax.broadcasted_iota(jnp.int32, (1, tq), 1), CHUNK_SHIFT)

    def allowed_at(k0, rows):
        kpos = k0 + lax.broadcasted_iota(jnp.int32, (rows, 1), 0)
        return (lax.shift_right_logical(kpos, CHUNK_SHIFT) <= qchunk) & (kpos < n_keys)

    kib = ki_ref[0].astype(BF16)
    qi = qi_ref[0]
    wi_t = wi_ref[0].T
    score = jnp.zeros((lk, tq), F32)
    for h in range(IDX_HEADS):
        d = _dot_t(kib, qi[:, h * IDX_DIM:(h + 1) * IDX_DIM])
        score = score + wi_t[IDX_DIM + h:IDX_DIM + h + 1, :] * jnp.maximum(d, 0.0)
    score = jnp.where(score == 0.0, 0.0, score)
    free = min(((q_start + q_block0 * tq) // CHUNK + 1) * CHUNK, n_keys, lk)
    if free < lk:
        masked = jnp.where(allowed_at(free, lk - free), score[free:], -jnp.inf)
        score = jnp.concatenate([score[:free], masked], axis=0) if free else masked

    bits = pltpu.bitcast(score, jnp.int32)
    key_ref[...] = bits ^ ((bits >> 31) & jnp.int32(0x7FFFFFFF))
    thr, left = _select_threshold(key_ref, planes_ref, topk, tq)

    clamped = thr < KEY_MIN_FINITE
    thr = jnp.where(clamped, KEY_MIN_FINITE, thr)
    need = jnp.where(clamped, topk, left).astype(F32)
    tril = (lax.broadcasted_iota(jnp.int32, (LANES, LANES), 0)
            >= lax.broadcasted_iota(jnp.int32, (LANES, LANES), 1)).astype(F32).astype(BF16)
    seen = jnp.zeros((1, tq), F32)
    for c in range(lk // LANES):
        sl = slice(c * LANES, (c + 1) * LANES)
        key_c = key_ref[sl, :]
        eq_c = key_c == thr
        eq_f = eq_c.astype(F32)
        rank = _dot(tril, eq_f.astype(BF16)) + seen
        chosen = (key_c > thr) | (eq_c & (rank <= need))
        bias_ref[sl, :] = jnp.where(chosen, 0.0, -jnp.inf)
        seen = seen + _col_reduce(eq_f, jnp.add)

    q = q_ref[0]
    kb = k_ref[0].astype(BF16)
    vt = v_ref[0].T.astype(BF16)
    bias = bias_ref[...]
    group = B_HEADS // B_KV_HEADS
    ones_rows = jnp.ones((BF16_SUBLANES, lk), BF16)
    outs = []
    for n in range(B_KV_HEADS):
        kn = kb[:, n * HEAD_DIM:(n + 1) * HEAD_DIM]
        vtn = jnp.concatenate([vt[n * HEAD_DIM:(n + 1) * HEAD_DIM, :], ones_rows], axis=0)
        qg = jnp.concatenate([q[:, (n * group + g) * HEAD_DIM:(n * group + g + 1) * HEAD_DIM]
                              for g in range(group)], axis=0)
        logits = _dot_t(kn, qg)
        ps = []
        for g in range(group):
            lg = logits[:, g * tq:(g + 1) * tq] + bias
            ps.append(jnp.exp2(lg - _col_reduce(lg, jnp.maximum)).astype(BF16))
        og = _dot(vtn, jnp.concatenate(ps, axis=1))
        og = og[:HEAD_DIM] / og[HEAD_DIM:HEAD_DIM + 1]
        outs.extend(og[:, g * tq:(g + 1) * tq] for g in range(group))
    o_ref[0] = jnp.concatenate(outs, axis=0).T.astype(BF16)


def _attn_t_call(q, qi, wi, k, v, ki, prev, layer, n_keys, q_start, q_block0, n_q_blocks, lk, tq):
    b, t, _ = q.shape
    topk = min(TOPK_MAX, n_keys // 4)
    aliased = prev is not None
    kern = functools.partial(_attn_t_kernel, tq=tq, n_keys=n_keys, q_start=q_start,
                             q_block0=q_block0, topk=topk, aliased=aliased)
    row3 = lambda bi, j: (bi, q_block0 + j, 0)
    lay_bat4 = lambda bi, j: (layer, bi, 0, 0)
    in_specs = [
        pl.BlockSpec((1, tq, B_WIDTH), row3),
        pl.BlockSpec((1, tq, IDX_HEADS * IDX_DIM), row3),
        pl.BlockSpec((1, tq, LANES), row3),
        pl.BlockSpec((None, 1, lk, B_KV_WIDTH), lay_bat4),
        pl.BlockSpec((None, 1, lk, B_KV_WIDTH), lay_bat4),
        pl.BlockSpec((None, 1, lk, IDX_DIM), lay_bat4),
    ]
    args = [q, qi, wi, k, v, ki]
    if aliased:
        in_specs.append(pl.BlockSpec(memory_space=pl.ANY))
        args.append(prev)
    return pl.pallas_call(
        kern,
        out_shape=jax.ShapeDtypeStruct((b, t, B_WIDTH), BF16),
        grid=(b, n_q_blocks),
        in_specs=in_specs,
        out_specs=pl.BlockSpec((1, tq, B_WIDTH), row3),
        scratch_shapes=[pltpu.VMEM((lk, tq), jnp.int32), pltpu.VMEM((lk, tq), F32),
                        pltpu.VMEM((WORD_BITS, lk // WORD_BITS, tq), jnp.int32)],
        input_output_aliases={6: 0} if aliased else {},
        compiler_params=pltpu.CompilerParams(
            dimension_semantics=("arbitrary", "arbitrary"), vmem_limit_bytes=VMEM_LIMIT),
        name="dsa_attention_t",
    )(*args)


def _attn_prompt(q, qi, wi, k, v, ki, layer, tq):
    t = q.shape[1]
    nqb = t // tq
    per_seg = max(1, WORD_BITS * SUBLANES // tq)
    assert nqb % per_seg == 0 and (per_seg * tq) % (WORD_BITS * SUBLANES) == 0
    bo = None
    for s in range(nqb // per_seg):
        bo = _attn_t_call(q, qi, wi, k, v, ki, bo, layer, t, 0, s * per_seg, per_seg,
                          (s + 1) * per_seg * tq, tq)
    return bo


def _ffn_kernel(x_ref, ac_ref, bo_ref, mod_ref, g2_ref, woac_ref, wob_ref, wg_ref, wu_ref, wd_ref,
                gf_ref, o_ref, *, d_model, ff_chunk, final):
    x = x_ref[0]
    mod = mod_ref[0]
    ga1 =mod[:, 2 * d_model:3 * d_model]
    sh2 = mod[:, 3 * d_model:4 * d_model]
    sc2 = mod[:, 4 * d_model:5 * d_model]
    ga2 = mod[:, 5 * d_model:6 * d_model]
    mix = _dot(ac_ref[0], woac_ref[...]) + _dot(bo_ref[0], wob_ref[...])
    x1 = x + ga1 * mix
    h2 = ((_rms(x1) * g2_ref[...]) * (1.0 + sc2) + sh2).astype(BF16)
    d_ff = wg_ref.shape[1]
    acc = jnp.zeros_like(x1)
    for c0 in range(0, d_ff, ff_chunk):
        gate = _dot(h2, wg_ref[:, c0:c0 + ff_chunk])
        up = _dot(h2, wu_ref[:, c0:c0 + ff_chunk])
        act = (gate * jax.nn.sigmoid(gate)) * up
        acc = acc + _dot(act.astype(BF16), wd_ref[c0:c0 + ff_chunk, :])
    x2 = x1 + ga2 * acc
    if final:
        o_ref[0] = _rms(x2) * gf_ref[...]
    else:
        o_ref[0] = x2


def _ffn_call(x, ac, bo, mod, layer, mod_row0, g2, woac, wob, wg, wu, wd, gf, tm, final):
    b, t, d = x.shape
    d_ff = wg.shape[2]
    ff_chunk = d_ff // 2 if (d_ff // 2) % LANES == 0 else d_ff
    kern = functools.partial(_ffn_kernel, d_model=d, ff_chunk=ff_chunk, final=final)
    row3 = lambda bi, j: (bi, j, 0)
    lay3 = lambda bi, j: (layer, 0, 0)
    once = pl.Buffered(1)
    return pl.pallas_call(
        kern,
        out_shape=jax.ShapeDtypeStruct((b, t, d), F32),
        grid=(b, t // tm),
        in_specs=[
            pl.BlockSpec((1, tm, d), row3),
            pl.BlockSpec((1, tm, A_WIDTH + C_WIDTH), row3),
            pl.BlockSpec((1, tm, B_WIDTH), row3),
            pl.BlockSpec((None, 1, 1, N_MOD * d), lambda bi, j: (layer, mod_row0 + bi, 0, 0)),
            pl.BlockSpec((None, 1, d), lay3),
            pl.BlockSpec((None, A_WIDTH + C_WIDTH, d), lay3, pipeline_mode=once),
            pl.BlockSpec((None, B_WIDTH, d), lay3, pipeline_mode=once),
            pl.BlockSpec((None, d, d_ff), lay3, pipeline_mode=once),
            pl.BlockSpec((None, d, d_ff), lay3, pipeline_mode=once),
            pl.BlockSpec((None, d_ff, d), lay3, pipeline_mode=once),
            pl.BlockSpec((1, d), lambda bi, j: (0, 0)),
        ],
        out_specs=pl.BlockSpec((1, tm, d), row3),
        compiler_params=pltpu.CompilerParams(
            dimension_semantics=("arbitrary", "arbitrary"), vmem_limit_bytes=VMEM_LIMIT),
        name="out_proj_ffn",
    )(x, ac, bo, mod, g2, woac, wob, wg, wu, wd, gf)


def _rope_tables(start, t):
    half = HEAD_DIM // 2
    inv = jnp.power(ROPE_THETA, -2.0 * jnp.arange(half, dtype=F32) / HEAD_DIM)
    ang = (start + jnp.arange(t)).astype(F32)[:, None] * inv[None, :]
    cos = jnp.cos(ang)
    sin = jnp.sin(ang)
    zero = jnp.zeros_like(sin)
    reps = LANES // HEAD_DIM
    return (jnp.tile(jnp.concatenate([cos, cos], axis=-1), (1, reps)),
            jnp.tile(jnp.concatenate([-sin, zero], axis=-1), (1, reps)),
            jnp.tile(jnp.concatenate([zero, sin], axis=-1), (1, reps)))


def _pad_w_in(w_in):
    n_head = OFF_KI + IDX_DIM + IDX_HEADS
    depth, d, _ = w_in.shape
    pad = jnp.zeros((depth, d, OFF_C - n_head), w_in.dtype)
    return jnp.concatenate([w_in[:, :, :n_head], pad, w_in[:, :, n_head:]], axis=-1).astype(BF16)


def _trunk(x, mod, mod_row0, cache, params, q_start, tm, tq):
    (g_norm1, g_norm2, w_in_p, w_spatial, b_spatial, w_dw, b_dw, g_cnorm,
     woac, wob, wg, wu, wd, g_final) = params
    b, t, d = x.shape
    depth = w_in_p.shape[0]
    ln = min(t, A_CHUNK)
    tabs = _rope_tables(q_start, t)
    wsp = w_spatial[:, :, :ln, :ln]
    bsp_tab = jnp.repeat(jnp.swapaxes(b_spatial[:, :, :ln], 1, 2), A_HEAD_DIM, axis=-1)
    kv_stack, convs, avs = None, [], []
    for l in range(depth):
        if cache is None:
            cst = jnp.zeros((b, CTX_ROWS, C_WIDTH), F32)
        else:
            cst = jnp.pad(cache[3][l], ((0, 0), (CTX_PAD, 0), (0, 0)))
        q, ks, vs, qi, kis, wi, ac, av, cnew = _in_call(
            x, mod, l, mod_row0, g_norm1, w_in_p, tabs, wsp, bsp_tab, w_dw, b_dw, g_cnorm, cst,
            kv_stack, tm)
        kv_stack = (ks, vs, kis)
        if cache is None:
            bo = _attn_prompt(q, qi, wi, ks, vs, kis, l, tq)
        else:
            past = cache[0].shape[2]
            n_keys = past + t
            lk = -(-n_keys // LANES) * LANES
            grow = lambda old, new: jnp.pad(jnp.concatenate([old, new], axis=1),
                                            ((0, 0), (0, lk - n_keys), (0, 0)))
            k_all = grow(cache[0][l].reshape(b, past, B_KV_WIDTH), ks[l])
            v_all = grow(cache[1][l].reshape(b, past, B_KV_WIDTH), vs[l])
            ki_all = grow(cache[2][l], kis[l])
            bo = _attn_call(q, qi, wi, k_all, v_all, ki_all, n_keys, q_start, tq)
        x = _ffn_call(x, ac, bo, mod, l, mod_row0, g_norm2, woac, wob, wg, wu, wd, g_final,
                      tm, final=(l == depth - 1))
        convs.append(cnew[:, CTX_PAD:, :])
        avs.append(av)
    ks, vs, kis = kv_stack
    return (x, ks.reshape(depth, b, t, B_KV_HEADS, HEAD_DIM), vs.reshape(depth, b, t, B_KV_HEADS, HEAD_DIM),
            kis, jnp.stack(convs), jnp.stack(avs))


def kernel(x_prompt, x_sample, cache_k, cache_v, cache_kidx, state_conv, c_prompt, c_sample,
           w_ada, b_ada, g_norm1, g_norm2, w_in, w_spatial, b_spatial, w_dw, b_dw, g_cnorm,
           w_out, w_gate, w_up, w_down, g_final):
    depth, d = g_norm1.shape
    nb_p, t_p, _ = x_prompt.shape
    nb_s, t_s, _ = x_sample.shape
    past = cache_k.shape[2]

    rows = nb_p + nb_s
    rows_pad = -(-rows // 8) * 8
    c_all = jnp.pad(jnp.concatenate([c_prompt, c_sample], axis=0), ((0, rows_pad - rows), (0, 0)))
    mod = _mod_call(c_all, w_ada, b_ada).reshape(depth, rows_pad, 1, N_MOD * d)

    w_out_b = w_out.astype(BF16)
    params = (
        g_norm1.reshape(depth, 1, d), g_norm2.reshape(depth, 1, d), _pad_w_in(w_in),
        w_spatial, b_spatial, w_dw, b_dw.reshape(depth, 1, C_WIDTH), g_cnorm.reshape(depth, 1, C_WIDTH),
        jnp.concatenate([w_out_b[:, :A_WIDTH], w_out_b[:, A_WIDTH + B_WIDTH:]], axis=1),
        w_out_b[:, A_WIDTH:A_WIDTH + B_WIDTH],
        w_gate.astype(BF16), w_up.astype(BF16), w_down.astype(BF16), g_final.reshape(1, d),
    )

    y_p, p_k, p_v, p_ki, p_conv, _ = _trunk(
        x_prompt, mod, 0, None, params, 0, tm=min(t_p, 512), tq=min(t_p, 256))
    y_s, s_k, s_v, s_ki, s_conv, s_av = _trunk(
        x_sample, mod, nb_p, (cache_k, cache_v, cache_kidx, state_conv), params, past,
        tm=t_s, tq=t_s)
    return (y_p, y_s, p_k, p_v, p_ki, p_conv, s_k, s_v, s_ki, s_conv, s_av)
```

```python
import functools

import jax
import jax.numpy as jnp
from jax import lax
from jax.experimental import pallas as pl
from jax.experimental.pallas import tpu as pltpu

CHUNK = 64
CHUNK_SHIFT = 6
assert 1 << CHUNK_SHIFT == CHUNK
A_HEADS = 4
A_HEAD_DIM = 64
A_WIDTH = A_HEADS * A_HEAD_DIM
A_CHUNK = 128
B_HEADS = 8
B_KV_HEADS = 2
HEAD_DIM = 64
B_WIDTH = B_HEADS * HEAD_DIM
B_KV_WIDTH = B_KV_HEADS * HEAD_DIM
IDX_HEADS = 4
IDX_DIM = 64
TOPK_MAX = 256
ROPE_THETA = 10000.0
C_GROUPS = 4
C_WIDTH = 256
CONV_WIDTH = 31
N_MOD = 6
EPS = 1e-6

LANES = 128
SUBLANES = 8
BF16_SUBLANES = 16
CTX_ROWS = 32
CTX_PAD = CTX_ROWS - (CONV_WIDTH - 1)
VMEM_LIMIT = 56 * 1024 * 1024

OFF_A = 0
OFF_Q = OFF_A + 2 * A_WIDTH
OFF_K = OFF_Q + B_WIDTH
OFF_V = OFF_K + B_KV_WIDTH
OFF_QI = OFF_V + B_KV_WIDTH
OFF_KI = OFF_QI + IDX_HEADS * IDX_DIM
OFF_C = OFF_KI + LANES
N_IN_PAD = OFF_C + 2 * C_WIDTH

LOG2E = 1.4426950408889634
KEY_MIN_FINITE = -2139095040

F32 = jnp.float32
BF16 = jnp.bfloat16


def _dot(a, b):
    return jnp.dot(a, b, preferred_element_type=F32)


def _dot_t(a, b):
    return lax.dot_general(a, b, (((1,), (1,)), ((), ())), preferred_element_type=F32)


def _rms(x):
    return x * lax.rsqrt(jnp.mean(x * x, axis=-1, keepdims=True) + EPS)


def _lane_group(n, group):
    shift = group.bit_length() - 1
    assert 1 << shift == group
    return lax.shift_right_logical(lax.broadcasted_iota(jnp.int32, (1, n), 1), shift)


def _group_standardize(y, group):
    n = y.shape[-1]
    gid = _lane_group(n, group)
    inv = 1.0 / group
    mean = jnp.zeros_like(y)
    for g in range(n // group):
        m = gid == g
        s = jnp.sum(jnp.where(m, y, 0.0), axis=-1, keepdims=True) * inv
        mean = jnp.where(m, s, mean)
    yc = y - mean
    sq = yc * yc
    var = jnp.zeros_like(y)
    for g in range(n // group):
        m = gid == g
        s = jnp.sum(jnp.where(m, sq, 0.0), axis=-1, keepdims=True) * inv
        var = jnp.where(m, s, var)
    return yc * lax.rsqrt(var + EPS)


def _rope(x, cos, sin_lo, sin_hi):
    parts = []
    for c in range(x.shape[-1] // LANES):
        xc = x[:, c * LANES:(c + 1) * LANES]
        up = pltpu.roll(xc, LANES - HEAD_DIM // 2, 1)
        down = pltpu.roll(xc, HEAD_DIM // 2, 1)
        parts.append(xc * cos + up * sin_lo + down * sin_hi)
    return parts[0] if len(parts) == 1 else jnp.concatenate(parts, axis=-1)


def _mod_kernel(c_ref, w_ref, b_ref, o_ref):
    c = c_ref[...]
    cond = c * jax.nn.sigmoid(c)
    o_ref[0] = _dot(cond.astype(BF16), w_ref[0].astype(BF16)) + b_ref[0]


def _mod_call(c_all, w_ada, b_ada):
    depth, d, n = w_ada.shape
    rows = c_all.shape[0]
    tn = d
    return pl.pallas_call(
        _mod_kernel,
        out_shape=jax.ShapeDtypeStruct((depth, rows, n), F32),
        grid=(depth, n // tn),
        in_specs=[
            pl.BlockSpec((rows, d), lambda l, j: (0, 0)),
            pl.BlockSpec((1, d, tn), lambda l, j: (l, 0, j)),
            pl.BlockSpec((1, 1, tn), lambda l, j: (l, 0, j)),
        ],
        out_specs=pl.BlockSpec((1, rows, tn), lambda l, j: (l, 0, j)),
        compiler_params=pltpu.CompilerParams(
            dimension_semantics=("arbitrary", "arbitrary"), vmem_limit_bytes=VMEM_LIMIT),
        name="adaln_mod",
    )(c_all, w_ada, b_ada.reshape(depth, 1, n))


N_IN_OPERANDS = 13


def _in_kernel(*refs, tm, ln, d_model, n_aliased):
    (x_ref, mod_ref, g1_ref, w_ref, cos_ref, slo_ref, shi_ref, wsp_ref, bsp_ref,
     wdw_ref, bdw_ref, gcn_ref, cst_ref) = refs[:N_IN_OPERANDS]
    (q_ref, k_ref, v_ref, qi_ref, ki_ref, wi_ref, ac_ref, av_ref, cnew_ref,
     hp_ref, sh_ref) = refs[N_IN_OPERANDS + n_aliased:]
    j = pl.program_id(1)
    x = x_ref[0]
    mod = mod_ref[0]
    sh1 = mod[:, 0:d_model]
    sc1 = mod[:, d_model:2 * d_model]
    h = (_rms(x) * g1_ref[...]) * (1.0 + sc1) + sh1
    z = _dot(h.astype(BF16), w_ref[...])

    cos = cos_ref[...]
    slo = slo_ref[...]
    shi = shi_ref[...]

    q = _rope(z[:, OFF_Q:OFF_Q + B_WIDTH], cos, slo, shi)
    q_ref[0] = (q * (LOG2E * HEAD_DIM ** -0.5)).astype(BF16)
    k_ref[0] = _rope(z[:, OFF_K:OFF_K + B_KV_WIDTH], cos, slo, shi)
    v_ref[0] = z[:, OFF_V:OFF_V + B_KV_WIDTH]
    qi_ref[0] = _rope(z[:, OFF_QI:OFF_QI + IDX_HEADS * IDX_DIM], cos, slo, shi).astype(BF16)
    kiwi = z[:, OFF_KI:OFF_KI + LANES]
    ki_ref[0] = _rope(kiwi, cos, slo, shi)[:, 0:IDX_DIM]
    wi_ref[0] = kiwi

    za = z[:, OFF_A:OFF_A + 2 * A_WIDTH]
    za = 0.5 * za * (1.0 + lax.erf(za * (2.0 ** -0.5)))
    u = za[:, 0:A_WIDTH]
    vn = _group_standardize(za[:, A_WIDTH:2 * A_WIDTH], A_HEAD_DIM)
    av_ref[0] = vn
    vb = vn.astype(BF16)
    row = lax.broadcasted_iota(jnp.int32, (ln, ln), 0)
    col = lax.broadcasted_iota(jnp.int32, (ln, ln), 1)
    head_of_lane = _lane_group(A_WIDTH, A_HEAD_DIM)
    wsp = [jnp.where(col <= row, wsp_ref[g], 0.0).astype(BF16) for g in range(A_HEADS)]
    gated = []
    for c in range(tm // ln):
        vc = vb[c * ln:(c + 1) * ln, :]
        s = jnp.zeros((ln, A_WIDTH), F32)
        for g in range(A_HEADS):
            s = jnp.where(head_of_lane == g, _dot(wsp[g], vc), s)
        gated.append(u[c * ln:(c + 1) * ln, :] * (s + bsp_ref[...]))
    a_out = gated[0] if len(gated) == 1 else jnp.concatenate(gated, axis=0)
    ac_ref[0, :, 0:A_WIDTH] = a_out.astype(BF16)

    zc = z[:, OFF_C:OFF_C + 2 * C_WIDTH]
    hc = zc[:, 0:C_WIDTH] * jax.nn.sigmoid(zc[:, C_WIDTH:2 * C_WIDTH])

    @pl.when(j == 0)
    def _():
        hp_ref[0:CTX_ROWS, :] = cst_ref[0]

    hp_ref[CTX_ROWS:CTX_ROWS + tm, :] = hc
    span = tm + CTX_ROWS - SUBLANES
    for s in range(1, SUBLANES):
        sh_ref[s - 1, 0:span, :] = hp_ref[s:s + span, :]
    rb = min(tm, 64)
    wdw = wdw_ref[...]
    conv = []
    for r0 in range(0, tm, rb):
        acc = jnp.zeros((rb, C_WIDTH), F32)
        for t in range(CONV_WIDTH):
            phase = (CTX_PAD + t) % SUBLANES
            base = CTX_PAD + t - phase + r0
            rows = hp_ref[base:base + rb, :] if phase == 0 else sh_ref[phase - 1, base:base + rb, :]
            acc = acc + rows * wdw[t:t + 1, :]
        conv.append(acc)
    y = (conv[0] if len(conv) == 1 else jnp.concatenate(conv, axis=0)) + bdw_ref[...]
    y = _group_standardize(y, C_WIDTH // C_GROUPS) * gcn_ref[...]
    ac_ref[0, :, A_WIDTH:A_WIDTH + C_WIDTH] = (y * jax.nn.sigmoid(y)).astype(BF16)

    tail = hp_ref[tm:tm + CTX_ROWS, :]
    cnew_ref[0] = tail
    hp_ref[0:CTX_ROWS, :] = tail


def _in_call(x, mod, layer, mod_row0, g1, w_in_p, tabs, wsp, bsp_tab, wdw, bdw, gcn, cst, kv_stack, tm):
    b, t, d = x.shape
    depth = w_in_p.shape[0]
    ln = min(t, A_CHUNK)
    cos, slo, shi = tabs
    n_aliased = 0 if kv_stack is None else len(kv_stack)
    kern = functools.partial(_in_kernel, tm=tm, ln=ln, d_model=d, n_aliased=n_aliased)
    row3 = lambda bi, j: (bi, j, 0)
    lay_row4 = lambda bi, j: (layer, bi, j, 0)
    lay3 = lambda bi, j: (layer, 0, 0)
    out_shape = (
        jax.ShapeDtypeStruct((b, t, B_WIDTH), BF16),
        jax.ShapeDtypeStruct((depth, b, t, B_KV_WIDTH), F32),
        jax.ShapeDtypeStruct((depth, b, t, B_KV_WIDTH), F32),
        jax.ShapeDtypeStruct((b, t, IDX_HEADS * IDX_DIM), BF16),
        jax.ShapeDtypeStruct((depth, b, t, IDX_DIM), F32),
        jax.ShapeDtypeStruct((b, t, LANES), F32),
        jax.ShapeDtypeStruct((b, t, A_WIDTH + C_WIDTH), BF16),
        jax.ShapeDtypeStruct((b, t, A_WIDTH), F32),
        jax.ShapeDtypeStruct((b, CTX_ROWS, C_WIDTH), F32),
    )
    out_specs = (
        pl.BlockSpec((1, tm, B_WIDTH), row3),
        pl.BlockSpec((None, 1, tm, B_KV_WIDTH), lay_row4),
        pl.BlockSpec((None, 1, tm, B_KV_WIDTH), lay_row4),
        pl.BlockSpec((1, tm, IDX_HEADS * IDX_DIM), row3),
        pl.BlockSpec((None, 1, tm, IDX_DIM), lay_row4),
        pl.BlockSpec((1, tm, LANES), row3),
        pl.BlockSpec((1, tm, A_WIDTH + C_WIDTH), row3),
        pl.BlockSpec((1, tm, A_WIDTH), row3),
        pl.BlockSpec((1, CTX_ROWS, C_WIDTH), lambda bi, j: (bi, 0, 0)),
    )
    kv_outputs = (1, 2, 4)
    in_specs = [
        pl.BlockSpec((1, tm, d), row3),
        pl.BlockSpec((None, 1, 1, N_MOD * d), lambda bi, j: (layer, mod_row0 + bi, 0, 0)),
        pl.BlockSpec((None, 1, d), lay3),
        pl.BlockSpec((None, d, N_IN_PAD), lay3),
        pl.BlockSpec((tm, LANES), lambda bi, j: (j, 0)),
        pl.BlockSpec((tm, LANES), lambda bi, j: (j, 0)),
        pl.BlockSpec((tm, LANES), lambda bi, j: (j, 0)),
        pl.BlockSpec((None, A_HEADS, ln, ln), lambda bi, j: (layer, 0, 0, 0)),
        pl.BlockSpec((None, ln, A_WIDTH), lay3),
        pl.BlockSpec((None, CONV_WIDTH, C_WIDTH), lay3),
        pl.BlockSpec((None, 1, C_WIDTH), lay3),
        pl.BlockSpec((None, 1, C_WIDTH), lay3),
        pl.BlockSpec((1, CTX_ROWS, C_WIDTH), lambda bi, j: (bi, 0, 0)),
    ]
    args = [x, mod, g1, w_in_p, cos, slo, shi, wsp, bsp_tab, wdw, bdw, gcn, cst]
    assert len(args) == N_IN_OPERANDS
    aliases = {}
    if kv_stack is not None:
        for arr, out_pos in zip(kv_stack, kv_outputs):
            aliases[len(args)] = out_pos
            in_specs.append(pl.BlockSpec(memory_space=pl.ANY))
            args.append(arr)
    return pl.pallas_call(
        kern,
        out_shape=out_shape,
        grid=(b, t // tm),
        in_specs=in_specs,
        out_specs=out_specs,
        scratch_shapes=[pltpu.VMEM((tm + CTX_ROWS, C_WIDTH), F32),
                        pltpu.VMEM((SUBLANES - 1, tm + CTX_ROWS, C_WIDTH), F32)],
        input_output_aliases=aliases,
        compiler_params=pltpu.CompilerParams(
            dimension_semantics=("arbitrary", "arbitrary"), vmem_limit_bytes=VMEM_LIMIT),
        name="in_proj_mixers_ac",
    )(*args)


def _attn_kernel(q_ref, qi_ref, wi_ref, ck_ref, cv_ref, cki_ref, k_ref, v_ref, ki_ref, o_ref,
                 kb_ref, vb_ref, kib_ref, key_ref, bias_ref, *, tq, q_start, topk):
    past = ck_ref.shape[1]
    n_keys = past + k_ref.shape[1]
    lk = kb_ref.shape[0]
    for dst, old, new in ((kb_ref, ck_ref, k_ref), (vb_ref, cv_ref, v_ref), (kib_ref, cki_ref, ki_ref)):
        dst[0:past, :] = old[0].astype(BF16)
        dst[past:n_keys, :] = new[0].astype(BF16)
        dst[n_keys:lk, :] = jnp.zeros((lk - n_keys, dst.shape[1]), BF16)
    j = pl.program_id(1)
    qchunk = lax.shift_right_logical(
        q_start + j * tq + lax.broadcasted_iota(jnp.int32, (tq, 1), 0), CHUNK_SHIFT)

    def allowed_at(k0, width):
        kpos = k0 + lax.broadcasted_iota(jnp.int32, (1, width), 1)
        return (lax.shift_right_logical(kpos, CHUNK_SHIFT) <= qchunk) & (kpos < n_keys)

    kib = kib_ref[...]
    qi = qi_ref[0]
    wi = wi_ref[0]
    score = jnp.zeros((tq, lk), F32)
    for h in range(IDX_HEADS):
        d = _dot_t(qi[:, h * IDX_DIM:(h + 1) * IDX_DIM], kib)
        score = score + wi[:, IDX_DIM + h:IDX_DIM + h + 1] * jnp.maximum(d, 0.0)
    score = jnp.where(score == 0.0, 0.0, score)
    score = jnp.where(allowed_at(0, lk), score, -jnp.inf)

    bits = pltpu.bitcast(score, jnp.int32)
    key_ref[...] = bits ^ ((bits >> 31) & jnp.int32(0x7FFFFFFF))
    int_min = jnp.int32(-2 ** 31)

    def digit_step(i, thr):
        unit = lax.shift_left(jnp.int32(1), jnp.int32(30) - 2 * i)
        key = key_ref[...]
        for mult in (1, 2, 3):
            cand = thr + mult * unit if mult == 1 else cand + unit
            cnt = jnp.sum((key >= cand).astype(F32), axis=-1, keepdims=True)
            best = jnp.where(cnt >= topk, cand, thr if mult == 1 else best)
        return best

    thr = lax.fori_loop(0, 16, digit_step, jnp.full((tq, 1), int_min, jnp.int32))

    need = topk - jnp.sum((key_ref[...] > thr).astype(F32), axis=-1, keepdims=True)
    tri = (lax.broadcasted_iota(jnp.int32, (LANES, LANES), 0)
           <= lax.broadcasted_iota(jnp.int32, (LANES, LANES), 1)).astype(F32).astype(BF16)
    seen = jnp.zeros((tq, 1), F32)
    for c in range(lk // LANES):
        sl = slice(c * LANES, (c + 1) * LANES)
        key_c = key_ref[:, sl]
        eq_c = key_c == thr
        eq_f = eq_c.astype(F32)
        rank = _dot(eq_f.astype(BF16), tri) + seen
        chosen = ((key_c > thr) | (eq_c & (rank <= need))) & allowed_at(c * LANES, LANES)
        bias_ref[:, sl] = jnp.where(chosen, 0.0, -jnp.inf)
        seen = seen + jnp.sum(eq_f, axis=-1, keepdims=True)

    q = q_ref[0]
    group = B_HEADS // B_KV_HEADS
    bias = jnp.concatenate([bias_ref[...]] * group, axis=0)
    outs = []
    for n in range(B_KV_HEADS):
        kn = kb_ref[:, n * HEAD_DIM:(n + 1) * HEAD_DIM]
        vn = vb_ref[:, n * HEAD_DIM:(n + 1) * HEAD_DIM]
        qg = jnp.concatenate([q[:, (n * group + g) * HEAD_DIM:(n * group + g + 1) * HEAD_DIM]
                              for g in range(group)], axis=0)
        logits = _dot_t(qg, kn) + bias
        p = jnp.exp2(logits - jnp.max(logits, axis=-1, keepdims=True))
        den = jnp.sum(p, axis=-1, keepdims=True)
        og = _dot(p.astype(BF16), vn) / den
        outs.extend(og[g * tq:(g + 1) * tq] for g in range(group))
    o_ref[0] = jnp.concatenate(outs, axis=-1).astype(BF16)


def _attn_call(q, qi, wi, cache_k, cache_v, cache_ki, k, v, ki, layer, q_start):
    b, t, _ = q.shape
    past = cache_k.shape[2]
    assert past % BF16_SUBLANES == 0 and t % BF16_SUBLANES == 0
    n_keys = past + t
    lk = -(-n_keys // LANES) * LANES
    topk = min(TOPK_MAX, n_keys // 4)
    kern = functools.partial(_attn_kernel, tq=t, q_start=q_start, topk=topk)
    row3 = lambda bi, j: (bi, 0, 0)
    lay_bat4 = lambda bi, j: (layer, bi, 0, 0)
    return pl.pallas_call(
        kern,
        out_shape=jax.ShapeDtypeStruct((b, t, B_WIDTH), BF16),
        grid=(b, 1),
        in_specs=[
            pl.BlockSpec((1, t, B_WIDTH), row3),
            pl.BlockSpec((1, t, IDX_HEADS * IDX_DIM), row3),
            pl.BlockSpec((1, t, LANES), row3),
            pl.BlockSpec((None, 1, past, B_KV_WIDTH), lay_bat4),
            pl.BlockSpec((None, 1, past, B_KV_WIDTH), lay_bat4),
            pl.BlockSpec((None, 1, past, IDX_DIM), lay_bat4),
            pl.BlockSpec((None, 1, t, B_KV_WIDTH), lay_bat4),
            pl.BlockSpec((None, 1, t, B_KV_WIDTH), lay_bat4),
            pl.BlockSpec((None, 1, t, IDX_DIM), lay_bat4),
        ],
        out_specs=pl.BlockSpec((1, t, B_WIDTH), row3),
        scratch_shapes=[pltpu.VMEM((lk, B_KV_WIDTH), BF16), pltpu.VMEM((lk, B_KV_WIDTH), BF16),
                        pltpu.VMEM((lk, IDX_DIM), BF16),
                        pltpu.VMEM((t, lk), jnp.int32), pltpu.VMEM((t, lk), F32)],
        compiler_params=pltpu.CompilerParams(
            dimension_semantics=("arbitrary", "arbitrary"), vmem_limit_bytes=VMEM_LIMIT),
        name="dsa_attention",
    )(q, qi, wi, cache_k, cache_v, cache_ki, k, v, ki)


FOLD_ROWS = 64


def _col_reduce(x, op):
    r = x.shape[0]
    if r > FOLD_ROWS and r % FOLD_ROWS == 0:
        acc = x[:FOLD_ROWS]
        for c in range(1, r // FOLD_ROWS):
            acc = op(acc, x[c * FOLD_ROWS:(c + 1) * FOLD_ROWS])
        x, r = acc, FOLD_ROWS
    while r % 16 == 0:
        x = op(x[:r // 2], x[r // 2:])
        r //= 2
    return (jnp.sum if op is jnp.add else jnp.max)(x, axis=0, keepdims=True)


WORD_BITS = 32
INT_MIN = -2 ** 31
QK_ROWS = 256


def _bit_transpose(words):
    a = list(words)
    j, m = WORD_BITS // 2, 0x0000FFFF
    while j:
        k = 0
        while k < WORD_BITS:
            t = (a[k] ^ lax.shift_right_logical(a[k + j], j)) & m
            a[k] = a[k] ^ t
            a[k + j] = a[k + j] ^ lax.shift_left(t, j)
            k = (k + j + 1) & ~j
        j >>= 1
        m = (m ^ (m << j)) & 0x7FFFFFFF if j else m
    return a


def _select_threshold(key_ref, planes_ref, topk, tq, side_work=()):
    lk = key_ref.shape[0]
    slab = lk // WORD_BITS
    for r0 in range(0, slab, SUBLANES):
        words = [key_ref[i * slab + r0:i * slab + r0 + SUBLANES, :] for i in range(WORD_BITS)]
        for b, plane in enumerate(_bit_transpose(words)):
            planes_ref[b, r0:r0 + SUBLANES, :] = plane
    count = lambda words: _col_reduce(lax.population_count(words), jnp.add)
    alive = jnp.full((slab, tq), -1, jnp.int32)
    left = jnp.full((1, tq), topk, jnp.int32)
    thr = jnp.zeros((1, tq), jnp.int32)
    steps = WORD_BITS // 2
    for step, hi_bit in enumerate(range(WORD_BITS - 1, 0, -2)):
        for piece in side_work[step * len(side_work) // steps:(step + 1) * len(side_work) // steps]:
            piece()
        p_hi = planes_ref[WORD_BITS - 1 - hi_bit]
        p_lo = planes_ref[WORD_BITS - hi_bit]
        if hi_bit == WORD_BITS - 1:
            p_hi = ~p_hi
        u1 = alive & p_hi
        u0 = alive ^ u1
        c11 = u1 & p_lo
        c10 = u1 ^ c11
        c01 = u0 & p_lo
        c00 = u0 ^ c01
        n11 = count(c11)
        n1x = n11 + count(c10)
        n_1 = n1x + count(c01)
        t11 = n11 >= left
        t1x = n1x >= left
        t_1 = n_1 >= left
        alive = jnp.where(t11, c11, jnp.where(t1x, c10, jnp.where(t_1, c01, c00)))
        left = left - jnp.where(t11, 0, jnp.where(t1x, n11, jnp.where(t_1, n1x, n_1)))
        hi_val = jnp.int32(1 << hi_bit if hi_bit < WORD_BITS - 1 else INT_MIN)
        lo_val = jnp.int32(1 << (hi_bit - 1))
        thr = thr | jnp.where(t1x, hi_val, 0) | jnp.where(t11 | (t_1 & ~t1x), lo_val, 0)
    return thr ^ INT_MIN, left


def _attn_t_kernel(*refs, tq, n_keys, q_start, q_block0, topk, aliased):
    q_ref, qi_ref, wi_ref, k_ref, v_ref, ki_ref = refs[:6]
    o_ref, key_ref, bias_ref, planes_ref, logit_ref = refs[7:] if aliased else refs[6:]
    lk = k_ref.shape[1]
    j = pl.program_id(1)
    qchunk = lax.shift_right_logical(
        q_start + (q_block0 + j) * tq + lax.broadcasted_iota(jnp.int32, (1, tq), 1), CHUNK_SHIFT)

    def allowed_at(k0, rows):
        kpos = k0 + lax.broadcasted_iota(jnp.int32, (rows, 1), 0)
        return (lax.shift_right_logical(kpos, CHUNK_SHIFT) <= qchunk) & (kpos < n_keys)

    kib = ki_ref[0].astype(BF16)
    qi = qi_ref[0]
    wi_t = wi_ref[0].T
    score = jnp.zeros((lk, tq), F32)
    for h in range(IDX_HEADS):
        d = _dot_t(kib, qi[:, h * IDX_DIM:(h + 1) * IDX_DIM])
        score = score + wi_t[IDX_DIM + h:IDX_DIM + h + 1, :] * jnp.maximum(d, 0.0)
    score = jnp.where(score == 0.0, 0.0, score)
    free = min(((q_start + q_block0 * tq) // CHUNK + 1) * CHUNK, n_keys, lk)
    if free < lk:
        masked = jnp.where(allowed_at(free, lk - free), score[free:], -jnp.inf)
        score = jnp.concatenate([score[:free], masked], axis=0) if free else masked

    bits = pltpu.bitcast(score, jnp.int32)
    key_ref[...] = bits ^ ((bits >> 31) & jnp.int32(0x7FFFFFFF))

    q = q_ref[0]
    group = B_HEADS // B_KV_HEADS
    qgs = [jnp.concatenate([q[:, (n * group + g) * HEAD_DIM:(n * group + g + 1) * HEAD_DIM]
                            for g in range(group)], axis=0) for n in range(B_KV_HEADS)]

    def logits_block(r0, n):
        kn = k_ref[0, r0:r0 + QK_ROWS, n * HEAD_DIM:(n + 1) * HEAD_DIM].astype(BF16)
        logit_ref[n, r0:r0 + QK_ROWS, :] = _dot_t(kn, qgs[n])

    side_work = [functools.partial(logits_block, r0, n)
                 for r0 in range(0, lk, QK_ROWS) for n in range(B_KV_HEADS)]
    thr, left = _select_threshold(key_ref, planes_ref, topk, tq, side_work)

    clamped = thr < KEY_MIN_FINITE
    thr = jnp.where(clamped, KEY_MIN_FINITE, thr)
    need = jnp.where(clamped, topk, left).astype(F32)
    tril = (lax.broadcasted_iota(jnp.int32, (LANES, LANES), 0)
            >= lax.broadcasted_iota(jnp.int32, (LANES, LANES), 1)).astype(F32).astype(BF16)
    seen = jnp.zeros((1, tq), F32)
    for c in range(lk // LANES):
        sl = slice(c * LANES, (c + 1) * LANES)
        key_c = key_ref[sl, :]
        eq_c = key_c == thr
        eq_f = eq_c.astype(F32)
        rank = _dot(tril, eq_f.astype(BF16)) + seen
        chosen = (key_c > thr) | (eq_c & (rank <= need))
        bias_ref[sl, :] = jnp.where(chosen, 0.0, -jnp.inf)
        seen = seen + _col_reduce(eq_f, jnp.add)

    vt = v_ref[0].T.astype(BF16)
    bias = bias_ref[...]
    ones_rows = jnp.ones((BF16_SUBLANES, lk), BF16)
    outs = []
    for n in range(B_KV_HEADS):
        vtn = jnp.concatenate([vt[n * HEAD_DIM:(n + 1) * HEAD_DIM, :], ones_rows], axis=0)
        ps = []
        for g in range(group):
            lg = logit_ref[n, :, g * tq:(g + 1) * tq] + bias
            ps.append(jnp.exp2(lg - _col_reduce(lg, jnp.maximum)).astype(BF16))
        og = _dot(vtn, jnp.concatenate(ps, axis=1))
        og = og[:HEAD_DIM] / og[HEAD_DIM:HEAD_DIM + 1]
        outs.extend(og[:, g * tq:(g + 1) * tq] for g in range(group))
    o_ref[0] = jnp.concatenate(outs, axis=0).T.astype(BF16)


def _attn_t_call(q, qi, wi, k, v, ki, prev, layer, n_keys, q_start, q_block0, n_q_blocks, lk, tq):
    b, t, _ = q.shape
    topk = min(TOPK_MAX, n_keys // 4)
    aliased = prev is not None
    kern = functools.partial(_attn_t_kernel, tq=tq, n_keys=n_keys, q_start=q_start,
                             q_block0=q_block0, topk=topk, aliased=aliased)
    row3 = lambda bi, j: (bi, q_block0 + j, 0)
    lay_bat4 = lambda bi, j: (layer, bi, 0, 0)
    in_specs = [
        pl.BlockSpec((1, tq, B_WIDTH), row3),
        pl.BlockSpec((1, tq, IDX_HEADS * IDX_DIM), row3),
        pl.BlockSpec((1, tq, LANES), row3),
        pl.BlockSpec((None, 1, lk, B_KV_WIDTH), lay_bat4),
        pl.BlockSpec((None, 1, lk, B_KV_WIDTH), lay_bat4),
        pl.BlockSpec((None, 1, lk, IDX_DIM), lay_bat4),
    ]
    args = [q, qi, wi, k, v, ki]
    if aliased:
        in_specs.append(pl.BlockSpec(memory_space=pl.ANY))
        args.append(prev)
    return pl.pallas_call(
        kern,
        out_shape=jax.ShapeDtypeStruct((b, t, B_WIDTH), BF16),
        grid=(b, n_q_blocks),
        in_specs=in_specs,
        out_specs=pl.BlockSpec((1, tq, B_WIDTH), row3),
        scratch_shapes=[pltpu.VMEM((lk, tq), jnp.int32), pltpu.VMEM((lk, tq), F32),
                        pltpu.VMEM((WORD_BITS, lk // WORD_BITS, tq), jnp.int32),
                        pltpu.VMEM((B_KV_HEADS, lk, B_HEADS // B_KV_HEADS * tq), F32)],
        input_output_aliases={6: 0} if aliased else {},
        compiler_params=pltpu.CompilerParams(
            dimension_semantics=("arbitrary", "arbitrary"), vmem_limit_bytes=VMEM_LIMIT),
        name="dsa_attention_t",
    )(*args)


def _attn_prompt(q, qi, wi, k, v, ki, layer, tq):
    t = q.shape[1]
    nqb = t // tq
    per_seg = max(1, WORD_BITS * SUBLANES // tq)
    assert nqb % per_seg == 0 and (per_seg * tq) % (WORD_BITS * SUBLANES) == 0
    bo = None
    for s in range(nqb // per_seg):
        bo = _attn_t_call(q, qi, wi, k, v, ki, bo, layer, t, 0, s * per_seg, per_seg,
                          (s + 1) * per_seg * tq, tq)
    return bo


def _ffn_kernel(x_ref, ac_ref, bo_ref, mod_ref, g2_ref, woac_ref, wob_ref, wg_ref, wu_ref, wd_ref,
                gf_ref, o_ref, *, d_model, ff_chunk, final):
    x = x_ref[0]
    mod = mod_ref[0]
    ga1 =mod[:, 2 * d_model:3 * d_model]
    sh2 = mod[:, 3 * d_model:4 * d_model]
    sc2 = mod[:, 4 * d_model:5 * d_model]
    ga2 = mod[:, 5 * d_model:6 * d_model]
    mix = _dot(ac_ref[0], woac_ref[...]) + _dot(bo_ref[0], wob_ref[...])
    x1 = x + ga1 * mix
    h2 = ((_rms(x1) * g2_ref[...]) * (1.0 + sc2) + sh2).astype(BF16)
    d_ff = wg_ref.shape[1]
    acc = jnp.zeros_like(x1)
    for c0 in range(0, d_ff, ff_chunk):
        gate = _dot(h2, wg_ref[:, c0:c0 + ff_chunk])
        up = _dot(h2, wu_ref[:, c0:c0 + ff_chunk])
        act = (gate * jax.nn.sigmoid(gate)) * up
        acc = acc + _dot(act.astype(BF16), wd_ref[c0:c0 + ff_chunk, :])
    x2 = x1 + ga2 * acc
    if final:
        o_ref[0] = _rms(x2) * gf_ref[...]
    else:
        o_ref[0] = x2


def _ffn_call(x, ac, bo, mod, layer, mod_row0, g2, woac, wob, wg, wu, wd, gf, tm, final):
    b, t, d = x.shape
    d_ff = wg.shape[2]
    ff_chunk = d_ff // 2 if (d_ff // 2) % LANES == 0 else d_ff
    kern = functools.partial(_ffn_kernel, d_model=d, ff_chunk=ff_chunk, final=final)
    row3 = lambda bi, j: (bi, j, 0)
    lay3 = lambda bi, j: (layer, 0, 0)
    once = pl.Buffered(1)
    return pl.pallas_call(
        kern,
        out_shape=jax.ShapeDtypeStruct((b, t, d), F32),
        grid=(b, t // tm),
        in_specs=[
            pl.BlockSpec((1, tm, d), row3),
            pl.BlockSpec((1, tm, A_WIDTH + C_WIDTH), row3),
            pl.BlockSpec((1, tm, B_WIDTH), row3),
            pl.BlockSpec((None, 1, 1, N_MOD * d), lambda bi, j: (layer, mod_row0 + bi, 0, 0)),
            pl.BlockSpec((None, 1, d), lay3),
            pl.BlockSpec((None, A_WIDTH + C_WIDTH, d), lay3, pipeline_mode=once),
            pl.BlockSpec((None, B_WIDTH, d), lay3, pipeline_mode=once),
            pl.BlockSpec((None, d, d_ff), lay3, pipeline_mode=once),
            pl.BlockSpec((None, d, d_ff), lay3, pipeline_mode=once),
            pl.BlockSpec((None, d_ff, d), lay3, pipeline_mode=once),
            pl.BlockSpec((1, d), lambda bi, j: (0, 0)),
        ],
        out_specs=pl.BlockSpec((1, tm, d), row3),
        compiler_params=pltpu.CompilerParams(
            dimension_semantics=("arbitrary", "arbitrary"), vmem_limit_bytes=VMEM_LIMIT),
        name="out_proj_ffn",
    )(x, ac, bo, mod, g2, woac, wob, wg, wu, wd, gf)


def _rope_tables(start, t):
    half = HEAD_DIM // 2
    inv = jnp.power(ROPE_THETA, -2.0 * jnp.arange(half, dtype=F32) / HEAD_DIM)
    ang = (start + jnp.arange(t)).astype(F32)[:, None] * inv[None, :]
    cos = jnp.cos(ang)
    sin = jnp.sin(ang)
    zero = jnp.zeros_like(sin)
    reps = LANES // HEAD_DIM
    return (jnp.tile(jnp.concatenate([cos, cos], axis=-1), (1, reps)),
            jnp.tile(jnp.concatenate([-sin, zero], axis=-1), (1, reps)),
            jnp.tile(jnp.concatenate([zero, sin], axis=-1), (1, reps)))


def _pad_w_in(w_in):
    n_head = OFF_KI + IDX_DIM + IDX_HEADS
    depth, d, _ = w_in.shape
    pad = jnp.zeros((depth, d, OFF_C - n_head), w_in.dtype)
    return jnp.concatenate([w_in[:, :, :n_head], pad, w_in[:, :, n_head:]], axis=-1).astype(BF16)


def _trunk(x, mod, mod_row0, cache, params, q_start, tm, tq):
    (g_norm1, g_norm2, w_in_p, w_spatial, b_spatial, w_dw, b_dw, g_cnorm,
     woac, wob, wg, wu, wd, g_final) = params
    b, t, d = x.shape
    depth = w_in_p.shape[0]
    ln = min(t, A_CHUNK)
    tabs = _rope_tables(q_start, t)
    wsp = w_spatial[:, :, :ln, :ln]
    bsp_tab = jnp.repeat(jnp.swapaxes(b_spatial[:, :, :ln], 1, 2), A_HEAD_DIM, axis=-1)
    kv_stack, convs, avs = None, [], []
    for l in range(depth):
        if cache is None:
            cst = jnp.zeros((b, CTX_ROWS, C_WIDTH), F32)
        else:
            cst = jnp.pad(cache[3][l], ((0, 0), (CTX_PAD, 0), (0, 0)))
        q, ks, vs, qi, kis, wi, ac, av, cnew = _in_call(
            x, mod, l, mod_row0, g_norm1, w_in_p, tabs, wsp, bsp_tab, w_dw, b_dw, g_cnorm, cst,
            kv_stack, tm)
        kv_stack = (ks, vs, kis)
        if cache is None:
            bo = _attn_prompt(q, qi, wi, ks, vs, kis, l, tq)
        else:
            past = cache[0].shape[2]
            bo = _attn_call(q, qi, wi, cache[0].reshape(depth, b, past, B_KV_WIDTH),
                            cache[1].reshape(depth, b, past, B_KV_WIDTH), cache[2], ks, vs, kis,
                            l, q_start)
        x = _ffn_call(x, ac, bo, mod, l, mod_row0, g_norm2, woac, wob, wg, wu, wd, g_final,
                      tm, final=(l == depth - 1))
        convs.append(cnew[:, CTX_PAD:, :])
        avs.append(av)
    ks, vs, kis = kv_stack
    return (x, ks.reshape(depth, b, t, B_KV_HEADS, HEAD_DIM), vs.reshape(depth, b, t, B_KV_HEADS, HEAD_DIM),
            kis, jnp.stack(convs), jnp.stack(avs))


def kernel(x_prompt, x_sample, cache_k, cache_v, cache_kidx, state_conv, c_prompt, c_sample,
           w_ada, b_ada, g_norm1, g_norm2, w_in, w_spatial, b_spatial, w_dw, b_dw, g_cnorm,
           w_out, w_gate, w_up, w_down, g_final):
    depth, d = g_norm1.shape
    nb_p, t_p, _ = x_prompt.shape
    nb_s, t_s, _ = x_sample.shape
    past = cache_k.shape[2]

    rows = nb_p + nb_s
    rows_pad = -(-rows // 8) * 8
    c_all = jnp.pad(jnp.concatenate([c_prompt, c_sample], axis=0), ((0, rows_pad - rows), (0, 0)))
    mod = _mod_call(c_all, w_ada, b_ada).reshape(depth, rows_pad, 1, N_MOD * d)

    w_out_b = w_out.astype(BF16)
    params = (
        g_norm1.reshape(depth, 1, d), g_norm2.reshape(depth, 1, d), _pad_w_in(w_in),
        w_spatial, b_spatial, w_dw, b_dw.reshape(depth, 1, C_WIDTH), g_cnorm.reshape(depth, 1, C_WIDTH),
        jnp.concatenate([w_out_b[:, :A_WIDTH], w_out_b[:, A_WIDTH + B_WIDTH:]], axis=1),
        w_out_b[:, A_WIDTH:A_WIDTH + B_WIDTH],
        w_gate.astype(BF16), w_up.astype(BF16), w_down.astype(BF16), g_final.reshape(1, d),
    )

    y_p, p_k, p_v, p_ki, p_conv, _ = _trunk(
        x_prompt, mod, 0, None, params, 0, tm=min(t_p, 512), tq=min(t_p, 256))
    y_s, s_k, s_v, s_ki, s_conv, s_av = _trunk(
        x_sample, mod, nb_p, (cache_k, cache_v, cache_kidx, state_conv), params, past,
        tm=t_s, tq=t_s)
    return (y_p, y_s, p_k, p_v, p_ki, p_conv, s_k, s_v, s_ki, s_conv, s_av)
```

```python
import functools

import jax
import jax.numpy as jnp
from jax import lax
from jax.experimental import pallas as pl
from jax.experimental.pallas import tpu as pltpu

CHUNK = 64
CHUNK_SHIFT = 6
assert 1 << CHUNK_SHIFT == CHUNK
A_HEADS = 4
A_HEAD_DIM = 64
A_WIDTH = A_HEADS * A_HEAD_DIM
A_CHUNK = 128
B_HEADS = 8
B_KV_HEADS = 2
HEAD_DIM = 64
B_WIDTH = B_HEADS * HEAD_DIM
B_KV_WIDTH = B_KV_HEADS * HEAD_DIM
IDX_HEADS = 4
IDX_DIM = 64
TOPK_MAX = 256
ROPE_THETA = 10000.0
C_GROUPS = 4
C_WIDTH = 256
CONV_WIDTH = 31
N_MOD = 6
EPS = 1e-6

LANES = 128
SUBLANES = 8
BF16_SUBLANES = 16
CTX_ROWS = 32
CTX_PAD = CTX_ROWS - (CONV_WIDTH - 1)
VMEM_LIMIT = 56 * 1024 * 1024

OFF_A = 0
OFF_Q = OFF_A + 2 * A_WIDTH
OFF_K = OFF_Q + B_WIDTH
OFF_V = OFF_K + B_KV_WIDTH
OFF_QI = OFF_V + B_KV_WIDTH
OFF_KI = OFF_QI + IDX_HEADS * IDX_DIM
OFF_C = OFF_KI + LANES
N_IN_PAD = OFF_C + 2 * C_WIDTH

LOG2E = 1.4426950408889634
KEY_MIN_FINITE = -2139095040

F32 = jnp.float32
BF16 = jnp.bfloat16


def _dot(a, b):
    return jnp.dot(a, b, preferred_element_type=F32)


def _dot_t(a, b):
    return lax.dot_general(a, b, (((1,), (1,)), ((), ())), preferred_element_type=F32)


def _rms(x):
    return x * lax.rsqrt(jnp.mean(x * x, axis=-1, keepdims=True) + EPS)


def _lane_group(n, group):
    shift = group.bit_length() - 1
    assert 1 << shift == group
    return lax.shift_right_logical(lax.broadcasted_iota(jnp.int32, (1, n), 1), shift)


def _group_standardize(y, group):
    n = y.shape[-1]
    gid = _lane_group(n, group)
    inv = 1.0 / group
    mean = jnp.zeros_like(y)
    for g in range(n // group):
        m = gid == g
        s = jnp.sum(jnp.where(m, y, 0.0), axis=-1, keepdims=True) * inv
        mean = jnp.where(m, s, mean)
    yc = y - mean
    sq = yc * yc
    var = jnp.zeros_like(y)
    for g in range(n // group):
        m = gid == g
        s = jnp.sum(jnp.where(m, sq, 0.0), axis=-1, keepdims=True) * inv
        var = jnp.where(m, s, var)
    return yc * lax.rsqrt(var + EPS)


def _rope(x, cos, sin_lo, sin_hi):
    parts = []
    for c in range(x.shape[-1] // LANES):
        xc = x[:, c * LANES:(c + 1) * LANES]
        up = pltpu.roll(xc, LANES - HEAD_DIM // 2, 1)
        down = pltpu.roll(xc, HEAD_DIM // 2, 1)
        parts.append(xc * cos + up * sin_lo + down * sin_hi)
    return parts[0] if len(parts) == 1 else jnp.concatenate(parts, axis=-1)


def _mod_kernel(c_ref, w_ref, b_ref, o_ref):
    c = c_ref[...]
    cond = c * jax.nn.sigmoid(c)
    o_ref[0] = _dot(cond.astype(BF16), w_ref[0].astype(BF16)) + b_ref[0]


def _mod_call(c_all, w_ada, b_ada):
    depth, d, n = w_ada.shape
    rows = c_all.shape[0]
    tn = d
    return pl.pallas_call(
        _mod_kernel,
        out_shape=jax.ShapeDtypeStruct((depth, rows, n), F32),
        grid=(depth, n // tn),
        in_specs=[
            pl.BlockSpec((rows, d), lambda l, j: (0, 0)),
            pl.BlockSpec((1, d, tn), lambda l, j: (l, 0, j)),
            pl.BlockSpec((1, 1, tn), lambda l, j: (l, 0, j)),
        ],
        out_specs=pl.BlockSpec((1, rows, tn), lambda l, j: (l, 0, j)),
        compiler_params=pltpu.CompilerParams(
            dimension_semantics=("arbitrary", "arbitrary"), vmem_limit_bytes=VMEM_LIMIT),
        name="adaln_mod",
    )(c_all, w_ada, b_ada.reshape(depth, 1, n))


N_IN_OPERANDS = 13


def _in_kernel(*refs, tm, ln, d_model, n_aliased):
    (x_ref, mod_ref, g1_ref, w_ref, cos_ref, slo_ref, shi_ref, wsp_ref, bsp_ref,
     wdw_ref, bdw_ref, gcn_ref, cst_ref) = refs[:N_IN_OPERANDS]
    (q_ref, k_ref, v_ref, qi_ref, ki_ref, wi_ref, ac_ref, av_ref, cnew_ref,
     hp_ref, sh_ref) = refs[N_IN_OPERANDS + n_aliased:]
    j = pl.program_id(1)
    x = x_ref[0]
    mod = mod_ref[0]
    sh1 = mod[:, 0:d_model]
    sc1 = mod[:, d_model:2 * d_model]
    h = (_rms(x) * g1_ref[...]) * (1.0 + sc1) + sh1
    z = _dot(h.astype(BF16), w_ref[...])

    cos = cos_ref[...]
    slo = slo_ref[...]
    shi = shi_ref[...]

    q = _rope(z[:, OFF_Q:OFF_Q + B_WIDTH], cos, slo, shi)
    q_ref[0] = (q * (LOG2E * HEAD_DIM ** -0.5)).astype(BF16)
    k_ref[0] = _rope(z[:, OFF_K:OFF_K + B_KV_WIDTH], cos, slo, shi)
    v_ref[0] = z[:, OFF_V:OFF_V + B_KV_WIDTH]
    qi_ref[0] = _rope(z[:, OFF_QI:OFF_QI + IDX_HEADS * IDX_DIM], cos, slo, shi).astype(BF16)
    kiwi = z[:, OFF_KI:OFF_KI + LANES]
    ki_ref[0] = _rope(kiwi, cos, slo, shi)[:, 0:IDX_DIM]
    wi_ref[0] = kiwi

    za = z[:, OFF_A:OFF_A + 2 * A_WIDTH]
    za = 0.5 * za * (1.0 + lax.erf(za * (2.0 ** -0.5)))
    u = za[:, 0:A_WIDTH]
    vn = _group_standardize(za[:, A_WIDTH:2 * A_WIDTH], A_HEAD_DIM)
    av_ref[0] = vn
    vb = vn.astype(BF16)
    row = lax.broadcasted_iota(jnp.int32, (ln, ln), 0)
    col = lax.broadcasted_iota(jnp.int32, (ln, ln), 1)
    head_of_lane = _lane_group(A_WIDTH, A_HEAD_DIM)
    wsp = [jnp.where(col <= row, wsp_ref[g], 0.0).astype(BF16) for g in range(A_HEADS)]
    gated = []
    for c in range(tm // ln):
        vc = vb[c * ln:(c + 1) * ln, :]
        s = jnp.zeros((ln, A_WIDTH), F32)
        for g in range(A_HEADS):
            s = jnp.where(head_of_lane == g, _dot(wsp[g], vc), s)
        gated.append(u[c * ln:(c + 1) * ln, :] * (s + bsp_ref[...]))
    a_out = gated[0] if len(gated) == 1 else jnp.concatenate(gated, axis=0)
    ac_ref[0, :, 0:A_WIDTH] = a_out.astype(BF16)

    zc = z[:, OFF_C:OFF_C + 2 * C_WIDTH]
    hc = zc[:, 0:C_WIDTH] * jax.nn.sigmoid(zc[:, C_WIDTH:2 * C_WIDTH])

    @pl.when(j == 0)
    def _():
        hp_ref[0:CTX_ROWS, :] = cst_ref[0]

    hp_ref[CTX_ROWS:CTX_ROWS + tm, :] = hc
    span = tm + CTX_ROWS - SUBLANES
    for s in range(1, SUBLANES):
        sh_ref[s - 1, 0:span, :] = hp_ref[s:s + span, :]
    rb = min(tm, 64)
    wdw = wdw_ref[...]
    conv = []
    for r0 in range(0, tm, rb):
        acc = jnp.zeros((rb, C_WIDTH), F32)
        for t in range(CONV_WIDTH):
            phase = (CTX_PAD + t) % SUBLANES
            base = CTX_PAD + t - phase + r0
            rows = hp_ref[base:base + rb, :] if phase == 0 else sh_ref[phase - 1, base:base + rb, :]
            acc = acc + rows * wdw[t:t + 1, :]
        conv.append(acc)
    y = (conv[0] if len(conv) == 1 else jnp.concatenate(conv, axis=0)) + bdw_ref[...]
    y = _group_standardize(y, C_WIDTH // C_GROUPS) * gcn_ref[...]
    ac_ref[0, :, A_WIDTH:A_WIDTH + C_WIDTH] = (y * jax.nn.sigmoid(y)).astype(BF16)

    tail = hp_ref[tm:tm + CTX_ROWS, :]
    cnew_ref[0] = tail
    hp_ref[0:CTX_ROWS, :] = tail


def _in_call(x, mod, layer, mod_row0, g1, w_in_p, tabs, wsp, bsp_tab, wdw, bdw, gcn, cst, kv_stack, tm):
    b, t, d = x.shape
    depth = w_in_p.shape[0]
    ln = min(t, A_CHUNK)
    cos, slo, shi = tabs
    n_aliased = 0 if kv_stack is None else len(kv_stack)
    kern = functools.partial(_in_kernel, tm=tm, ln=ln, d_model=d, n_aliased=n_aliased)
    row3 = lambda bi, j: (bi, j, 0)
    lay_row4 = lambda bi, j: (layer, bi, j, 0)
    lay3 = lambda bi, j: (layer, 0, 0)
    out_shape = (
        jax.ShapeDtypeStruct((b, t, B_WIDTH), BF16),
        jax.ShapeDtypeStruct((depth, b, t, B_KV_WIDTH), F32),
        jax.ShapeDtypeStruct((depth, b, t, B_KV_WIDTH), F32),
        jax.ShapeDtypeStruct((b, t, IDX_HEADS * IDX_DIM), BF16),
        jax.ShapeDtypeStruct((depth, b, t, IDX_DIM), F32),
        jax.ShapeDtypeStruct((b, t, LANES), F32),
        jax.ShapeDtypeStruct((b, t, A_WIDTH + C_WIDTH), BF16),
        jax.ShapeDtypeStruct((b, t, A_WIDTH), F32),
        jax.ShapeDtypeStruct((b, CTX_ROWS, C_WIDTH), F32),
    )
    out_specs = (
        pl.BlockSpec((1, tm, B_WIDTH), row3),
        pl.BlockSpec((None, 1, tm, B_KV_WIDTH), lay_row4),
        pl.BlockSpec((None, 1, tm, B_KV_WIDTH), lay_row4),
        pl.BlockSpec((1, tm, IDX_HEADS * IDX_DIM), row3),
        pl.BlockSpec((None, 1, tm, IDX_DIM), lay_row4),
        pl.BlockSpec((1, tm, LANES), row3),
        pl.BlockSpec((1, tm, A_WIDTH + C_WIDTH), row3),
        pl.BlockSpec((1, tm, A_WIDTH), row3),
        pl.BlockSpec((1, CTX_ROWS, C_WIDTH), lambda bi, j: (bi, 0, 0)),
    )
    kv_outputs = (1, 2, 4)
    in_specs = [
        pl.BlockSpec((1, tm, d), row3),
        pl.BlockSpec((None, 1, 1, N_MOD * d), lambda bi, j: (layer, mod_row0 + bi, 0, 0)),
        pl.BlockSpec((None, 1, d), lay3),
        pl.BlockSpec((None, d, N_IN_PAD), lay3),
        pl.BlockSpec((tm, LANES), lambda bi, j: (j, 0)),
        pl.BlockSpec((tm, LANES), lambda bi, j: (j, 0)),
        pl.BlockSpec((tm, LANES), lambda bi, j: (j, 0)),
        pl.BlockSpec((None, A_HEADS, ln, ln), lambda bi, j: (layer, 0, 0, 0)),
        pl.BlockSpec((None, ln, A_WIDTH), lay3),
        pl.BlockSpec((None, CONV_WIDTH, C_WIDTH), lay3),
        pl.BlockSpec((None, 1, C_WIDTH), lay3),
        pl.BlockSpec((None, 1, C_WIDTH), lay3),
        pl.BlockSpec((1, CTX_ROWS, C_WIDTH), lambda bi, j: (bi, 0, 0)),
    ]
    args = [x, mod, g1, w_in_p, cos, slo, shi, wsp, bsp_tab, wdw, bdw, gcn, cst]
    assert len(args) == N_IN_OPERANDS
    aliases = {}
    if kv_stack is not None:
        for arr, out_pos in zip(kv_stack, kv_outputs):
            aliases[len(args)] = out_pos
            in_specs.append(pl.BlockSpec(memory_space=pl.ANY))
            args.append(arr)
    return pl.pallas_call(
        kern,
        out_shape=out_shape,
        grid=(b, t // tm),
        in_specs=in_specs,
        out_specs=out_specs,
        scratch_shapes=[pltpu.VMEM((tm + CTX_ROWS, C_WIDTH), F32),
                        pltpu.VMEM((SUBLANES - 1, tm + CTX_ROWS, C_WIDTH), F32)],
        input_output_aliases=aliases,
        compiler_params=pltpu.CompilerParams(
            dimension_semantics=("arbitrary", "arbitrary"), vmem_limit_bytes=VMEM_LIMIT),
        name="in_proj_mixers_ac",
    )(*args)


def _attn_kernel(q_ref, qi_ref, wi_ref, k_ref, v_ref, ki_ref, o_ref, key_ref, bias_ref,
                 *, tq, n_keys, q_start, topk):
    lk = k_ref.shape[1]
    j = pl.program_id(1)
    qchunk = lax.shift_right_logical(
        q_start + j * tq + lax.broadcasted_iota(jnp.int32, (tq, 1), 0), CHUNK_SHIFT)

    def allowed_at(k0, width):
        kpos = k0 + lax.broadcasted_iota(jnp.int32, (1, width), 1)
        return (lax.shift_right_logical(kpos, CHUNK_SHIFT) <= qchunk) & (kpos < n_keys)

    kib = ki_ref[0].astype(BF16)
    qi = qi_ref[0]
    wi = wi_ref[0]
    score = jnp.zeros((tq, lk), F32)
    for h in range(IDX_HEADS):
        d = _dot_t(qi[:, h * IDX_DIM:(h + 1) * IDX_DIM], kib)
        score = score + wi[:, IDX_DIM + h:IDX_DIM + h + 1] * jnp.maximum(d, 0.0)
    score = jnp.where(score == 0.0, 0.0, score)
    score = jnp.where(allowed_at(0, lk), score, -jnp.inf)

    bits = pltpu.bitcast(score, jnp.int32)
    key_ref[...] = bits ^ ((bits >> 31) & jnp.int32(0x7FFFFFFF))
    int_min = jnp.int32(-2 ** 31)

    def digit_step(i, thr):
        unit = lax.shift_left(jnp.int32(1), jnp.int32(30) - 2 * i)
        key = key_ref[...]
        for mult in (1, 2, 3):
            cand = thr + mult * unit if mult == 1 else cand + unit
            cnt = jnp.sum((key >= cand).astype(F32), axis=-1, keepdims=True)
            best = jnp.where(cnt >= topk, cand, thr if mult == 1 else best)
        return best

    thr = lax.fori_loop(0, 16, digit_step, jnp.full((tq, 1), int_min, jnp.int32))

    need = topk - jnp.sum((key_ref[...] > thr).astype(F32), axis=-1, keepdims=True)
    tri = (lax.broadcasted_iota(jnp.int32, (LANES, LANES), 0)
           <= lax.broadcasted_iota(jnp.int32, (LANES, LANES), 1)).astype(F32).astype(BF16)
    seen = jnp.zeros((tq, 1), F32)
    for c in range(lk // LANES):
        sl = slice(c * LANES, (c + 1) * LANES)
        key_c = key_ref[:, sl]
        eq_c = key_c == thr
        eq_f = eq_c.astype(F32)
        rank = _dot(eq_f.astype(BF16), tri) + seen
        chosen = ((key_c > thr) | (eq_c & (rank <= need))) & allowed_at(c * LANES, LANES)
        bias_ref[:, sl] = jnp.where(chosen, 0.0, -jnp.inf)
        seen = seen + jnp.sum(eq_f, axis=-1, keepdims=True)

    q = q_ref[0]
    kb = k_ref[0].astype(BF16)
    vb = v_ref[0].astype(BF16)
    group = B_HEADS // B_KV_HEADS
    bias = jnp.concatenate([bias_ref[...]] * group, axis=0)
    outs = []
    for n in range(B_KV_HEADS):
        kn = kb[:, n * HEAD_DIM:(n + 1) * HEAD_DIM]
        vn = vb[:, n * HEAD_DIM:(n + 1) * HEAD_DIM]
        qg = jnp.concatenate([q[:, (n * group + g) * HEAD_DIM:(n * group + g + 1) * HEAD_DIM]
                              for g in range(group)], axis=0)
        logits = _dot_t(qg, kn) + bias
        p = jnp.exp2(logits - jnp.max(logits, axis=-1, keepdims=True))
        den = jnp.sum(p, axis=-1, keepdims=True)
        og = _dot(p.astype(BF16), vn) / den
        outs.extend(og[g * tq:(g + 1) * tq] for g in range(group))
    o_ref[0] = jnp.concatenate(outs, axis=-1).astype(BF16)


def _attn_call(q, qi, wi, k_all, v_all, ki_all, n_keys, q_start, tq):
    b, t, _ = q.shape
    lk = k_all.shape[1]
    topk = min(TOPK_MAX, n_keys // 4)
    kern = functools.partial(_attn_kernel, tq=tq, n_keys=n_keys, q_start=q_start, topk=topk)
    row3 = lambda bi, j: (bi, j, 0)
    bat3 = lambda bi, j: (bi, 0, 0)
    return pl.pallas_call(
        kern,
        out_shape=jax.ShapeDtypeStruct((b, t, B_WIDTH), BF16),
        grid=(b, t // tq),
        in_specs=[
            pl.BlockSpec((1, tq, B_WIDTH), row3),
            pl.BlockSpec((1, tq, IDX_HEADS * IDX_DIM), row3),
            pl.BlockSpec((1, tq, LANES), row3),
            pl.BlockSpec((1, lk, B_KV_WIDTH), bat3),
            pl.BlockSpec((1, lk, B_KV_WIDTH), bat3),
            pl.BlockSpec((1, lk, IDX_DIM), bat3),
        ],
        out_specs=pl.BlockSpec((1, tq, B_WIDTH), row3),
        scratch_shapes=[pltpu.VMEM((tq, lk), jnp.int32), pltpu.VMEM((tq, lk), F32)],
        compiler_params=pltpu.CompilerParams(
            dimension_semantics=("arbitrary", "arbitrary"), vmem_limit_bytes=VMEM_LIMIT),
        name="dsa_attention",
    )(q, qi, wi, k_all, v_all, ki_all)


FOLD_ROWS = 64


def _col_reduce(x, op):
    r = x.shape[0]
    if r > FOLD_ROWS and r % FOLD_ROWS == 0:
        acc = x[:FOLD_ROWS]
        for c in range(1, r // FOLD_ROWS):
            acc = op(acc, x[c * FOLD_ROWS:(c + 1) * FOLD_ROWS])
        x, r = acc, FOLD_ROWS
    while r % 16 == 0:
        x = op(x[:r // 2], x[r // 2:])
        r //= 2
    return (jnp.sum if op is jnp.add else jnp.max)(x, axis=0, keepdims=True)


WORD_BITS = 32
INT_MIN = -2 ** 31
QK_ROWS = 256


def _bit_transpose(words):
    a = list(words)
    j, m = WORD_BITS // 2, 0x0000FFFF
    while j:
        k = 0
        while k < WORD_BITS:
            t = (a[k] ^ lax.shift_right_logical(a[k + j], j)) & m
            a[k] = a[k] ^ t
            a[k + j] = a[k + j] ^ lax.shift_left(t, j)
            k = (k + j + 1) & ~j
        j >>= 1
        m = (m ^ (m << j)) & 0x7FFFFFFF if j else m
    return a


def _select_threshold(key_ref, planes_ref, topk, tq, side_work=()):
    lk = key_ref.shape[0]
    slab = lk // WORD_BITS
    for r0 in range(0, slab, SUBLANES):
        words = [key_ref[i * slab + r0:i * slab + r0 + SUBLANES, :] for i in range(WORD_BITS)]
        for b, plane in enumerate(_bit_transpose(words)):
            planes_ref[b, r0:r0 + SUBLANES, :] = plane
    count = lambda words: _col_reduce(lax.population_count(words), jnp.add)
    alive = jnp.full((slab, tq), -1, jnp.int32)
    left = jnp.full((1, tq), topk, jnp.int32)
    thr = jnp.zeros((1, tq), jnp.int32)
    steps = WORD_BITS // 2
    for step, hi_bit in enumerate(range(WORD_BITS - 1, 0, -2)):
        for piece in side_work[step * len(side_work) // steps:(step + 1) * len(side_work) // steps]:
            piece()
        p_hi = planes_ref[WORD_BITS - 1 - hi_bit]
        p_lo = planes_ref[WORD_BITS - hi_bit]
        if hi_bit == WORD_BITS - 1:
            p_hi = ~p_hi
        u1 = alive & p_hi
        u0 = alive ^ u1
        c11 = u1 & p_lo
        c10 = u1 ^ c11
        c01 = u0 & p_lo
        c00 = u0 ^ c01
        n11 = count(c11)
        n1x = n11 + count(c10)
        n_1 = n1x + count(c01)
        t11 = n11 >= left
        t1x = n1x >= left
        t_1 = n_1 >= left
        alive = jnp.where(t11, c11, jnp.where(t1x, c10, jnp.where(t_1, c01, c00)))
        left = left - jnp.where(t11, 0, jnp.where(t1x, n11, jnp.where(t_1, n1x, n_1)))
        hi_val = jnp.int32(1 << hi_bit if hi_bit < WORD_BITS - 1 else INT_MIN)
        lo_val = jnp.int32(1 << (hi_bit - 1))
        thr = thr | jnp.where(t1x, hi_val, 0) | jnp.where(t11 | (t_1 & ~t1x), lo_val, 0)
    return thr ^ INT_MIN, left


def _attn_t_kernel(*refs, tq, n_keys, q_start, q_block0, topk, aliased):
    q_ref, qi_ref, wi_ref, k_ref, v_ref, ki_ref = refs[:6]
    o_ref, key_ref, bias_ref, planes_ref, logit_ref = refs[7:] if aliased else refs[6:]
    lk = k_ref.shape[1]
    j = pl.program_id(1)
    qchunk = lax.shift_right_logical(
        q_start + (q_block0 + j) * tq + lax.broadcasted_iota(jnp.int32, (1, tq), 1), CHUNK_SHIFT)

    def allowed_at(k0, rows):
        kpos = k0 + lax.broadcasted_iota(jnp.int32, (rows, 1), 0)
        return (lax.shift_right_logical(kpos, CHUNK_SHIFT) <= qchunk) & (kpos < n_keys)

    kib = ki_ref[0].astype(BF16)
    qi = qi_ref[0]
    wi_t = wi_ref[0].T
    score = jnp.zeros((lk, tq), F32)
    for h in range(IDX_HEADS):
        d = _dot_t(kib, qi[:, h * IDX_DIM:(h + 1) * IDX_DIM])
        score = score + wi_t[IDX_DIM + h:IDX_DIM + h + 1, :] * jnp.maximum(d, 0.0)
    score = jnp.where(score == 0.0, 0.0, score)
    free = min(((q_start + q_block0 * tq) // CHUNK + 1) * CHUNK, n_keys, lk)
    if free < lk:
        masked = jnp.where(allowed_at(free, lk - free), score[free:], -jnp.inf)
        score = jnp.concatenate([score[:free], masked], axis=0) if free else masked

    bits = pltpu.bitcast(score, jnp.int32)
    key_ref[...] = bits ^ ((bits >> 31) & jnp.int32(0x7FFFFFFF))

    q = q_ref[0]
    group = B_HEADS // B_KV_HEADS
    qgs = [jnp.concatenate([q[:, (n * group + g) * HEAD_DIM:(n * group + g + 1) * HEAD_DIM]
                            for g in range(group)], axis=0) for n in range(B_KV_HEADS)]

    def logits_block(r0, n):
        kn = k_ref[0, r0:r0 + QK_ROWS, n * HEAD_DIM:(n + 1) * HEAD_DIM].astype(BF16)
        logit_ref[n, r0:r0 + QK_ROWS, :] = _dot_t(kn, qgs[n])

    side_work = [functools.partial(logits_block, r0, n)
                 for r0 in range(0, lk, QK_ROWS) for n in range(B_KV_HEADS)]
    thr, left = _select_threshold(key_ref, planes_ref, topk, tq, side_work)

    clamped = thr < KEY_MIN_FINITE
    thr = jnp.where(clamped, KEY_MIN_FINITE, thr)
    need = jnp.where(clamped, topk, left).astype(F32)
    tril = (lax.broadcasted_iota(jnp.int32, (LANES, LANES), 0)
            >= lax.broadcasted_iota(jnp.int32, (LANES, LANES), 1)).astype(F32).astype(BF16)
    seen = jnp.zeros((1, tq), F32)
    for c in range(lk // LANES):
        sl = slice(c * LANES, (c + 1) * LANES)
        key_c = key_ref[sl, :]
        eq_c = key_c == thr
        eq_f = eq_c.astype(F32)
        rank = _dot(tril, eq_f.astype(BF16)) + seen
        chosen = (key_c > thr) | (eq_c & (rank <= need))
        bias_ref[sl, :] = jnp.where(chosen, 0.0, -jnp.inf)
        seen = seen + _col_reduce(eq_f, jnp.add)

    vt = v_ref[0].T.astype(BF16)
    bias = bias_ref[...]
    ones_rows = jnp.ones((BF16_SUBLANES, lk), BF16)
    outs = []
    for n in range(B_KV_HEADS):
        vtn = jnp.concatenate([vt[n * HEAD_DIM:(n + 1) * HEAD_DIM, :], ones_rows], axis=0)
        ps = []
        for g in range(group):
            lg = logit_ref[n, :, g * tq:(g + 1) * tq] + bias
            ps.append(jnp.exp2(lg - _col_reduce(lg, jnp.maximum)).astype(BF16))
        og = _dot(vtn, jnp.concatenate(ps, axis=1))
        og = og[:HEAD_DIM] / og[HEAD_DIM:HEAD_DIM + 1]
        outs.extend(og[:, g * tq:(g + 1) * tq] for g in range(group))
    o_ref[0] = jnp.concatenate(outs, axis=0).T.astype(BF16)


def _attn_t_call(q, qi, wi, k, v, ki, prev, layer, n_keys, q_start, q_block0, n_q_blocks, lk, tq):
    b, t, _ = q.shape
    topk = min(TOPK_MAX, n_keys // 4)
    aliased = prev is not None
    kern = functools.partial(_attn_t_kernel, tq=tq, n_keys=n_keys, q_start=q_start,
                             q_block0=q_block0, topk=topk, aliased=aliased)
    row3 = lambda bi, j: (bi, q_block0 + j, 0)
    lay_bat4 = lambda bi, j: (layer, bi, 0, 0)
    in_specs = [
        pl.BlockSpec((1, tq, B_WIDTH), row3),
        pl.BlockSpec((1, tq, IDX_HEADS * IDX_DIM), row3),
        pl.BlockSpec((1, tq, LANES), row3),
        pl.BlockSpec((None, 1, lk, B_KV_WIDTH), lay_bat4),
        pl.BlockSpec((None, 1, lk, B_KV_WIDTH), lay_bat4),
        pl.BlockSpec((None, 1, lk, IDX_DIM), lay_bat4),
    ]
    args = [q, qi, wi, k, v, ki]
    if aliased:
        in_specs.append(pl.BlockSpec(memory_space=pl.ANY))
        args.append(prev)
    return pl.pallas_call(
        kern,
        out_shape=jax.ShapeDtypeStruct((b, t, B_WIDTH), BF16),
        grid=(b, n_q_blocks),
        in_specs=in_specs,
        out_specs=pl.BlockSpec((1, tq, B_WIDTH), row3),
        scratch_shapes=[pltpu.VMEM((lk, tq), jnp.int32), pltpu.VMEM((lk, tq), F32),
                        pltpu.VMEM((WORD_BITS, lk // WORD_BITS, tq), jnp.int32),
                        pltpu.VMEM((B_KV_HEADS, lk, B_HEADS // B_KV_HEADS * tq), F32)],
        input_output_aliases={6: 0} if aliased else {},
        compiler_params=pltpu.CompilerParams(
            dimension_semantics=("arbitrary", "arbitrary"), vmem_limit_bytes=VMEM_LIMIT),
        name="dsa_attention_t",
    )(*args)


def _attn_prompt(q, qi, wi, k, v, ki, layer, tq):
    t = q.shape[1]
    nqb = t // tq
    per_seg = max(1, WORD_BITS * SUBLANES // tq)
    assert nqb % per_seg == 0 and (per_seg * tq) % (WORD_BITS * SUBLANES) == 0
    bo = None
    for s in range(nqb // per_seg):
        bo = _attn_t_call(q, qi, wi, k, v, ki, bo, layer, t, 0, s * per_seg, per_seg,
                          (s + 1) * per_seg * tq, tq)
    return bo


def _ffn_kernel(x_ref, ac_ref, bo_ref, mod_ref, g2_ref, wo_ref, wg_ref, wu_ref, wd_ref,
                gf_ref, o_ref, *, d_model, ff_chunk, final):
    x = x_ref[0]
    mod = mod_ref[0]
    ga1 = mod[:, 2 * d_model:3 * d_model]
    sh2 = mod[:, 3 * d_model:4 * d_model]
    sc2 = mod[:, 4 * d_model:5 * d_model]
    ga2 = mod[:, 5 * d_model:6 * d_model]
    ac = ac_ref[0]
    mix = (_dot(ac[:, :A_WIDTH], wo_ref[0:A_WIDTH, :])
           + _dot(bo_ref[0], wo_ref[A_WIDTH:A_WIDTH + B_WIDTH, :])
           + _dot(ac[:, A_WIDTH:], wo_ref[A_WIDTH + B_WIDTH:, :]))
    x1 = x + ga1 * mix
    h2 = ((_rms(x1) * g2_ref[...]) * (1.0 + sc2) + sh2).astype(BF16)
    d_ff = wg_ref.shape[1]
    acc = jnp.zeros_like(x1)
    for c0 in range(0, d_ff, ff_chunk):
        gate = _dot(h2, wg_ref[:, c0:c0 + ff_chunk])
        up = _dot(h2, wu_ref[:, c0:c0 + ff_chunk])
        act = (gate * jax.nn.sigmoid(gate)) * up
        acc = acc + _dot(act.astype(BF16), wd_ref[c0:c0 + ff_chunk, :])
    x2 = x1 + ga2 * acc
    if final:
        o_ref[0] = _rms(x2) * gf_ref[...]
    else:
        o_ref[0] = x2


def _ffn_call(x, ac, bo, mod, layer, mod_row0, g2, wo, wg, wu, wd, gf, tm, final):
    b, t, d = x.shape
    d_ff = wg.shape[2]
    ff_chunk = d_ff // 2 if (d_ff // 2) % LANES == 0 else d_ff
    kern = functools.partial(_ffn_kernel, d_model=d, ff_chunk=ff_chunk, final=final)
    row3 = lambda bi, j: (bi, j, 0)
    lay3 = lambda bi, j: (layer, 0, 0)
    once = pl.Buffered(1)
    return pl.pallas_call(
        kern,
        out_shape=jax.ShapeDtypeStruct((b, t, d), F32),
        grid=(b, t // tm),
        in_specs=[
            pl.BlockSpec((1, tm, d), row3),
            pl.BlockSpec((1, tm, A_WIDTH + C_WIDTH), row3),
            pl.BlockSpec((1, tm, B_WIDTH), row3),
            pl.BlockSpec((None, 1, 1, N_MOD * d), lambda bi, j: (layer, mod_row0 + bi, 0, 0)),
            pl.BlockSpec((None, 1, d), lay3),
            pl.BlockSpec((None, A_WIDTH + B_WIDTH + C_WIDTH, d), lay3, pipeline_mode=once),
            pl.BlockSpec((None, d, d_ff), lay3, pipeline_mode=once),
            pl.BlockSpec((None, d, d_ff), lay3, pipeline_mode=once),
            pl.BlockSpec((None, d_ff, d), lay3, pipeline_mode=once),
            pl.BlockSpec((1, d), lambda bi, j: (0, 0)),
        ],
        out_specs=pl.BlockSpec((1, tm, d), row3),
        compiler_params=pltpu.CompilerParams(
            dimension_semantics=("arbitrary", "arbitrary"), vmem_limit_bytes=VMEM_LIMIT),
        name="out_proj_ffn",
    )(x, ac, bo, mod, g2, wo, wg, wu, wd, gf)


def _rope_tables(start, t):
    half = HEAD_DIM // 2
    inv = jnp.power(ROPE_THETA, -2.0 * jnp.arange(half, dtype=F32) / HEAD_DIM)
    ang = (start + jnp.arange(t)).astype(F32)[:, None] * inv[None, :]
    cos = jnp.cos(ang)
    sin = jnp.sin(ang)
    zero = jnp.zeros_like(sin)
    reps = LANES // HEAD_DIM
    return (jnp.tile(jnp.concatenate([cos, cos], axis=-1), (1, reps)),
            jnp.tile(jnp.concatenate([-sin, zero], axis=-1), (1, reps)),
            jnp.tile(jnp.concatenate([zero, sin], axis=-1), (1, reps)))


def _pad_w_in(w_in):
    n_head = OFF_KI + IDX_DIM + IDX_HEADS
    depth, d, _ = w_in.shape
    pad = jnp.zeros((depth, d, OFF_C - n_head), w_in.dtype)
    return jnp.concatenate([w_in[:, :, :n_head], pad, w_in[:, :, n_head:]], axis=-1).astype(BF16)


def _trunk(x, mod, mod_row0, cache, params, q_start, tm, tq):
    (g_norm1, g_norm2, w_in_p, w_spatial, b_spatial, w_dw, b_dw, g_cnorm,
     wo, wg, wu, wd, g_final) = params
    b, t, d = x.shape
    depth = w_in_p.shape[0]
    ln = min(t, A_CHUNK)
    tabs = _rope_tables(q_start, t)
    wsp = w_spatial[:, :, :ln, :ln]
    bsp_tab = jnp.repeat(jnp.swapaxes(b_spatial[:, :, :ln], 1, 2), A_HEAD_DIM, axis=-1)
    kv_stack, convs, avs = None, [], []
    for l in range(depth):
        if cache is None:
            cst = jnp.zeros((b, CTX_ROWS, C_WIDTH), F32)
        else:
            cst = jnp.pad(cache[3][l], ((0, 0), (CTX_PAD, 0), (0, 0)))
        q, ks, vs, qi, kis, wi, ac, av, cnew = _in_call(
            x, mod, l, mod_row0, g_norm1, w_in_p, tabs, wsp, bsp_tab, w_dw, b_dw, g_cnorm, cst,
            kv_stack, tm)
        kv_stack = (ks, vs, kis)
        if cache is None:
            bo = _attn_prompt(q, qi, wi, ks, vs, kis, l, tq)
        else:
            past = cache[0].shape[2]
            n_keys = past + t
            lk = -(-n_keys // LANES) * LANES
            grow = lambda old, new: jnp.pad(jnp.concatenate([old, new], axis=1),
                                            ((0, 0), (0, lk - n_keys), (0, 0)))
            k_all = grow(cache[0][l].reshape(b, past, B_KV_WIDTH), ks[l])
            v_all = grow(cache[1][l].reshape(b, past, B_KV_WIDTH), vs[l])
            ki_all = grow(cache[2][l], kis[l])
            bo = _attn_call(q, qi, wi, k_all, v_all, ki_all, n_keys, q_start, tq)
        x = _ffn_call(x, ac, bo, mod, l, mod_row0, g_norm2, wo, wg, wu, wd, g_final,
                      tm, final=(l == depth - 1))
        convs.append(cnew[:, CTX_PAD:, :])
        avs.append(av)
    ks, vs, kis = kv_stack
    return (x, ks.reshape(depth, b, t, B_KV_HEADS, HEAD_DIM), vs.reshape(depth, b, t, B_KV_HEADS, HEAD_DIM),
            kis, jnp.stack(convs), jnp.stack(avs))


def kernel(x_prompt, x_sample, cache_k, cache_v, cache_kidx, state_conv, c_prompt, c_sample,
           w_ada, b_ada, g_norm1, g_norm2, w_in, w_spatial, b_spatial, w_dw, b_dw, g_cnorm,
           w_out, w_gate, w_up, w_down, g_final):
    depth, d = g_norm1.shape
    nb_p, t_p, _ = x_prompt.shape
    nb_s, t_s, _ = x_sample.shape
    past = cache_k.shape[2]

    rows = nb_p + nb_s
    rows_pad = -(-rows // 8) * 8
    c_all = jnp.pad(jnp.concatenate([c_prompt, c_sample], axis=0), ((0, rows_pad - rows), (0, 0)))
    mod = _mod_call(c_all, w_ada, b_ada).reshape(depth, rows_pad, 1, N_MOD * d)

    params = (
        g_norm1.reshape(depth, 1, d), g_norm2.reshape(depth, 1, d), _pad_w_in(w_in),
        w_spatial, b_spatial, w_dw, b_dw.reshape(depth, 1, C_WIDTH), g_cnorm.reshape(depth, 1, C_WIDTH),
        w_out.astype(BF16), w_gate.astype(BF16), w_up.astype(BF16), w_down.astype(BF16),
        g_final.reshape(1, d),
    )

    y_p, p_k, p_v, p_ki, p_conv, _ = _trunk(
        x_prompt, mod, 0, None, params, 0, tm=min(t_p, 512), tq=min(t_p, 256))
    y_s, s_k, s_v, s_ki, s_conv, s_av = _trunk(
        x_sample, mod, nb_p, (cache_k, cache_v, cache_kidx, state_conv), params, past,
        tm=t_s, tq=t_s)
    return (y_p, y_s, p_k, p_v, p_ki, p_conv, s_k, s_v, s_ki, s_conv, s_av)
```

```python
import functools

import jax
import jax.numpy as jnp
from jax import lax
from jax.experimental import pallas as pl
from jax.experimental.pallas import tpu as pltpu

CHUNK = 64
CHUNK_SHIFT = 6
assert 1 << CHUNK_SHIFT == CHUNK
A_HEADS = 4
A_HEAD_DIM = 64
A_WIDTH = A_HEADS * A_HEAD_DIM
A_CHUNK = 128
B_HEADS = 8
B_KV_HEADS = 2
HEAD_DIM = 64
B_WIDTH = B_HEADS * HEAD_DIM
B_KV_WIDTH = B_KV_HEADS * HEAD_DIM
IDX_HEADS = 4
IDX_DIM = 64
TOPK_MAX = 256
ROPE_THETA = 10000.0
C_GROUPS = 4
C_WIDTH = 256
CONV_WIDTH = 31
N_MOD = 6
EPS = 1e-6

LANES = 128
SUBLANES = 8
BF16_SUBLANES = 16
CTX_ROWS = 32
CTX_PAD = CTX_ROWS - (CONV_WIDTH - 1)
VMEM_LIMIT = 56 * 1024 * 1024

OFF_A = 0
OFF_Q = OFF_A + 2 * A_WIDTH
OFF_K = OFF_Q + B_WIDTH
OFF_V = OFF_K + B_KV_WIDTH
OFF_QI = OFF_V + B_KV_WIDTH
OFF_KI = OFF_QI + IDX_HEADS * IDX_DIM
OFF_C = OFF_KI + LANES
N_IN_PAD = OFF_C + 2 * C_WIDTH

LOG2E = 1.4426950408889634
KEY_MIN_FINITE = -2139095040

F32 = jnp.float32
BF16 = jnp.bfloat16


def _dot(a, b):
    return jnp.dot(a, b, preferred_element_type=F32)


def _dot_t(a, b):
    return lax.dot_general(a, b, (((1,), (1,)), ((), ())), preferred_element_type=F32)


def _rms(x):
    return x * lax.rsqrt(jnp.mean(x * x, axis=-1, keepdims=True) + EPS)


def _lane_group(n, group):
    shift = group.bit_length() - 1
    assert 1 << shift == group
    return lax.shift_right_logical(lax.broadcasted_iota(jnp.int32, (1, n), 1), shift)


def _group_standardize(y, group):
    n = y.shape[-1]
    gid = _lane_group(n, group)
    inv = 1.0 / group
    mean = jnp.zeros_like(y)
    for g in range(n // group):
        m = gid == g
        s = jnp.sum(jnp.where(m, y, 0.0), axis=-1, keepdims=True) * inv
        mean = jnp.where(m, s, mean)
    yc = y - mean
    sq = yc * yc
    var = jnp.zeros_like(y)
    for g in range(n // group):
        m = gid == g
        s = jnp.sum(jnp.where(m, sq, 0.0), axis=-1, keepdims=True) * inv
        var = jnp.where(m, s, var)
    return yc * lax.rsqrt(var + EPS)


def _rope(x, cos, sin_lo, sin_hi):
    parts = []
    for c in range(x.shape[-1] // LANES):
        xc = x[:, c * LANES:(c + 1) * LANES]
        up = pltpu.roll(xc, LANES - HEAD_DIM // 2, 1)
        down = pltpu.roll(xc, HEAD_DIM // 2, 1)
        parts.append(xc * cos + up * sin_lo + down * sin_hi)
    return parts[0] if len(parts) == 1 else jnp.concatenate(parts, axis=-1)


def _mod_kernel(c_ref, w_ref, b_ref, o_ref):
    c = c_ref[...]
    cond = c * jax.nn.sigmoid(c)
    o_ref[0] = _dot(cond.astype(BF16), w_ref[0].astype(BF16)) + b_ref[0]


def _mod_call(c_all, w_ada, b_ada):
    depth, d, n = w_ada.shape
    rows = c_all.shape[0]
    tn = d
    return pl.pallas_call(
        _mod_kernel,
        out_shape=jax.ShapeDtypeStruct((depth, rows, n), F32),
        grid=(depth, n // tn),
        in_specs=[
            pl.BlockSpec((rows, d), lambda l, j: (0, 0)),
            pl.BlockSpec((1, d, tn), lambda l, j: (l, 0, j)),
            pl.BlockSpec((1, 1, tn), lambda l, j: (l, 0, j)),
        ],
        out_specs=pl.BlockSpec((1, rows, tn), lambda l, j: (l, 0, j)),
        compiler_params=pltpu.CompilerParams(
            dimension_semantics=("arbitrary", "arbitrary"), vmem_limit_bytes=VMEM_LIMIT),
        name="adaln_mod",
    )(c_all, w_ada, b_ada.reshape(depth, 1, n))


N_IN_OPERANDS = 13


def _in_kernel(*refs, tm, ln, d_model, n_aliased):
    (x_ref, mod_ref, g1_ref, w_ref, cos_ref, slo_ref, shi_ref, wsp_ref, bsp_ref,
     wdw_ref, bdw_ref, gcn_ref, cst_ref) = refs[:N_IN_OPERANDS]
    (q_ref, k_ref, v_ref, qi_ref, ki_ref, wi_ref, ac_ref, av_ref, cnew_ref,
     hp_ref, sh_ref) = refs[N_IN_OPERANDS + n_aliased:]
    j = pl.program_id(1)
    x = x_ref[0]
    mod = mod_ref[0]
    sh1 = mod[:, 0:d_model]
    sc1 = mod[:, d_model:2 * d_model]
    h = (_rms(x) * g1_ref[...]) * (1.0 + sc1) + sh1
    z = _dot(h.astype(BF16), w_ref[...])
    part = lambda off, width: z[:, off:off + width]

    cos = cos_ref[...]
    slo = slo_ref[...]
    shi = shi_ref[...]
    q = _rope(part(OFF_Q, B_WIDTH), cos, slo, shi)
    q_ref[0] = (q * (LOG2E * HEAD_DIM ** -0.5)).astype(BF16)
    k_ref[0] = _rope(part(OFF_K, B_KV_WIDTH), cos, slo, shi)
    v_ref[0] = part(OFF_V, B_KV_WIDTH)
    qi_ref[0] = _rope(part(OFF_QI, IDX_HEADS * IDX_DIM), cos, slo, shi).astype(BF16)
    kiwi = part(OFF_KI, LANES)
    ki_ref[0] = _rope(kiwi, cos, slo, shi)[:, 0:IDX_DIM]
    wi_ref[0] = kiwi

    za = part(OFF_A, 2 * A_WIDTH)
    za = 0.5 * za * (1.0 + lax.erf(za * (2.0 ** -0.5)))
    u = za[:, 0:A_WIDTH]
    vn = _group_standardize(za[:, A_WIDTH:2 * A_WIDTH], A_HEAD_DIM)
    av_ref[0] = vn
    vb = vn.astype(BF16)
    row = lax.broadcasted_iota(jnp.int32, (ln, ln), 0)
    col = lax.broadcasted_iota(jnp.int32, (ln, ln), 1)
    head_of_lane = _lane_group(A_WIDTH, A_HEAD_DIM)
    wsp = [jnp.where(col <= row, wsp_ref[g], 0.0).astype(BF16) for g in range(A_HEADS)]
    gated = []
    for c in range(tm // ln):
        vc = vb[c * ln:(c + 1) * ln, :]
        s = jnp.zeros((ln, A_WIDTH), F32)
        for g in range(A_HEADS):
            s = jnp.where(head_of_lane == g, _dot(wsp[g], vc), s)
        gated.append(u[c * ln:(c + 1) * ln, :] * (s + bsp_ref[...]))
    a_out = gated[0] if len(gated) == 1 else jnp.concatenate(gated, axis=0)
    ac_ref[0, :, 0:A_WIDTH] = a_out.astype(BF16)

    zc = part(OFF_C, 2 * C_WIDTH)
    hc = zc[:, 0:C_WIDTH] * jax.nn.sigmoid(zc[:, C_WIDTH:2 * C_WIDTH])

    @pl.when(j == 0)
    def _():
        hp_ref[0:CTX_ROWS, :] = cst_ref[0]

    hp_ref[CTX_ROWS:CTX_ROWS + tm, :] = hc
    span = tm + CTX_ROWS - SUBLANES
    for s in range(1, SUBLANES):
        sh_ref[s - 1, 0:span, :] = hp_ref[s:s + span, :]
    rb = min(tm, 64)
    wdw = wdw_ref[...]
    conv = []
    for r0 in range(0, tm, rb):
        acc = jnp.zeros((rb, C_WIDTH), F32)
        for t in range(CONV_WIDTH):
            phase = (CTX_PAD + t) % SUBLANES
            base = CTX_PAD + t - phase + r0
            rows = hp_ref[base:base + rb, :] if phase == 0 else sh_ref[phase - 1, base:base + rb, :]
            acc = acc + rows * wdw[t:t + 1, :]
        conv.append(acc)
    y = (conv[0] if len(conv) == 1 else jnp.concatenate(conv, axis=0)) + bdw_ref[...]
    y = _group_standardize(y, C_WIDTH // C_GROUPS) * gcn_ref[...]
    ac_ref[0, :, A_WIDTH:A_WIDTH + C_WIDTH] = (y * jax.nn.sigmoid(y)).astype(BF16)

    tail = hp_ref[tm:tm + CTX_ROWS, :]
    cnew_ref[0] = tail
    hp_ref[0:CTX_ROWS, :] = tail


def _in_call(x, mod, layer, mod_row0, g1, w_in_p, tabs, wsp, bsp_tab, wdw, bdw, gcn, cst, kv_stack, tm):
    b, t, d = x.shape
    depth = w_in_p.shape[0]
    ln = min(t, A_CHUNK)
    cos, slo, shi = tabs
    n_aliased = 0 if kv_stack is None else len(kv_stack)
    kern = functools.partial(_in_kernel, tm=tm, ln=ln, d_model=d, n_aliased=n_aliased)
    row3 = lambda bi, j: (bi, j, 0)
    lay_row4 = lambda bi, j: (layer, bi, j, 0)
    lay3 = lambda bi, j: (layer, 0, 0)
    out_shape = (
        jax.ShapeDtypeStruct((b, t, B_WIDTH), BF16),
        jax.ShapeDtypeStruct((depth, b, t, B_KV_WIDTH), F32),
        jax.ShapeDtypeStruct((depth, b, t, B_KV_WIDTH), F32),
        jax.ShapeDtypeStruct((b, t, IDX_HEADS * IDX_DIM), BF16),
        jax.ShapeDtypeStruct((depth, b, t, IDX_DIM), F32),
        jax.ShapeDtypeStruct((b, t, LANES), F32),
        jax.ShapeDtypeStruct((b, t, A_WIDTH + C_WIDTH), BF16),
        jax.ShapeDtypeStruct((b, t, A_WIDTH), F32),
        jax.ShapeDtypeStruct((b, CTX_ROWS, C_WIDTH), F32),
    )
    out_specs = (
        pl.BlockSpec((1, tm, B_WIDTH), row3),
        pl.BlockSpec((None, 1, tm, B_KV_WIDTH), lay_row4),
        pl.BlockSpec((None, 1, tm, B_KV_WIDTH), lay_row4),
        pl.BlockSpec((1, tm, IDX_HEADS * IDX_DIM), row3),
        pl.BlockSpec((None, 1, tm, IDX_DIM), lay_row4),
        pl.BlockSpec((1, tm, LANES), row3),
        pl.BlockSpec((1, tm, A_WIDTH + C_WIDTH), row3),
        pl.BlockSpec((1, tm, A_WIDTH), row3),
        pl.BlockSpec((1, CTX_ROWS, C_WIDTH), lambda bi, j: (bi, 0, 0)),
    )
    kv_outputs = (1, 2, 4)
    in_specs = [
        pl.BlockSpec((1, tm, d), row3),
        pl.BlockSpec((None, 1, 1, N_MOD * d), lambda bi, j: (layer, mod_row0 + bi, 0, 0)),
        pl.BlockSpec((None, 1, d), lay3),
        pl.BlockSpec((None, d, N_IN_PAD), lay3),
        pl.BlockSpec((tm, LANES), lambda bi, j: (j, 0)),
        pl.BlockSpec((tm, LANES), lambda bi, j: (j, 0)),
        pl.BlockSpec((tm, LANES), lambda bi, j: (j, 0)),
        pl.BlockSpec((None, A_HEADS, ln, ln), lambda bi, j: (layer, 0, 0, 0)),
        pl.BlockSpec((None, ln, A_WIDTH), lay3),
        pl.BlockSpec((None, CONV_WIDTH, C_WIDTH), lay3),
        pl.BlockSpec((None, 1, C_WIDTH), lay3),
        pl.BlockSpec((None, 1, C_WIDTH), lay3),
        pl.BlockSpec((1, CTX_ROWS, C_WIDTH), lambda bi, j: (bi, 0, 0)),
    ]
    args = [x, mod, g1, w_in_p, cos, slo, shi, wsp, bsp_tab, wdw, bdw, gcn, cst]
    assert len(args) == N_IN_OPERANDS
    aliases = {}
    if kv_stack is not None:
        for arr, out_pos in zip(kv_stack, kv_outputs):
            aliases[len(args)] = out_pos
            in_specs.append(pl.BlockSpec(memory_space=pl.ANY))
            args.append(arr)
    return pl.pallas_call(
        kern,
        out_shape=out_shape,
        grid=(b, t // tm),
        in_specs=in_specs,
        out_specs=out_specs,
        scratch_shapes=[pltpu.VMEM((tm + CTX_ROWS, C_WIDTH), F32),
                        pltpu.VMEM((SUBLANES - 1, tm + CTX_ROWS, C_WIDTH), F32)],
        input_output_aliases=aliases,
        compiler_params=pltpu.CompilerParams(
            dimension_semantics=("arbitrary", "arbitrary"), vmem_limit_bytes=VMEM_LIMIT),
        name="in_proj_mixers_ac",
    )(*args)


def _attn_kernel(q_ref, qi_ref, wi_ref, ck_ref, cv_ref, cki_ref, k_ref, v_ref, ki_ref, o_ref,
                 kb_ref, vb_ref, kib_ref, key_ref, bias_ref, *, tq, q_start, topk):
    past = ck_ref.shape[1]
    n_keys = past + k_ref.shape[1]
    lk = kb_ref.shape[0]
    for dst, old, new in ((kb_ref, ck_ref, k_ref), (vb_ref, cv_ref, v_ref), (kib_ref, cki_ref, ki_ref)):
        dst[0:past, :] = old[0]
        dst[past:n_keys, :] = new[0].astype(BF16)
        dst[n_keys:lk, :] = jnp.zeros((lk - n_keys, dst.shape[1]), BF16)
    j = pl.program_id(1)
    qchunk = lax.shift_right_logical(
        q_start + j * tq + lax.broadcasted_iota(jnp.int32, (tq, 1), 0), CHUNK_SHIFT)

    def allowed_at(k0, width):
        kpos = k0 + lax.broadcasted_iota(jnp.int32, (1, width), 1)
        return (lax.shift_right_logical(kpos, CHUNK_SHIFT) <= qchunk) & (kpos < n_keys)

    kib = kib_ref[...]
    qi = qi_ref[0]
    wi = wi_ref[0]
    score = jnp.zeros((tq, lk), F32)
    for h in range(IDX_HEADS):
        d = _dot_t(qi[:, h * IDX_DIM:(h + 1) * IDX_DIM], kib)
        score = score + wi[:, IDX_DIM + h:IDX_DIM + h + 1] * jnp.maximum(d, 0.0)
    score = jnp.where(score == 0.0, 0.0, score)
    score = jnp.where(allowed_at(0, lk), score, -jnp.inf)

    bits = pltpu.bitcast(score, jnp.int32)
    key_ref[...] = bits ^ ((bits >> 31) & jnp.int32(0x7FFFFFFF))
    int_min = jnp.int32(-2 ** 31)

    def digit_step(i, thr):
        unit = lax.shift_left(jnp.int32(1), jnp.int32(30) - 2 * i)
        key = key_ref[...]
        for mult in (1, 2, 3):
            cand = thr + mult * unit if mult == 1 else cand + unit
            cnt = jnp.sum((key >= cand).astype(F32), axis=-1, keepdims=True)
            best = jnp.where(cnt >= topk, cand, thr if mult == 1 else best)
        return best

    thr = lax.fori_loop(0, 16, digit_step, jnp.full((tq, 1), int_min, jnp.int32))

    need = topk - jnp.sum((key_ref[...] > thr).astype(F32), axis=-1, keepdims=True)
    tri = (lax.broadcasted_iota(jnp.int32, (LANES, LANES), 0)
           <= lax.broadcasted_iota(jnp.int32, (LANES, LANES), 1)).astype(F32).astype(BF16)
    seen = jnp.zeros((tq, 1), F32)
    for c in range(lk // LANES):
        sl = slice(c * LANES, (c + 1) * LANES)
        key_c = key_ref[:, sl]
        eq_c = key_c == thr
        eq_f = eq_c.astype(F32)
        rank = _dot(eq_f.astype(BF16), tri) + seen
        chosen = ((key_c > thr) | (eq_c & (rank <= need))) & allowed_at(c * LANES, LANES)
        bias_ref[:, sl] = jnp.where(chosen, 0.0, -jnp.inf)
        seen = seen + jnp.sum(eq_f, axis=-1, keepdims=True)

    q = q_ref[0]
    group = B_HEADS // B_KV_HEADS
    bias = jnp.concatenate([bias_ref[...]] * group, axis=0)
    outs = []
    for n in range(B_KV_HEADS):
        kn = kb_ref[:, n * HEAD_DIM:(n + 1) * HEAD_DIM]
        vn = vb_ref[:, n * HEAD_DIM:(n + 1) * HEAD_DIM]
        qg = jnp.concatenate([q[:, (n * group + g) * HEAD_DIM:(n * group + g + 1) * HEAD_DIM]
                              for g in range(group)], axis=0)
        logits = _dot_t(qg, kn) + bias
        p = jnp.exp2(logits - jnp.max(logits, axis=-1, keepdims=True))
        den = jnp.sum(p, axis=-1, keepdims=True)
        og = _dot(p.astype(BF16), vn) / den
        outs.extend(og[g * tq:(g + 1) * tq] for g in range(group))
    o_ref[0] = jnp.concatenate(outs, axis=-1).astype(BF16)


def _attn_call(q, qi, wi, cache_k, cache_v, cache_ki, k, v, ki, layer, q_start):
    b, t, _ = q.shape
    past = cache_k.shape[2]
    assert past % BF16_SUBLANES == 0 and t % BF16_SUBLANES == 0
    n_keys = past + t
    lk = -(-n_keys // LANES) * LANES
    topk = min(TOPK_MAX, n_keys // 4)
    kern = functools.partial(_attn_kernel, tq=t, q_start=q_start, topk=topk)
    row3 = lambda bi, j: (bi, 0, 0)
    lay_bat4 = lambda bi, j: (layer, bi, 0, 0)
    return pl.pallas_call(
        kern,
        out_shape=jax.ShapeDtypeStruct((b, t, B_WIDTH), BF16),
        grid=(b, 1),
        in_specs=[
            pl.BlockSpec((1, t, B_WIDTH), row3),
            pl.BlockSpec((1, t, IDX_HEADS * IDX_DIM), row3),
            pl.BlockSpec((1, t, LANES), row3),
            pl.BlockSpec((None, 1, past, B_KV_WIDTH), lay_bat4),
            pl.BlockSpec((None, 1, past, B_KV_WIDTH), lay_bat4),
            pl.BlockSpec((None, 1, past, IDX_DIM), lay_bat4),
            pl.BlockSpec((None, 1, t, B_KV_WIDTH), lay_bat4),
            pl.BlockSpec((None, 1, t, B_KV_WIDTH), lay_bat4),
            pl.BlockSpec((None, 1, t, IDX_DIM), lay_bat4),
        ],
        out_specs=pl.BlockSpec((1, t, B_WIDTH), row3),
        scratch_shapes=[pltpu.VMEM((lk, B_KV_WIDTH), BF16), pltpu.VMEM((lk, B_KV_WIDTH), BF16),
                        pltpu.VMEM((lk, IDX_DIM), BF16),
                        pltpu.VMEM((t, lk), jnp.int32), pltpu.VMEM((t, lk), F32)],
        compiler_params=pltpu.CompilerParams(
            dimension_semantics=("arbitrary", "arbitrary"), vmem_limit_bytes=VMEM_LIMIT),
        name="dsa_attention",
    )(q, qi, wi, cache_k, cache_v, cache_ki, k, v, ki)


FOLD_ROWS = 64


def _col_reduce(x, op):
    r = x.shape[0]
    if r > FOLD_ROWS and r % FOLD_ROWS == 0:
        acc = x[:FOLD_ROWS]
        for c in range(1, r // FOLD_ROWS):
            acc = op(acc, x[c * FOLD_ROWS:(c + 1) * FOLD_ROWS])
        x, r = acc, FOLD_ROWS
    while r % 16 == 0:
        x = op(x[:r // 2], x[r // 2:])
        r //= 2
    return (jnp.sum if op is jnp.add else jnp.max)(x, axis=0, keepdims=True)


WORD_BITS = 32
INT_MIN = -2 ** 31
QK_ROWS = 256


def _bit_transpose(words):
    a = list(words)
    j, m = WORD_BITS // 2, 0x0000FFFF
    while j:
        k = 0
        while k < WORD_BITS:
            t = (a[k] ^ lax.shift_right_logical(a[k + j], j)) & m
            a[k] = a[k] ^ t
            a[k + j] = a[k + j] ^ lax.shift_left(t, j)
            k = (k + j + 1) & ~j
        j >>= 1
        m = (m ^ (m << j)) & 0x7FFFFFFF if j else m
    return a


def _select_threshold(key_ref, planes_ref, topk, tq, side_work=()):
    lk = key_ref.shape[0]
    slab = lk // WORD_BITS
    for r0 in range(0, slab, SUBLANES):
        words = [key_ref[i * slab + r0:i * slab + r0 + SUBLANES, :] for i in range(WORD_BITS)]
        for b, plane in enumerate(_bit_transpose(words)):
            planes_ref[b, r0:r0 + SUBLANES, :] = plane
    count = lambda words: _col_reduce(lax.population_count(words), jnp.add)
    alive = jnp.full((slab, tq), -1, jnp.int32)
    left = jnp.full((1, tq), topk, jnp.int32)
    thr = jnp.zeros((1, tq), jnp.int32)
    steps = WORD_BITS // 2
    for step, hi_bit in enumerate(range(WORD_BITS - 1, 0, -2)):
        for piece in side_work[step * len(side_work) // steps:(step + 1) * len(side_work) // steps]:
            piece()
        p_hi = planes_ref[WORD_BITS - 1 - hi_bit]
        p_lo = planes_ref[WORD_BITS - hi_bit]
        if hi_bit == WORD_BITS - 1:
            p_hi = ~p_hi
        u1 = alive & p_hi
        u0 = alive ^ u1
        c11 = u1 & p_lo
        c10 = u1 ^ c11
        c01 = u0 & p_lo
        c00 = u0 ^ c01
        n11 = count(c11)
        n1x = n11 + count(c10)
        n_1 = n1x + count(c01)
        t11 = n11 >= left
        t1x = n1x >= left
        t_1 = n_1 >= left
        alive = jnp.where(t11, c11, jnp.where(t1x, c10, jnp.where(t_1, c01, c00)))
        left = left - jnp.where(t11, 0, jnp.where(t1x, n11, jnp.where(t_1, n1x, n_1)))
        hi_val = jnp.int32(1 << hi_bit if hi_bit < WORD_BITS - 1 else INT_MIN)
        lo_val = jnp.int32(1 << (hi_bit - 1))
        thr = thr | jnp.where(t1x, hi_val, 0) | jnp.where(t11 | (t_1 & ~t1x), lo_val, 0)
    return thr ^ INT_MIN, left


def _attn_t_kernel(*refs, tq, n_keys, q_start, q_block0, topk, aliased):
    q_ref, qi_ref, wi_ref, k_ref, v_ref, ki_ref = refs[:6]
    o_ref, key_ref, bias_ref, planes_ref, logit_ref = refs[7:] if aliased else refs[6:]
    lk = k_ref.shape[1]
    j = pl.program_id(1)
    qchunk = lax.shift_right_logical(
        q_start + (q_block0 + j) * tq + lax.broadcasted_iota(jnp.int32, (1, tq), 1), CHUNK_SHIFT)

    def allowed_at(k0, rows):
        kpos = k0 + lax.broadcasted_iota(jnp.int32, (rows, 1), 0)
        return (lax.shift_right_logical(kpos, CHUNK_SHIFT) <= qchunk) & (kpos < n_keys)

    kib = ki_ref[0].astype(BF16)
    qi = qi_ref[0]
    wi_t = wi_ref[0].T
    score = jnp.zeros((lk, tq), F32)
    for h in range(IDX_HEADS):
        d = _dot_t(kib, qi[:, h * IDX_DIM:(h + 1) * IDX_DIM])
        score = score + wi_t[IDX_DIM + h:IDX_DIM + h + 1, :] * jnp.maximum(d, 0.0)
    score = jnp.where(score == 0.0, 0.0, score)
    free = min(((q_start + q_block0 * tq) // CHUNK + 1) * CHUNK, n_keys, lk)
    if free < lk:
        masked = jnp.where(allowed_at(free, lk - free), score[free:], -jnp.inf)
        score = jnp.concatenate([score[:free], masked], axis=0) if free else masked

    bits = pltpu.bitcast(score, jnp.int32)
    key_ref[...] = bits ^ ((bits >> 31) & jnp.int32(0x7FFFFFFF))

    q = q_ref[0]
    group = B_HEADS // B_KV_HEADS
    qgs = [jnp.concatenate([q[:, (n * group + g) * HEAD_DIM:(n * group + g + 1) * HEAD_DIM]
                            for g in range(group)], axis=0) for n in range(B_KV_HEADS)]

    def logits_block(r0, n):
        kn = k_ref[0, r0:r0 + QK_ROWS, n * HEAD_DIM:(n + 1) * HEAD_DIM].astype(BF16)
        logit_ref[n, r0:r0 + QK_ROWS, :] = _dot_t(kn, qgs[n])

    side_work = [functools.partial(logits_block, r0, n)
                 for r0 in range(0, lk, QK_ROWS) for n in range(B_KV_HEADS)]
    thr, left = _select_threshold(key_ref, planes_ref, topk, tq, side_work)

    clamped = thr < KEY_MIN_FINITE
    thr = jnp.where(clamped, KEY_MIN_FINITE, thr)
    need = jnp.where(clamped, topk, left).astype(F32)
    tril = (lax.broadcasted_iota(jnp.int32, (LANES, LANES), 0)
            >= lax.broadcasted_iota(jnp.int32, (LANES, LANES), 1)).astype(F32).astype(BF16)
    seen = jnp.zeros((1, tq), F32)
    for c in range(lk // LANES):
        sl = slice(c * LANES, (c + 1) * LANES)
        key_c = key_ref[sl, :]
        eq_c = key_c == thr
        eq_f = eq_c.astype(F32)
        rank = _dot(tril, eq_f.astype(BF16)) + seen
        chosen = (key_c > thr) | (eq_c & (rank <= need))
        bias_ref[sl, :] = jnp.where(chosen, 0.0, -jnp.inf)
        seen = seen + _col_reduce(eq_f, jnp.add)

    vt = v_ref[0].T.astype(BF16)
    bias = bias_ref[...]
    ones_rows = jnp.ones((BF16_SUBLANES, lk), BF16)
    outs = []
    for n in range(B_KV_HEADS):
        vtn = jnp.concatenate([vt[n * HEAD_DIM:(n + 1) * HEAD_DIM, :], ones_rows], axis=0)
        ps = []
        for g in range(group):
            lg = logit_ref[n, :, g * tq:(g + 1) * tq] + bias
            ps.append(jnp.exp2(lg - _col_reduce(lg, jnp.maximum)).astype(BF16))
        og = _dot(vtn, jnp.concatenate(ps, axis=1))
        og = og[:HEAD_DIM] / og[HEAD_DIM:HEAD_DIM + 1]
        outs.extend(og[:, g * tq:(g + 1) * tq] for g in range(group))
    o_ref[0] = jnp.concatenate(outs, axis=0).T.astype(BF16)


def _attn_t_call(q, qi, wi, k, v, ki, prev, layer, n_keys, q_start, q_block0, n_q_blocks, lk, tq):
    b, t, _ = q.shape
    topk = min(TOPK_MAX, n_keys // 4)
    aliased = prev is not None
    kern = functools.partial(_attn_t_kernel, tq=tq, n_keys=n_keys, q_start=q_start,
                             q_block0=q_block0, topk=topk, aliased=aliased)
    row3 = lambda bi, j: (bi, q_block0 + j, 0)
    lay_bat4 = lambda bi, j: (layer, bi, 0, 0)
    in_specs = [
        pl.BlockSpec((1, tq, B_WIDTH), row3),
        pl.BlockSpec((1, tq, IDX_HEADS * IDX_DIM), row3),
        pl.BlockSpec((1, tq, LANES), row3),
        pl.BlockSpec((None, 1, lk, B_KV_WIDTH), lay_bat4),
        pl.BlockSpec((None, 1, lk, B_KV_WIDTH), lay_bat4),
        pl.BlockSpec((None, 1, lk, IDX_DIM), lay_bat4),
    ]
    args = [q, qi, wi, k, v, ki]
    if aliased:
        in_specs.append(pl.BlockSpec(memory_space=pl.ANY))
        args.append(prev)
    return pl.pallas_call(
        kern,
        out_shape=jax.ShapeDtypeStruct((b, t, B_WIDTH), BF16),
        grid=(b, n_q_blocks),
        in_specs=in_specs,
        out_specs=pl.BlockSpec((1, tq, B_WIDTH), row3),
        scratch_shapes=[pltpu.VMEM((lk, tq), jnp.int32), pltpu.VMEM((lk, tq), F32),
                        pltpu.VMEM((WORD_BITS, lk // WORD_BITS, tq), jnp.int32),
                        pltpu.VMEM((B_KV_HEADS, lk, B_HEADS // B_KV_HEADS * tq), F32)],
        input_output_aliases={6: 0} if aliased else {},
        compiler_params=pltpu.CompilerParams(
            dimension_semantics=("arbitrary", "arbitrary"), vmem_limit_bytes=VMEM_LIMIT),
        name="dsa_attention_t",
    )(*args)


def _attn_prompt(q, qi, wi, k, v, ki, layer, tq):
    t = q.shape[1]
    nqb = t // tq
    per_seg = max(1, WORD_BITS * SUBLANES // tq)
    assert nqb % per_seg == 0 and (per_seg * tq) % (WORD_BITS * SUBLANES) == 0
    bo = None
    for s in range(nqb // per_seg):
        bo = _attn_t_call(q, qi, wi, k, v, ki, bo, layer, t, 0, s * per_seg, per_seg,
                          (s + 1) * per_seg * tq, tq)
    return bo


def _ffn_kernel(x_ref, ac_ref, bo_ref, mod_ref, g2_ref, wo_ref, wg_ref, wu_ref, wd_ref,
                gf_ref, o_ref, *, d_model, ff_chunk, final):
    x = x_ref[0]
    mod = mod_ref[0]
    ga1 = mod[:, 2 * d_model:3 * d_model]
    sh2 = mod[:, 3 * d_model:4 * d_model]
    sc2 = mod[:, 4 * d_model:5 * d_model]
    ga2 = mod[:, 5 * d_model:6 * d_model]
    ac = ac_ref[0]
    mix = (_dot(ac[:, :A_WIDTH], wo_ref[0:A_WIDTH, :])
           + _dot(bo_ref[0], wo_ref[A_WIDTH:A_WIDTH + B_WIDTH, :])
           + _dot(ac[:, A_WIDTH:], wo_ref[A_WIDTH + B_WIDTH:, :]))
    x1 = x + ga1 * mix
    h2 = ((_rms(x1) * g2_ref[...]) * (1.0 + sc2) + sh2).astype(BF16)
    d_ff = wg_ref.shape[1]
    acc = jnp.zeros_like(x1)
    for c0 in range(0, d_ff, ff_chunk):
        gate = _dot(h2, wg_ref[:, c0:c0 + ff_chunk])
        up = _dot(h2, wu_ref[:, c0:c0 + ff_chunk])
        act = (gate * jax.nn.sigmoid(gate)) * up
        acc = acc + _dot(act.astype(BF16), wd_ref[c0:c0 + ff_chunk, :])
    x2 = x1 + ga2 * acc
    if final:
        o_ref[0] = _rms(x2) * gf_ref[...]
    else:
        o_ref[0] = x2


def _ffn_call(x, ac, bo, mod, layer, mod_row0, g2, wo, wg, wu, wd, gf, tm, final):
    b, t, d = x.shape
    d_ff = wg.shape[2]
    ff_chunk = d_ff // 2 if (d_ff // 2) % LANES == 0 else d_ff
    kern = functools.partial(_ffn_kernel, d_model=d, ff_chunk=ff_chunk, final=final)
    row3 = lambda bi, j: (bi, j, 0)
    lay3 = lambda bi, j: (layer, 0, 0)
    once = pl.Buffered(1)
    return pl.pallas_call(
        kern,
        out_shape=jax.ShapeDtypeStruct((b, t, d), F32),
        grid=(b, t // tm),
        in_specs=[
            pl.BlockSpec((1, tm, d), row3),
            pl.BlockSpec((1, tm, A_WIDTH + C_WIDTH), row3),
            pl.BlockSpec((1, tm, B_WIDTH), row3),
            pl.BlockSpec((None, 1, 1, N_MOD * d), lambda bi, j: (layer, mod_row0 + bi, 0, 0)),
            pl.BlockSpec((None, 1, d), lay3),
            pl.BlockSpec((None, A_WIDTH + B_WIDTH + C_WIDTH, d), lay3, pipeline_mode=once),
            pl.BlockSpec((None, d, d_ff), lay3, pipeline_mode=once),
            pl.BlockSpec((None, d, d_ff), lay3, pipeline_mode=once),
            pl.BlockSpec((None, d_ff, d), lay3, pipeline_mode=once),
            pl.BlockSpec((1, d), lambda bi, j: (0, 0)),
        ],
        out_specs=pl.BlockSpec((1, tm, d), row3),
        compiler_params=pltpu.CompilerParams(
            dimension_semantics=("arbitrary", "arbitrary"), vmem_limit_bytes=VMEM_LIMIT),
        name="out_proj_ffn",
    )(x, ac, bo, mod, g2, wo, wg, wu, wd, gf)


def _rope_tables(start, t):
    half = HEAD_DIM // 2
    inv = jnp.power(ROPE_THETA, -2.0 * jnp.arange(half, dtype=F32) / HEAD_DIM)
    ang = (start + jnp.arange(t)).astype(F32)[:, None] * inv[None, :]
    cos = jnp.cos(ang)
    sin = jnp.sin(ang)
    zero = jnp.zeros_like(sin)
    reps = LANES // HEAD_DIM
    return (jnp.tile(jnp.concatenate([cos, cos], axis=-1), (1, reps)),
            jnp.tile(jnp.concatenate([-sin, zero], axis=-1), (1, reps)),
            jnp.tile(jnp.concatenate([zero, sin], axis=-1), (1, reps)))


def _pad_w_in(w_in):
    n_head = OFF_KI + IDX_DIM + IDX_HEADS
    depth, d, _ = w_in.shape
    pad = jnp.zeros((depth, d, OFF_C - n_head), w_in.dtype)
    return jnp.concatenate([w_in[:, :, :n_head], pad, w_in[:, :, n_head:]], axis=-1).astype(BF16)


def _trunk(x, mod, mod_row0, cache, params, q_start, tm, tq):
    (g_norm1, g_norm2, w_in_p, w_spatial, b_spatial, w_dw, b_dw, g_cnorm,
     wo, wg, wu, wd, g_final) = params
    b, t, d = x.shape
    depth = w_in_p.shape[0]
    ln = min(t, A_CHUNK)
    tabs = _rope_tables(q_start, t)
    wsp = w_spatial[:, :, :ln, :ln]
    bsp_tab = jnp.repeat(jnp.swapaxes(b_spatial[:, :, :ln], 1, 2), A_HEAD_DIM, axis=-1)
    kv_stack, convs, avs = None, [], []
    if cache is not None:
        past = cache[0].shape[2]
        cache_bf16 = (cache[0].astype(BF16).reshape(depth, b, past, B_KV_WIDTH),
                      cache[1].astype(BF16).reshape(depth, b, past, B_KV_WIDTH),
                      cache[2].astype(BF16))
    for l in range(depth):
        if cache is None:
            cst = jnp.zeros((b, CTX_ROWS, C_WIDTH), F32)
        else:
            cst = jnp.pad(cache[3][l], ((0, 0), (CTX_PAD, 0), (0, 0)))
        q, ks, vs, qi, kis, wi, ac, av, cnew = _in_call(
            x, mod, l, mod_row0, g_norm1, w_in_p, tabs, wsp, bsp_tab, w_dw, b_dw, g_cnorm, cst,
            kv_stack, tm)
        kv_stack = (ks, vs, kis)
        if cache is None:
            bo = _attn_prompt(q, qi, wi, ks, vs, kis, l, tq)
        else:
            bo = _attn_call(q, qi, wi, *cache_bf16, ks, vs, kis, l, q_start)
        x = _ffn_call(x, ac, bo, mod, l, mod_row0, g_norm2, wo, wg, wu, wd, g_final,
                      tm, final=(l == depth - 1))
        convs.append(cnew[:, CTX_PAD:, :])
        avs.append(av)
    ks, vs, kis = kv_stack
    return (x, ks.reshape(depth, b, t, B_KV_HEADS, HEAD_DIM), vs.reshape(depth, b, t, B_KV_HEADS, HEAD_DIM),
            kis, jnp.stack(convs), jnp.stack(avs))


def kernel(x_prompt, x_sample, cache_k, cache_v, cache_kidx, state_conv, c_prompt, c_sample,
           w_ada, b_ada, g_norm1, g_norm2, w_in, w_spatial, b_spatial, w_dw, b_dw, g_cnorm,
           w_out, w_gate, w_up, w_down, g_final):
    depth, d = g_norm1.shape
    nb_p, t_p, _ = x_prompt.shape
    nb_s, t_s, _ = x_sample.shape
    past = cache_k.shape[2]

    rows = nb_p + nb_s
    rows_pad = -(-rows // 8) * 8
    c_all = jnp.pad(jnp.concatenate([c_prompt, c_sample], axis=0), ((0, rows_pad - rows), (0, 0)))
    mod = _mod_call(c_all, w_ada, b_ada).reshape(depth, rows_pad, 1, N_MOD * d)

    params = (
        g_norm1.reshape(depth, 1, d), g_norm2.reshape(depth, 1, d), _pad_w_in(w_in),
        w_spatial, b_spatial, w_dw, b_dw.reshape(depth, 1, C_WIDTH), g_cnorm.reshape(depth, 1, C_WIDTH),
        w_out.astype(BF16), w_gate.astype(BF16), w_up.astype(BF16), w_down.astype(BF16),
        g_final.reshape(1, d),
    )

    y_p, p_k, p_v, p_ki, p_conv, _ = _trunk(
        x_prompt, mod, 0, None, params, 0, tm=min(t_p, 512), tq=min(t_p, 256))
    y_s, s_k, s_v, s_ki, s_conv, s_av = _trunk(
        x_sample, mod, nb_p, (cache_k, cache_v, cache_kidx, state_conv), params, past,
        tm=t_s, tq=t_s)
    return (y_p, y_s, p_k, p_v, p_ki, p_conv, s_k, s_v, s_ki, s_conv, s_av)
```

```python
import functools

import jax
import jax.numpy as jnp
from jax import lax
from jax.experimental import pallas as pl
from jax.experimental.pallas import tpu as pltpu

CHUNK = 64
CHUNK_SHIFT = 6
assert 1 << CHUNK_SHIFT == CHUNK
A_HEADS = 4
A_HEAD_DIM = 64
A_WIDTH = A_HEADS * A_HEAD_DIM
A_CHUNK = 128
B_HEADS = 8
B_KV_HEADS = 2
HEAD_DIM = 64
B_WIDTH = B_HEADS * HEAD_DIM
B_KV_WIDTH = B_KV_HEADS * HEAD_DIM
IDX_HEADS = 4
IDX_DIM = 64
TOPK_MAX = 256
ROPE_THETA = 10000.0
C_GROUPS = 4
C_WIDTH = 256
CONV_WIDTH = 31
N_MOD = 6
EPS = 1e-6

LANES = 128
SUBLANES = 8
BF16_SUBLANES = 16
CTX_ROWS = 32
CTX_PAD = CTX_ROWS - (CONV_WIDTH - 1)
VMEM_LIMIT = 56 * 1024 * 1024

OFF_A = 0
OFF_Q = OFF_A + 2 * A_WIDTH
OFF_K = OFF_Q + B_WIDTH
OFF_V = OFF_K + B_KV_WIDTH
OFF_QI = OFF_V + B_KV_WIDTH
OFF_KI = OFF_QI + IDX_HEADS * IDX_DIM
OFF_C = OFF_KI + LANES
N_IN_PAD = OFF_C + 2 * C_WIDTH

LOG2E = 1.4426950408889634
KEY_MIN_FINITE = -2139095040

F32 = jnp.float32
BF16 = jnp.bfloat16


def _dot(a, b):
    return jnp.dot(a, b, preferred_element_type=F32)


def _dot_t(a, b):
    return lax.dot_general(a, b, (((1,), (1,)), ((), ())), preferred_element_type=F32)


def _rms(x):
    return x * lax.rsqrt(jnp.mean(x * x, axis=-1, keepdims=True) + EPS)


def _lane_group(n, group):
    shift = group.bit_length() - 1
    assert 1 << shift == group
    return lax.shift_right_logical(lax.broadcasted_iota(jnp.int32, (1, n), 1), shift)


def _group_standardize(y, group):
    n = y.shape[-1]
    gid = _lane_group(n, group)
    inv = 1.0 / group
    mean = jnp.zeros_like(y)
    for g in range(n // group):
        m = gid == g
        s = jnp.sum(jnp.where(m, y, 0.0), axis=-1, keepdims=True) * inv
        mean = jnp.where(m, s, mean)
    yc = y - mean
    sq = yc * yc
    var = jnp.zeros_like(y)
    for g in range(n // group):
        m = gid == g
        s = jnp.sum(jnp.where(m, sq, 0.0), axis=-1, keepdims=True) * inv
        var = jnp.where(m, s, var)
    return yc * lax.rsqrt(var + EPS)


def _rope(x, cos, sin_lo, sin_hi):
    parts = []
    for c in range(x.shape[-1] // LANES):
        xc = x[:, c * LANES:(c + 1) * LANES]
        up = pltpu.roll(xc, LANES - HEAD_DIM // 2, 1)
        down = pltpu.roll(xc, HEAD_DIM // 2, 1)
        parts.append(xc * cos + up * sin_lo + down * sin_hi)
    return parts[0] if len(parts) == 1 else jnp.concatenate(parts, axis=-1)


def _mod_kernel(c_ref, w_ref, b_ref, o_ref):
    c = c_ref[...]
    cond = c * jax.nn.sigmoid(c)
    o_ref[0] = _dot(cond.astype(BF16), w_ref[0].astype(BF16)) + b_ref[0]


def _mod_call(c_all, w_ada, b_ada):
    depth, d, n = w_ada.shape
    rows = c_all.shape[0]
    tn = d
    return pl.pallas_call(
        _mod_kernel,
        out_shape=jax.ShapeDtypeStruct((depth, rows, n), F32),
        grid=(depth, n // tn),
        in_specs=[
            pl.BlockSpec((rows, d), lambda l, j: (0, 0)),
            pl.BlockSpec((1, d, tn), lambda l, j: (l, 0, j)),
            pl.BlockSpec((1, 1, tn), lambda l, j: (l, 0, j)),
        ],
        out_specs=pl.BlockSpec((1, rows, tn), lambda l, j: (l, 0, j)),
        compiler_params=pltpu.CompilerParams(
            dimension_semantics=("arbitrary", "arbitrary"), vmem_limit_bytes=VMEM_LIMIT),
        name="adaln_mod",
    )(c_all, w_ada, b_ada.reshape(depth, 1, n))


N_IN_OPERANDS = 13


def _in_kernel(*refs, tm, ln, d_model, n_aliased):
    (x_ref, mod_ref, g1_ref, w_ref, cos_ref, slo_ref, shi_ref, wsp_ref, bsp_ref,
     wdw_ref, bdw_ref, gcn_ref, cst_ref) = refs[:N_IN_OPERANDS]
    (q_ref, k_ref, v_ref, qi_ref, ki_ref, wi_ref, ac_ref, av_ref, cnew_ref,
     hp_ref, sh_ref) = refs[N_IN_OPERANDS + n_aliased:]
    j = pl.program_id(1)
    x = x_ref[0]
    mod = mod_ref[0]
    sh1 = mod[:, 0:d_model]
    sc1 = mod[:, d_model:2 * d_model]
    h = (_rms(x) * g1_ref[...]) * (1.0 + sc1) + sh1
    z = _dot(h.astype(BF16), w_ref[...])
    part = lambda off, width: z[:, off:off + width]

    cos = cos_ref[...]
    slo = slo_ref[...]
    shi = shi_ref[...]
    q = _rope(part(OFF_Q, B_WIDTH), cos, slo, shi)
    q_ref[0] = (q * (LOG2E * HEAD_DIM ** -0.5)).astype(BF16)
    k_ref[0] = _rope(part(OFF_K, B_KV_WIDTH), cos, slo, shi)
    v_ref[0] = part(OFF_V, B_KV_WIDTH)
    qi_ref[0] = _rope(part(OFF_QI, IDX_HEADS * IDX_DIM), cos, slo, shi).astype(BF16)
    kiwi = part(OFF_KI, LANES)
    ki_ref[0] = _rope(kiwi, cos, slo, shi)[:, 0:IDX_DIM]
    wi_ref[0] = kiwi

    za = part(OFF_A, 2 * A_WIDTH)
    za = 0.5 * za * (1.0 + lax.erf(za * (2.0 ** -0.5)))
    u = za[:, 0:A_WIDTH]
    vn = _group_standardize(za[:, A_WIDTH:2 * A_WIDTH], A_HEAD_DIM)
    av_ref[0] = vn
    vb = vn.astype(BF16)
    row = lax.broadcasted_iota(jnp.int32, (ln, ln), 0)
    col = lax.broadcasted_iota(jnp.int32, (ln, ln), 1)
    head_of_lane = _lane_group(A_WIDTH, A_HEAD_DIM)
    wsp = [jnp.where(col <= row, wsp_ref[g], 0.0).astype(BF16) for g in range(A_HEADS)]
    gated = []
    for c in range(tm // ln):
        vc = vb[c * ln:(c + 1) * ln, :]
        s = jnp.zeros((ln, A_WIDTH), F32)
        for g in range(A_HEADS):
            s = jnp.where(head_of_lane == g, _dot(wsp[g], vc), s)
        gated.append(u[c * ln:(c + 1) * ln, :] * (s + bsp_ref[...]))
    a_out = gated[0] if len(gated) == 1 else jnp.concatenate(gated, axis=0)
    ac_ref[0, :, 0:A_WIDTH] = a_out.astype(BF16)

    zc = part(OFF_C, 2 * C_WIDTH)
    hc = zc[:, 0:C_WIDTH] * jax.nn.sigmoid(zc[:, C_WIDTH:2 * C_WIDTH])

    @pl.when(j == 0)
    def _():
        hp_ref[0:CTX_ROWS, :] = cst_ref[0]

    hp_ref[CTX_ROWS:CTX_ROWS + tm, :] = hc
    span = tm + CTX_ROWS - SUBLANES
    for s in range(1, SUBLANES):
        sh_ref[s - 1, 0:span, :] = hp_ref[s:s + span, :]
    rb = min(tm, 64)
    wdw = wdw_ref[...]
    conv = []
    for r0 in range(0, tm, rb):
        acc = jnp.zeros((rb, C_WIDTH), F32)
        for t in range(CONV_WIDTH):
            phase = (CTX_PAD + t) % SUBLANES
            base = CTX_PAD + t - phase + r0
            rows = hp_ref[base:base + rb, :] if phase == 0 else sh_ref[phase - 1, base:base + rb, :]
            acc = acc + rows * wdw[t:t + 1, :]
        conv.append(acc)
    y = (conv[0] if len(conv) == 1 else jnp.concatenate(conv, axis=0)) + bdw_ref[...]
    y = _group_standardize(y, C_WIDTH // C_GROUPS) * gcn_ref[...]
    ac_ref[0, :, A_WIDTH:A_WIDTH + C_WIDTH] = (y * jax.nn.sigmoid(y)).astype(BF16)

    tail = hp_ref[tm:tm + CTX_ROWS, :]
    cnew_ref[0] = tail
    hp_ref[0:CTX_ROWS, :] = tail


def _in_call(x, mod, layer, mod_row0, g1, w_in_p, tabs, wsp, bsp_tab, wdw, bdw, gcn, cst, kv_stack, tm):
    b, t, d = x.shape
    depth = w_in_p.shape[0]
    ln = min(t, A_CHUNK)
    cos, slo, shi = tabs
    n_aliased = 0 if kv_stack is None else len(kv_stack)
    kern = functools.partial(_in_kernel, tm=tm, ln=ln, d_model=d, n_aliased=n_aliased)
    row3 = lambda bi, j: (bi, j, 0)
    lay_row4 = lambda bi, j: (layer, bi, j, 0)
    lay3 = lambda bi, j: (layer, 0, 0)
    out_shape = (
        jax.ShapeDtypeStruct((b, t, B_WIDTH), BF16),
        jax.ShapeDtypeStruct((depth, b, t, B_KV_WIDTH), F32),
        jax.ShapeDtypeStruct((depth, b, t, B_KV_WIDTH), F32),
        jax.ShapeDtypeStruct((b, t, IDX_HEADS * IDX_DIM), BF16),
        jax.ShapeDtypeStruct((depth, b, t, IDX_DIM), F32),
        jax.ShapeDtypeStruct((b, t, LANES), F32),
        jax.ShapeDtypeStruct((b, t, A_WIDTH + C_WIDTH), BF16),
        jax.ShapeDtypeStruct((b, t, A_WIDTH), F32),
        jax.ShapeDtypeStruct((b, CTX_ROWS, C_WIDTH), F32),
    )
    out_specs = (
        pl.BlockSpec((1, tm, B_WIDTH), row3),
        pl.BlockSpec((None, 1, tm, B_KV_WIDTH), lay_row4),
        pl.BlockSpec((None, 1, tm, B_KV_WIDTH), lay_row4),
        pl.BlockSpec((1, tm, IDX_HEADS * IDX_DIM), row3),
        pl.BlockSpec((None, 1, tm, IDX_DIM), lay_row4),
        pl.BlockSpec((1, tm, LANES), row3),
        pl.BlockSpec((1, tm, A_WIDTH + C_WIDTH), row3),
        pl.BlockSpec((1, tm, A_WIDTH), row3),
        pl.BlockSpec((1, CTX_ROWS, C_WIDTH), lambda bi, j: (bi, 0, 0)),
    )
    kv_outputs = (1, 2, 4)
    in_specs = [
        pl.BlockSpec((1, tm, d), row3),
        pl.BlockSpec((None, 1, 1, N_MOD * d), lambda bi, j: (layer, mod_row0 + bi, 0, 0)),
        pl.BlockSpec((None, 1, d), lay3),
        pl.BlockSpec((None, d, N_IN_PAD), lay3),
        pl.BlockSpec((tm, LANES), lambda bi, j: (j, 0)),
        pl.BlockSpec((tm, LANES), lambda bi, j: (j, 0)),
        pl.BlockSpec((tm, LANES), lambda bi, j: (j, 0)),
        pl.BlockSpec((None, A_HEADS, ln, ln), lambda bi, j: (layer, 0, 0, 0)),
        pl.BlockSpec((None, ln, A_WIDTH), lay3),
        pl.BlockSpec((None, CONV_WIDTH, C_WIDTH), lay3),
        pl.BlockSpec((None, 1, C_WIDTH), lay3),
        pl.BlockSpec((None, 1, C_WIDTH), lay3),
        pl.BlockSpec((1, CTX_ROWS, C_WIDTH), lambda bi, j: (bi, 0, 0)),
    ]
    args = [x, mod, g1, w_in_p, cos, slo, shi, wsp, bsp_tab, wdw, bdw, gcn, cst]
    assert len(args) == N_IN_OPERANDS
    aliases = {}
    if kv_stack is not None:
        for arr, out_pos in zip(kv_stack, kv_outputs):
            aliases[len(args)] = out_pos
            in_specs.append(pl.BlockSpec(memory_space=pl.ANY))
            args.append(arr)
    return pl.pallas_call(
        kern,
        out_shape=out_shape,
        grid=(b, t // tm),
        in_specs=in_specs,
        out_specs=out_specs,
        scratch_shapes=[pltpu.VMEM((tm + CTX_ROWS, C_WIDTH), F32),
                        pltpu.VMEM((SUBLANES - 1, tm + CTX_ROWS, C_WIDTH), F32)],
        input_output_aliases=aliases,
        compiler_params=pltpu.CompilerParams(
            dimension_semantics=("arbitrary", "arbitrary"), vmem_limit_bytes=VMEM_LIMIT),
        name="in_proj_mixers_ac",
    )(*args)


def _attn_kernel(q_ref, qi_ref, wi_ref, ck_ref, cv_ref, cki_ref, k_ref, v_ref, ki_ref, o_ref,
                 kb_ref, vb_ref, kib_ref, key_ref, bias_ref, *, tq, q_start, topk):
    past = ck_ref.shape[1]
    n_keys = past + k_ref.shape[1]
    lk = kb_ref.shape[0]
    for dst, old, new in ((kb_ref, ck_ref, k_ref), (vb_ref, cv_ref, v_ref), (kib_ref, cki_ref, ki_ref)):
        dst[0:past, :] = old[0]
        dst[past:n_keys, :] = new[0].astype(BF16)
        dst[n_keys:lk, :] = jnp.zeros((lk - n_keys, dst.shape[1]), BF16)
    j = pl.program_id(1)
    qchunk = lax.shift_right_logical(
        q_start + j * tq + lax.broadcasted_iota(jnp.int32, (tq, 1), 0), CHUNK_SHIFT)

    def allowed_at(k0, width):
        kpos = k0 + lax.broadcasted_iota(jnp.int32, (1, width), 1)
        return (lax.shift_right_logical(kpos, CHUNK_SHIFT) <= qchunk) & (kpos < n_keys)

    kib = kib_ref[...]
    qi = qi_ref[0]
    wi = wi_ref[0]
    score = jnp.zeros((tq, lk), F32)
    for h in range(IDX_HEADS):
        d = _dot_t(qi[:, h * IDX_DIM:(h + 1) * IDX_DIM], kib)
        score = score + wi[:, IDX_DIM + h:IDX_DIM + h + 1] * jnp.maximum(d, 0.0)
    score = jnp.where(score == 0.0, 0.0, score)
    score = jnp.where(allowed_at(0, lk), score, -jnp.inf)

    bits = pltpu.bitcast(score, jnp.int32)
    key_ref[...] = bits ^ ((bits >> 31) & jnp.int32(0x7FFFFFFF))
    int_min = jnp.int32(-2 ** 31)

    def digit_step(i, thr):
        unit = lax.shift_left(jnp.int32(1), jnp.int32(30) - 2 * i)
        key = key_ref[...]
        for mult in (1, 2, 3):
            cand = thr + mult * unit if mult == 1 else cand + unit
            cnt = jnp.sum((key >= cand).astype(F32), axis=-1, keepdims=True)
            best = jnp.where(cnt >= topk, cand, thr if mult == 1 else best)
        return best

    thr = lax.fori_loop(0, 16, digit_step, jnp.full((tq, 1), int_min, jnp.int32))

    need = topk - jnp.sum((key_ref[...] > thr).astype(F32), axis=-1, keepdims=True)
    tri = (lax.broadcasted_iota(jnp.int32, (LANES, LANES), 0)
           <= lax.broadcasted_iota(jnp.int32, (LANES, LANES), 1)).astype(F32).astype(BF16)
    seen = jnp.zeros((tq, 1), F32)
    for c in range(lk // LANES):
        sl = slice(c * LANES, (c + 1) * LANES)
        key_c = key_ref[:, sl]
        eq_c = key_c == thr
        eq_f = eq_c.astype(F32)
        rank = _dot(eq_f.astype(BF16), tri) + seen
        chosen = ((key_c > thr) | (eq_c & (rank <= need))) & allowed_at(c * LANES, LANES)
        bias_ref[:, sl] = jnp.where(chosen, 0.0, -jnp.inf)
        seen = seen + jnp.sum(eq_f, axis=-1, keepdims=True)

    q = q_ref[0]
    group = B_HEADS // B_KV_HEADS
    bias = jnp.concatenate([bias_ref[...]] * group, axis=0)
    outs = []
    for n in range(B_KV_HEADS):
        kn = kb_ref[:, n * HEAD_DIM:(n + 1) * HEAD_DIM]
        vn = vb_ref[:, n * HEAD_DIM:(n + 1) * HEAD_DIM]
        qg = jnp.concatenate([q[:, (n * group + g) * HEAD_DIM:(n * group + g + 1) * HEAD_DIM]
                              for g in range(group)], axis=0)
        logits = _dot_t(qg, kn) + bias
        p = jnp.exp2(logits - jnp.max(logits, axis=-1, keepdims=True))
        den = jnp.sum(p, axis=-1, keepdims=True)
        og = _dot(p.astype(BF16), vn) / den
        outs.extend(og[g * tq:(g + 1) * tq] for g in range(group))
    o_ref[0] = jnp.concatenate(outs, axis=-1).astype(BF16)


def _attn_call(q, qi, wi, cache_k, cache_v, cache_ki, k, v, ki, layer, q_start):
    b, t, _ = q.shape
    past = cache_k.shape[2]
    assert past % BF16_SUBLANES == 0 and t % BF16_SUBLANES == 0
    n_keys = past + t
    lk = -(-n_keys // LANES) * LANES
    topk = min(TOPK_MAX, n_keys // 4)
    kern = functools.partial(_attn_kernel, tq=t, q_start=q_start, topk=topk)
    row3 = lambda bi, j: (bi, 0, 0)
    lay_bat4 = lambda bi, j: (layer, bi, 0, 0)
    return pl.pallas_call(
        kern,
        out_shape=jax.ShapeDtypeStruct((b, t, B_WIDTH), BF16),
        grid=(b, 1),
        in_specs=[
            pl.BlockSpec((1, t, B_WIDTH), row3),
            pl.BlockSpec((1, t, IDX_HEADS * IDX_DIM), row3),
            pl.BlockSpec((1, t, LANES), row3),
            pl.BlockSpec((None, 1, past, B_KV_WIDTH), lay_bat4),
            pl.BlockSpec((None, 1, past, B_KV_WIDTH), lay_bat4),
            pl.BlockSpec((None, 1, past, IDX_DIM), lay_bat4),
            pl.BlockSpec((None, 1, t, B_KV_WIDTH), lay_bat4),
            pl.BlockSpec((None, 1, t, B_KV_WIDTH), lay_bat4),
            pl.BlockSpec((None, 1, t, IDX_DIM), lay_bat4),
        ],
        out_specs=pl.BlockSpec((1, t, B_WIDTH), row3),
        scratch_shapes=[pltpu.VMEM((lk, B_KV_WIDTH), BF16), pltpu.VMEM((lk, B_KV_WIDTH), BF16),
                        pltpu.VMEM((lk, IDX_DIM), BF16),
                        pltpu.VMEM((t, lk), jnp.int32), pltpu.VMEM((t, lk), F32)],
        compiler_params=pltpu.CompilerParams(
            dimension_semantics=("arbitrary", "arbitrary"), vmem_limit_bytes=VMEM_LIMIT),
        name="dsa_attention",
    )(q, qi, wi, cache_k, cache_v, cache_ki, k, v, ki)


FOLD_ROWS = 64


def _col_reduce(x, op):
    r = x.shape[0]
    if r > FOLD_ROWS and r % FOLD_ROWS == 0:
        acc = x[:FOLD_ROWS]
        for c in range(1, r // FOLD_ROWS):
            acc = op(acc, x[c * FOLD_ROWS:(c + 1) * FOLD_ROWS])
        x, r = acc, FOLD_ROWS
    while r % 16 == 0:
        x = op(x[:r // 2], x[r // 2:])
        r //= 2
    return (jnp.sum if op is jnp.add else jnp.max)(x, axis=0, keepdims=True)


WORD_BITS = 32
INT_MIN = -2 ** 31
QK_ROWS = 256
FFN_MIN_ROWS = 256


def _bit_transpose(words):
    a = list(words)
    j, m = WORD_BITS // 2, 0x0000FFFF
    while j:
        k = 0
        while k < WORD_BITS:
            t = (a[k] ^ lax.shift_right_logical(a[k + j], j)) & m
            a[k] = a[k] ^ t
            a[k + j] = a[k + j] ^ lax.shift_left(t, j)
            k = (k + j + 1) & ~j
        j >>= 1
        m = (m ^ (m << j)) & 0x7FFFFFFF if j else m
    return a


def _select_threshold(key_ref, planes_ref, topk, tq, side_work=()):
    lk = key_ref.shape[0]
    slab = lk // WORD_BITS
    for r0 in range(0, slab, SUBLANES):
        words = [key_ref[i * slab + r0:i * slab + r0 + SUBLANES, :] for i in range(WORD_BITS)]
        for b, plane in enumerate(_bit_transpose(words)):
            planes_ref[b, r0:r0 + SUBLANES, :] = plane
    count = lambda words: _col_reduce(lax.population_count(words), jnp.add)
    alive = jnp.full((slab, tq), -1, jnp.int32)
    left = jnp.full((1, tq), topk, jnp.int32)
    thr = jnp.zeros((1, tq), jnp.int32)
    steps = WORD_BITS // 2
    for step, hi_bit in enumerate(range(WORD_BITS - 1, 0, -2)):
        for piece in side_work[step * len(side_work) // steps:(step + 1) * len(side_work) // steps]:
            piece()
        p_hi = planes_ref[WORD_BITS - 1 - hi_bit]
        p_lo = planes_ref[WORD_BITS - hi_bit]
        if hi_bit == WORD_BITS - 1:
            p_hi = ~p_hi
        u1 = alive & p_hi
        u0 = alive ^ u1
        c11 = u1 & p_lo
        c10 = u1 ^ c11
        c01 = u0 & p_lo
        c00 = u0 ^ c01
        n11 = count(c11)
        n1x = n11 + count(c10)
        n_1 = n1x + count(c01)
        t11 = n11 >= left
        t1x = n1x >= left
        t_1 = n_1 >= left
        alive = jnp.where(t11, c11, jnp.where(t1x, c10, jnp.where(t_1, c01, c00)))
        left = left - jnp.where(t11, 0, jnp.where(t1x, n11, jnp.where(t_1, n1x, n_1)))
        hi_val = jnp.int32(1 << hi_bit if hi_bit < WORD_BITS - 1 else INT_MIN)
        lo_val = jnp.int32(1 << (hi_bit - 1))
        thr = thr | jnp.where(t1x, hi_val, 0) | jnp.where(t11 | (t_1 & ~t1x), lo_val, 0)
    return thr ^ INT_MIN, left


def _attn_t_kernel(*refs, tq, n_keys, q_start, q_block0, topk, aliased):
    q_ref, qi_ref, wi_ref, k_ref, v_ref, ki_ref = refs[:6]
    o_ref, key_ref, bias_ref, planes_ref, logit_ref = refs[7:] if aliased else refs[6:]
    lk = k_ref.shape[1]
    j = pl.program_id(1)
    qchunk = lax.shift_right_logical(
        q_start + (q_block0 + j) * tq + lax.broadcasted_iota(jnp.int32, (1, tq), 1), CHUNK_SHIFT)

    def allowed_at(k0, rows):
        kpos = k0 + lax.broadcasted_iota(jnp.int32, (rows, 1), 0)
        return (lax.shift_right_logical(kpos, CHUNK_SHIFT) <= qchunk) & (kpos < n_keys)

    kib = ki_ref[0].astype(BF16)
    qi = qi_ref[0]
    wi_t = wi_ref[0].T
    score = jnp.zeros((lk, tq), F32)
    for h in range(IDX_HEADS):
        d = _dot_t(kib, qi[:, h * IDX_DIM:(h + 1) * IDX_DIM])
        score = score + wi_t[IDX_DIM + h:IDX_DIM + h + 1, :] * jnp.maximum(d, 0.0)
    score = jnp.where(score == 0.0, 0.0, score)
    free = min(((q_start + q_block0 * tq) // CHUNK + 1) * CHUNK, n_keys, lk)
    if free < lk:
        masked = jnp.where(allowed_at(free, lk - free), score[free:], -jnp.inf)
        score = jnp.concatenate([score[:free], masked], axis=0) if free else masked

    bits = pltpu.bitcast(score, jnp.int32)
    key_ref[...] = bits ^ ((bits >> 31) & jnp.int32(0x7FFFFFFF))

    q = q_ref[0]
    group = B_HEADS // B_KV_HEADS
    qgs = [jnp.concatenate([q[:, (n * group + g) * HEAD_DIM:(n * group + g + 1) * HEAD_DIM]
                            for g in range(group)], axis=0) for n in range(B_KV_HEADS)]

    def logits_block(r0, n):
        kn = k_ref[0, r0:r0 + QK_ROWS, n * HEAD_DIM:(n + 1) * HEAD_DIM].astype(BF16)
        logit_ref[n, r0:r0 + QK_ROWS, :] = _dot_t(kn, qgs[n])

    side_work = [functools.partial(logits_block, r0, n)
                 for r0 in range(0, lk, QK_ROWS) for n in range(B_KV_HEADS)]
    thr, left = _select_threshold(key_ref, planes_ref, topk, tq, side_work)

    clamped = thr < KEY_MIN_FINITE
    thr = jnp.where(clamped, KEY_MIN_FINITE, thr)
    need = jnp.where(clamped, topk, left).astype(F32)
    tril = (lax.broadcasted_iota(jnp.int32, (LANES, LANES), 0)
            >= lax.broadcasted_iota(jnp.int32, (LANES, LANES), 1)).astype(F32).astype(BF16)
    seen = jnp.zeros((1, tq), F32)
    for c in range(lk // LANES):
        sl = slice(c * LANES, (c + 1) * LANES)
        key_c = key_ref[sl, :]
        eq_c = key_c == thr
        eq_f = eq_c.astype(F32)
        rank = _dot(tril, eq_f.astype(BF16)) + seen
        chosen = (key_c > thr) | (eq_c & (rank <= need))
        bias_ref[sl, :] = jnp.where(chosen, 0.0, -jnp.inf)
        seen = seen + _col_reduce(eq_f, jnp.add)

    vt = v_ref[0].T.astype(BF16)
    bias = bias_ref[...]
    ones_rows = jnp.ones((BF16_SUBLANES, lk), BF16)
    outs = []
    for n in range(B_KV_HEADS):
        vtn = jnp.concatenate([vt[n * HEAD_DIM:(n + 1) * HEAD_DIM, :], ones_rows], axis=0)
        ps = []
        for g in range(group):
            lg = logit_ref[n, :, g * tq:(g + 1) * tq] + bias
            ps.append(jnp.exp2(lg - _col_reduce(lg, jnp.maximum)).astype(BF16))
        og = _dot(vtn, jnp.concatenate(ps, axis=1))
        og = og[:HEAD_DIM] / og[HEAD_DIM:HEAD_DIM + 1]
        outs.extend(og[:, g * tq:(g + 1) * tq] for g in range(group))
    o_ref[0] = jnp.concatenate(outs, axis=0).T.astype(BF16)


def _attn_t_call(q, qi, wi, k, v, ki, prev, layer, n_keys, q_start, q_block0, n_q_blocks, lk, tq):
    b, t, _ = q.shape
    topk = min(TOPK_MAX, n_keys // 4)
    aliased = prev is not None
    kern = functools.partial(_attn_t_kernel, tq=tq, n_keys=n_keys, q_start=q_start,
                             q_block0=q_block0, topk=topk, aliased=aliased)
    row3 = lambda bi, j: (bi, q_block0 + j, 0)
    lay_bat4 = lambda bi, j: (layer, bi, 0, 0)
    in_specs = [
        pl.BlockSpec((1, tq, B_WIDTH), row3),
        pl.BlockSpec((1, tq, IDX_HEADS * IDX_DIM), row3),
        pl.BlockSpec((1, tq, LANES), row3),
        pl.BlockSpec((None, 1, lk, B_KV_WIDTH), lay_bat4),
        pl.BlockSpec((None, 1, lk, B_KV_WIDTH), lay_bat4),
        pl.BlockSpec((None, 1, lk, IDX_DIM), lay_bat4),
    ]
    args = [q, qi, wi, k, v, ki]
    if aliased:
        in_specs.append(pl.BlockSpec(memory_space=pl.ANY))
        args.append(prev)
    return pl.pallas_call(
        kern,
        out_shape=jax.ShapeDtypeStruct((b, t, B_WIDTH), BF16),
        grid=(b, n_q_blocks),
        in_specs=in_specs,
        out_specs=pl.BlockSpec((1, tq, B_WIDTH), row3),
        scratch_shapes=[pltpu.VMEM((lk, tq), jnp.int32), pltpu.VMEM((lk, tq), F32),
                        pltpu.VMEM((WORD_BITS, lk // WORD_BITS, tq), jnp.int32),
                        pltpu.VMEM((B_KV_HEADS, lk, B_HEADS // B_KV_HEADS * tq), F32)],
        input_output_aliases={6: 0} if aliased else {},
        compiler_params=pltpu.CompilerParams(
            dimension_semantics=("arbitrary", "arbitrary"), vmem_limit_bytes=VMEM_LIMIT),
        name="dsa_attention_t",
    )(*args)


def _attn_prompt(q, qi, wi, k, v, ki, layer, tq):
    t = q.shape[1]
    nqb = t // tq
    per_seg = max(1, WORD_BITS * SUBLANES // tq)
    assert nqb % per_seg == 0 and (per_seg * tq) % (WORD_BITS * SUBLANES) == 0
    bo = None
    for s in range(nqb // per_seg):
        bo = _attn_t_call(q, qi, wi, k, v, ki, bo, layer, t, 0, s * per_seg, per_seg,
                          (s + 1) * per_seg * tq, tq)
    return bo


def _ffn_kernel(x_ref, ac_ref, bo_ref, mod_ref, g2_ref, wo_ref, wg_ref, wu_ref, wd_ref,
                gf_ref, o_ref, *, d_model, ff_chunk, final):
    nb, tm, _ = x_ref.shape
    merge = lambda ref: ref[...].reshape(nb * tm, ref.shape[-1])
    x = merge(x_ref)
    mod = mod_ref[...]

    def mod_rows(i):
        m = mod[:, :, i * d_model:(i + 1) * d_model]
        return m[0] if nb == 1 else jnp.broadcast_to(m, (nb, tm, d_model)).reshape(nb * tm, d_model)

    ga1, sh2, sc2, ga2 = mod_rows(2), mod_rows(3), mod_rows(4), mod_rows(5)
    ac = merge(ac_ref)
    mix = (_dot(ac[:, :A_WIDTH], wo_ref[0:A_WIDTH, :])
           + _dot(merge(bo_ref), wo_ref[A_WIDTH:A_WIDTH + B_WIDTH, :])
           + _dot(ac[:, A_WIDTH:], wo_ref[A_WIDTH + B_WIDTH:, :]))
    x1 = x + ga1 * mix
    h2 = ((_rms(x1) * g2_ref[...]) * (1.0 + sc2) + sh2).astype(BF16)
    d_ff = wg_ref.shape[1]
    acc = jnp.zeros_like(x1)
    for c0 in range(0, d_ff, ff_chunk):
        gate = _dot(h2, wg_ref[:, c0:c0 + ff_chunk])
        up = _dot(h2, wu_ref[:, c0:c0 + ff_chunk])
        act = (gate * jax.nn.sigmoid(gate)) * up
        acc = acc + _dot(act.astype(BF16), wd_ref[c0:c0 + ff_chunk, :])
    x2 = x1 + ga2 * acc
    out = _rms(x2) * gf_ref[...] if final else x2
    o_ref[...] = out.reshape(nb, tm, d_model)


def _ffn_call(x, ac, bo, mod, layer, mod_row0, g2, wo, wg, wu, wd, gf, tm, final):
    b, t, d = x.shape
    d_ff = wg.shape[2]
    ff_chunk = d_ff // 2 if (d_ff // 2) % LANES == 0 else d_ff
    nb = max(1, min(b, FFN_MIN_ROWS // tm))
    assert b % nb == 0 and mod_row0 % nb == 0
    kern = functools.partial(_ffn_kernel, d_model=d, ff_chunk=ff_chunk, final=final)
    row3 = lambda bi, j: (bi, j, 0)
    lay3 = lambda bi, j: (layer, 0, 0)
    once = pl.Buffered(1)
    return pl.pallas_call(
        kern,
        out_shape=jax.ShapeDtypeStruct((b, t, d), F32),
        grid=(b // nb, t // tm),
        in_specs=[
            pl.BlockSpec((nb, tm, d), row3),
            pl.BlockSpec((nb, tm, A_WIDTH + C_WIDTH), row3),
            pl.BlockSpec((nb, tm, B_WIDTH), row3),
            pl.BlockSpec((None, nb, 1, N_MOD * d), lambda bi, j: (layer, mod_row0 // nb + bi, 0, 0)),
            pl.BlockSpec((None, 1, d), lay3),
            pl.BlockSpec((None, A_WIDTH + B_WIDTH + C_WIDTH, d), lay3, pipeline_mode=once),
            pl.BlockSpec((None, d, d_ff), lay3, pipeline_mode=once),
            pl.BlockSpec((None, d, d_ff), lay3, pipeline_mode=once),
            pl.BlockSpec((None, d_ff, d), lay3, pipeline_mode=once),
            pl.BlockSpec((1, d), lambda bi, j: (0, 0)),
        ],
        out_specs=pl.BlockSpec((nb, tm, d), row3),
        compiler_params=pltpu.CompilerParams(
            dimension_semantics=("arbitrary", "arbitrary"), vmem_limit_bytes=VMEM_LIMIT),
        name="out_proj_ffn",
    )(x, ac, bo, mod, g2, wo, wg, wu, wd, gf)


def _rope_tables(start, t):
    half = HEAD_DIM // 2
    inv = jnp.power(ROPE_THETA, -2.0 * jnp.arange(half, dtype=F32) / HEAD_DIM)
    ang = (start + jnp.arange(t)).astype(F32)[:, None] * inv[None, :]
    cos = jnp.cos(ang)
    sin = jnp.sin(ang)
    zero = jnp.zeros_like(sin)
    reps = LANES // HEAD_DIM
    return (jnp.tile(jnp.concatenate([cos, cos], axis=-1), (1, reps)),
            jnp.tile(jnp.concatenate([-sin, zero], axis=-1), (1, reps)),
            jnp.tile(jnp.concatenate([zero, sin], axis=-1), (1, reps)))


def _pad_w_in(w_in):
    n_head = OFF_KI + IDX_DIM + IDX_HEADS
    depth, d, _ = w_in.shape
    pad = jnp.zeros((depth, d, OFF_C - n_head), w_in.dtype)
    return jnp.concatenate([w_in[:, :, :n_head], pad, w_in[:, :, n_head:]], axis=-1).astype(BF16)


def _trunk(x, mod, mod_row0, cache, params, q_start, tm, tq):
    (g_norm1, g_norm2, w_in_p, w_spatial, b_spatial, w_dw, b_dw, g_cnorm,
     wo, wg, wu, wd, g_final) = params
    b, t, d = x.shape
    depth = w_in_p.shape[0]
    ln = min(t, A_CHUNK)
    tabs = _rope_tables(q_start, t)
    wsp = w_spatial[:, :, :ln, :ln]
    bsp_tab = jnp.repeat(jnp.swapaxes(b_spatial[:, :, :ln], 1, 2), A_HEAD_DIM, axis=-1)
    kv_stack, convs, avs = None, [], []
    if cache is not None:
        past = cache[0].shape[2]
        cache_bf16 = (cache[0].astype(BF16).reshape(depth, b, past, B_KV_WIDTH),
                      cache[1].astype(BF16).reshape(depth, b, past, B_KV_WIDTH),
                      cache[2].astype(BF16))
    for l in range(depth):
        if cache is None:
            cst = jnp.zeros((b, CTX_ROWS, C_WIDTH), F32)
        else:
            cst = jnp.pad(cache[3][l], ((0, 0), (CTX_PAD, 0), (0, 0)))
        q, ks, vs, qi, kis, wi, ac, av, cnew = _in_call(
            x, mod, l, mod_row0, g_norm1, w_in_p, tabs, wsp, bsp_tab, w_dw, b_dw, g_cnorm, cst,
            kv_stack, tm)
        kv_stack = (ks, vs, kis)
        if cache is None:
            bo = _attn_prompt(q, qi, wi, ks, vs, kis, l, tq)
        else:
            bo = _attn_call(q, qi, wi, *cache_bf16, ks, vs, kis, l, q_start)
        x = _ffn_call(x, ac, bo, mod, l, mod_row0, g_norm2, wo, wg, wu, wd, g_final,
                      tm, final=(l == depth - 1))
        convs.append(cnew[:, CTX_PAD:, :])
        avs.append(av)
    ks, vs, kis = kv_stack
    return (x, ks.reshape(depth, b, t, B_KV_HEADS, HEAD_DIM), vs.reshape(depth, b, t, B_KV_HEADS, HEAD_DIM),
            kis, jnp.stack(convs), jnp.stack(avs))


def kernel(x_prompt, x_sample, cache_k, cache_v, cache_kidx, state_conv, c_prompt, c_sample,
           w_ada, b_ada, g_norm1, g_norm2, w_in, w_spatial, b_spatial, w_dw, b_dw, g_cnorm,
           w_out, w_gate, w_up, w_down, g_final):
    depth, d = g_norm1.shape
    nb_p, t_p, _ = x_prompt.shape
    nb_s, t_s, _ = x_sample.shape
    past = cache_k.shape[2]

    rows = nb_p + nb_s
    rows_pad = -(-rows // 8) * 8
    c_all = jnp.pad(jnp.concatenate([c_prompt, c_sample], axis=0), ((0, rows_pad - rows), (0, 0)))
    mod = _mod_call(c_all, w_ada, b_ada).reshape(depth, rows_pad, 1, N_MOD * d)

    params = (
        g_norm1.reshape(depth, 1, d), g_norm2.reshape(depth, 1, d), _pad_w_in(w_in),
        w_spatial, b_spatial, w_dw, b_dw.reshape(depth, 1, C_WIDTH), g_cnorm.reshape(depth, 1, C_WIDTH),
        w_out.astype(BF16), w_gate.astype(BF16), w_up.astype(BF16), w_down.astype(BF16),
        g_final.reshape(1, d),
    )

    y_p, p_k, p_v, p_ki, p_conv, _ = _trunk(
        x_prompt, mod, 0, None, params, 0, tm=min(t_p, 512), tq=min(t_p, 256))
    y_s, s_k, s_v, s_ki, s_conv, s_av = _trunk(
        x_sample, mod, nb_p, (cache_k, cache_v, cache_kidx, state_conv), params, past,
        tm=t_s, tq=t_s)
    return (y_p, y_s, p_k, p_v, p_ki, p_conv, s_k, s_v, s_ki, s_conv, s_av)
```

```python
import functools

import jax
import jax.numpy as jnp
from jax import lax
from jax.experimental import pallas as pl
from jax.experimental.pallas import tpu as pltpu

CHUNK = 64
CHUNK_SHIFT = 6
assert 1 << CHUNK_SHIFT == CHUNK
A_HEADS = 4
A_HEAD_DIM = 64
A_WIDTH = A_HEADS * A_HEAD_DIM
A_CHUNK = 128
B_HEADS = 8
B_KV_HEADS = 2
HEAD_DIM = 64
B_WIDTH = B_HEADS * HEAD_DIM
B_KV_WIDTH = B_KV_HEADS * HEAD_DIM
IDX_HEADS = 4
IDX_DIM = 64
TOPK_MAX = 256
ROPE_THETA = 10000.0
C_GROUPS = 4
C_WIDTH = 256
CONV_WIDTH = 31
N_MOD = 6
EPS = 1e-6

LANES = 128
SUBLANES = 8
BF16_SUBLANES = 16
CTX_ROWS = 32
CTX_PAD = CTX_ROWS - (CONV_WIDTH - 1)
VMEM_LIMIT = 56 * 1024 * 1024

OFF_A = 0
OFF_Q = OFF_A + 2 * A_WIDTH
OFF_K = OFF_Q + B_WIDTH
OFF_V = OFF_K + B_KV_WIDTH
OFF_QI = OFF_V + B_KV_WIDTH
OFF_KI = OFF_QI + IDX_HEADS * IDX_DIM
OFF_C = OFF_KI + LANES
N_IN_PAD = OFF_C + 2 * C_WIDTH

LOG2E = 1.4426950408889634
KEY_MIN_FINITE = -2139095040

F32 = jnp.float32
BF16 = jnp.bfloat16


def _dot(a, b):
    return jnp.dot(a, b, preferred_element_type=F32)


def _dot_t(a, b):
    return lax.dot_general(a, b, (((1,), (1,)), ((), ())), preferred_element_type=F32)


def _rms(x):
    return x * lax.rsqrt(jnp.mean(x * x, axis=-1, keepdims=True) + EPS)


def _lane_group(n, group):
    shift = group.bit_length() - 1
    assert 1 << shift == group
    return lax.shift_right_logical(lax.broadcasted_iota(jnp.int32, (1, n), 1), shift)


def _group_standardize(y, group):
    n = y.shape[-1]
    gid = _lane_group(n, group)
    inv = 1.0 / group
    mean = jnp.zeros_like(y)
    for g in range(n // group):
        m = gid == g
        s = jnp.sum(jnp.where(m, y, 0.0), axis=-1, keepdims=True) * inv
        mean = jnp.where(m, s, mean)
    yc = y - mean
    sq = yc * yc
    var = jnp.zeros_like(y)
    for g in range(n // group):
        m = gid == g
        s = jnp.sum(jnp.where(m, sq, 0.0), axis=-1, keepdims=True) * inv
        var = jnp.where(m, s, var)
    return yc * lax.rsqrt(var + EPS)


def _rope(x, cos, sin_lo, sin_hi):
    parts = []
    for c in range(x.shape[-1] // LANES):
        xc = x[:, c * LANES:(c + 1) * LANES]
        up = pltpu.roll(xc, LANES - HEAD_DIM // 2, 1)
        down = pltpu.roll(xc, HEAD_DIM // 2, 1)
        parts.append(xc * cos + up * sin_lo + down * sin_hi)
    return parts[0] if len(parts) == 1 else jnp.concatenate(parts, axis=-1)


def _mod_kernel(c_ref, w_ref, b_ref, o_ref):
    c = c_ref[...]
    cond = c * jax.nn.sigmoid(c)
    o_ref[0] = _dot(cond.astype(BF16), w_ref[0].astype(BF16)) + b_ref[0]


def _mod_call(c_all, w_ada, b_ada):
    depth, d, n = w_ada.shape
    rows = c_all.shape[0]
    tn = d
    return pl.pallas_call(
        _mod_kernel,
        out_shape=jax.ShapeDtypeStruct((depth, rows, n), F32),
        grid=(depth, n // tn),
        in_specs=[
            pl.BlockSpec((rows, d), lambda l, j: (0, 0)),
            pl.BlockSpec((1, d, tn), lambda l, j: (l, 0, j)),
            pl.BlockSpec((1, 1, tn), lambda l, j: (l, 0, j)),
        ],
        out_specs=pl.BlockSpec((1, rows, tn), lambda l, j: (l, 0, j)),
        compiler_params=pltpu.CompilerParams(
            dimension_semantics=("arbitrary", "arbitrary"), vmem_limit_bytes=VMEM_LIMIT),
        name="adaln_mod",
    )(c_all, w_ada, b_ada.reshape(depth, 1, n))


N_IN_OPERANDS = 13


def _in_kernel(*refs, tm, ln, d_model, n_aliased):
    (x_ref, mod_ref, g1_ref, w_ref, cos_ref, slo_ref, shi_ref, wsp_ref, bsp_ref,
     wdw_ref, bdw_ref, gcn_ref, cst_ref) = refs[:N_IN_OPERANDS]
    (q_ref, k_ref, v_ref, qi_ref, ki_ref, wi_ref, ac_ref, av_ref, cnew_ref,
     hp_ref, sh_ref) = refs[N_IN_OPERANDS + n_aliased:]
    j = pl.program_id(1)
    x = x_ref[0]
    mod = mod_ref[0]
    sh1 = mod[:, 0:d_model]
    sc1 = mod[:, d_model:2 * d_model]
    h = (_rms(x) * g1_ref[...]) * (1.0 + sc1) + sh1
    z = _dot(h.astype(BF16), w_ref[...])
    part = lambda off, width: z[:, off:off + width]

    cos = cos_ref[...]
    slo = slo_ref[...]
    shi = shi_ref[...]
    q = _rope(part(OFF_Q, B_WIDTH), cos, slo, shi)
    q_ref[0] = (q * (LOG2E * HEAD_DIM ** -0.5)).astype(BF16)
    k_ref[0] = _rope(part(OFF_K, B_KV_WIDTH), cos, slo, shi)
    v_ref[0] = part(OFF_V, B_KV_WIDTH)
    qi_ref[0] = _rope(part(OFF_QI, IDX_HEADS * IDX_DIM), cos, slo, shi).astype(BF16)
    kiwi = part(OFF_KI, LANES)
    ki_ref[0] = _rope(kiwi, cos, slo, shi)[:, 0:IDX_DIM]
    wi_ref[0] = kiwi

    za = part(OFF_A, 2 * A_WIDTH)
    za = 0.5 * za * (1.0 + lax.erf(za * (2.0 ** -0.5)))
    u = za[:, 0:A_WIDTH]
    vn = _group_standardize(za[:, A_WIDTH:2 * A_WIDTH], A_HEAD_DIM)
    av_ref[0] = vn
    vb = vn.astype(BF16)
    row = lax.broadcasted_iota(jnp.int32, (ln, ln), 0)
    col = lax.broadcasted_iota(jnp.int32, (ln, ln), 1)
    head_of_lane = _lane_group(A_WIDTH, A_HEAD_DIM)
    wsp = [jnp.where(col <= row, wsp_ref[g], 0.0).astype(BF16) for g in range(A_HEADS)]
    gated = []
    for c in range(tm // ln):
        vc = vb[c * ln:(c + 1) * ln, :]
        s = jnp.zeros((ln, A_WIDTH), F32)
        for g in range(A_HEADS):
            s = jnp.where(head_of_lane == g, _dot(wsp[g], vc), s)
        gated.append(u[c * ln:(c + 1) * ln, :] * (s + bsp_ref[...]))
    a_out = gated[0] if len(gated) == 1 else jnp.concatenate(gated, axis=0)
    ac_ref[0, :, 0:A_WIDTH] = a_out.astype(BF16)

    zc = part(OFF_C, 2 * C_WIDTH)
    hc = zc[:, 0:C_WIDTH] * jax.nn.sigmoid(zc[:, C_WIDTH:2 * C_WIDTH])

    @pl.when(j == 0)
    def _():
        hp_ref[0:CTX_ROWS, :] = cst_ref[0]

    hp_ref[CTX_ROWS:CTX_ROWS + tm, :] = hc
    span = tm + CTX_ROWS - SUBLANES
    for s in range(1, SUBLANES):
        sh_ref[s - 1, 0:span, :] = hp_ref[s:s + span, :]
    rb = min(tm, 64)
    wdw = wdw_ref[...]
    conv = []
    for r0 in range(0, tm, rb):
        acc = jnp.zeros((rb, C_WIDTH), F32)
        for t in range(CONV_WIDTH):
            phase = (CTX_PAD + t) % SUBLANES
            base = CTX_PAD + t - phase + r0
            rows = hp_ref[base:base + rb, :] if phase == 0 else sh_ref[phase - 1, base:base + rb, :]
            acc = acc + rows * wdw[t:t + 1, :]
        conv.append(acc)
    y = (conv[0] if len(conv) == 1 else jnp.concatenate(conv, axis=0)) + bdw_ref[...]
    y = _group_standardize(y, C_WIDTH // C_GROUPS) * gcn_ref[...]
    ac_ref[0, :, A_WIDTH:A_WIDTH + C_WIDTH] = (y * jax.nn.sigmoid(y)).astype(BF16)

    tail = hp_ref[tm:tm + CTX_ROWS, :]
    cnew_ref[0] = tail
    hp_ref[0:CTX_ROWS, :] = tail


def _in_call(x, mod, layer, mod_row0, g1, w_in_p, tabs, wsp, bsp_tab, wdw, bdw, gcn, cst, kv_stack, tm):
    b, t, d = x.shape
    depth = w_in_p.shape[0]
    ln = min(t, A_CHUNK)
    cos, slo, shi = tabs
    n_aliased = 0 if kv_stack is None else len(kv_stack)
    kern = functools.partial(_in_kernel, tm=tm, ln=ln, d_model=d, n_aliased=n_aliased)
    row3 = lambda bi, j: (bi, j, 0)
    lay_row4 = lambda bi, j: (layer, bi, j, 0)
    lay3 = lambda bi, j: (layer, 0, 0)
    out_shape = (
        jax.ShapeDtypeStruct((b, t, B_WIDTH), BF16),
        jax.ShapeDtypeStruct((depth, b, t, B_KV_WIDTH), F32),
        jax.ShapeDtypeStruct((depth, b, t, B_KV_WIDTH), F32),
        jax.ShapeDtypeStruct((b, t, IDX_HEADS * IDX_DIM), BF16),
        jax.ShapeDtypeStruct((depth, b, t, IDX_DIM), F32),
        jax.ShapeDtypeStruct((b, t, LANES), F32),
        jax.ShapeDtypeStruct((b, t, A_WIDTH + C_WIDTH), BF16),
        jax.ShapeDtypeStruct((b, t, A_WIDTH), F32),
        jax.ShapeDtypeStruct((b, CTX_ROWS, C_WIDTH), F32),
    )
    out_specs = (
        pl.BlockSpec((1, tm, B_WIDTH), row3),
        pl.BlockSpec((None, 1, tm, B_KV_WIDTH), lay_row4),
        pl.BlockSpec((None, 1, tm, B_KV_WIDTH), lay_row4),
        pl.BlockSpec((1, tm, IDX_HEADS * IDX_DIM), row3),
        pl.BlockSpec((None, 1, tm, IDX_DIM), lay_row4),
        pl.BlockSpec((1, tm, LANES), row3),
        pl.BlockSpec((1, tm, A_WIDTH + C_WIDTH), row3),
        pl.BlockSpec((1, tm, A_WIDTH), row3),
        pl.BlockSpec((1, CTX_ROWS, C_WIDTH), lambda bi, j: (bi, 0, 0)),
    )
    kv_outputs = (1, 2, 4)
    in_specs = [
        pl.BlockSpec((1, tm, d), row3),
        pl.BlockSpec((None, 1, 1, N_MOD * d), lambda bi, j: (layer, mod_row0 + bi, 0, 0)),
        pl.BlockSpec((None, 1, d), lay3),
        pl.BlockSpec((None, d, N_IN_PAD), lay3),
        pl.BlockSpec((tm, LANES), lambda bi, j: (j, 0)),
        pl.BlockSpec((tm, LANES), lambda bi, j: (j, 0)),
        pl.BlockSpec((tm, LANES), lambda bi, j: (j, 0)),
        pl.BlockSpec((None, A_HEADS, ln, ln), lambda bi, j: (layer, 0, 0, 0)),
        pl.BlockSpec((None, ln, A_WIDTH), lay3),
        pl.BlockSpec((None, CONV_WIDTH, C_WIDTH), lay3),
        pl.BlockSpec((None, 1, C_WIDTH), lay3),
        pl.BlockSpec((None, 1, C_WIDTH), lay3),
        pl.BlockSpec((1, CTX_ROWS, C_WIDTH), lambda bi, j: (bi, 0, 0)),
    ]
    args = [x, mod, g1, w_in_p, cos, slo, shi, wsp, bsp_tab, wdw, bdw, gcn, cst]
    assert len(args) == N_IN_OPERANDS
    aliases = {}
    if kv_stack is not None:
        for arr, out_pos in zip(kv_stack, kv_outputs):
            aliases[len(args)] = out_pos
            in_specs.append(pl.BlockSpec(memory_space=pl.ANY))
            args.append(arr)
    return pl.pallas_call(
        kern,
        out_shape=out_shape,
        grid=(b, t // tm),
        in_specs=in_specs,
        out_specs=out_specs,
        scratch_shapes=[pltpu.VMEM((tm + CTX_ROWS, C_WIDTH), F32),
                        pltpu.VMEM((SUBLANES - 1, tm + CTX_ROWS, C_WIDTH), F32)],
        input_output_aliases=aliases,
        compiler_params=pltpu.CompilerParams(
            dimension_semantics=("arbitrary", "arbitrary"), vmem_limit_bytes=VMEM_LIMIT),
        name="in_proj_mixers_ac",
    )(*args)


def _attn_kernel(q_ref, qi_ref, wi_ref, ck_ref, cv_ref, cki_ref, k_ref, v_ref, ki_ref, o_ref,
                 kb_ref, vb_ref, kib_ref, key_ref, bias_ref, *, tq, q_start, topk):
    past = ck_ref.shape[1]
    n_keys = past + k_ref.shape[1]
    lk = kb_ref.shape[0]
    for dst, old, new in ((kb_ref, ck_ref, k_ref), (vb_ref, cv_ref, v_ref), (kib_ref, cki_ref, ki_ref)):
        dst[0:past, :] = old[0]
        dst[past:n_keys, :] = new[0].astype(BF16)
        dst[n_keys:lk, :] = jnp.zeros((lk - n_keys, dst.shape[1]), BF16)
    j = pl.program_id(1)
    qchunk = lax.shift_right_logical(
        q_start + j * tq + lax.broadcasted_iota(jnp.int32, (tq, 1), 0), CHUNK_SHIFT)

    def allowed_at(k0, width):
        kpos = k0 + lax.broadcasted_iota(jnp.int32, (1, width), 1)
        return (lax.shift_right_logical(kpos, CHUNK_SHIFT) <= qchunk) & (kpos < n_keys)

    kib = kib_ref[...]
    qi = qi_ref[0]
    wi = wi_ref[0]
    score = jnp.zeros((tq, lk), F32)
    for h in range(IDX_HEADS):
        d = _dot_t(qi[:, h * IDX_DIM:(h + 1) * IDX_DIM], kib)
        score = score + wi[:, IDX_DIM + h:IDX_DIM + h + 1] * jnp.maximum(d, 0.0)
    score = jnp.where(score == 0.0, 0.0, score)
    score = jnp.where(allowed_at(0, lk), score, -jnp.inf)

    bits = pltpu.bitcast(score, jnp.int32)
    key_ref[...] = bits ^ ((bits >> 31) & jnp.int32(0x7FFFFFFF))
    int_min = jnp.int32(-2 ** 31)

    def digit_step(i, thr):
        unit = lax.shift_left(jnp.int32(1), jnp.int32(30) - 2 * i)
        key = key_ref[...]
        for mult in (1, 2, 3):
            cand = thr + mult * unit if mult == 1 else cand + unit
            cnt = jnp.sum((key >= cand).astype(F32), axis=-1, keepdims=True)
            best = jnp.where(cnt >= topk, cand, thr if mult == 1 else best)
        return best

    thr = lax.fori_loop(0, 16, digit_step, jnp.full((tq, 1), int_min, jnp.int32))

    need = topk - jnp.sum((key_ref[...] > thr).astype(F32), axis=-1, keepdims=True)
    tri = (lax.broadcasted_iota(jnp.int32, (LANES, LANES), 0)
           <= lax.broadcasted_iota(jnp.int32, (LANES, LANES), 1)).astype(F32).astype(BF16)
    seen = jnp.zeros((tq, 1), F32)
    for c in range(lk // LANES):
        sl = slice(c * LANES, (c + 1) * LANES)
        key_c = key_ref[:, sl]
        eq_c = key_c == thr
        eq_f = eq_c.astype(F32)
        rank = _dot(eq_f.astype(BF16), tri) + seen
        chosen = ((key_c > thr) | (eq_c & (rank <= need))) & allowed_at(c * LANES, LANES)
        bias_ref[:, sl] = jnp.where(chosen, 0.0, -jnp.inf)
        seen = seen + jnp.sum(eq_f, axis=-1, keepdims=True)

    q = q_ref[0]
    group = B_HEADS // B_KV_HEADS
    bias = jnp.concatenate([bias_ref[...]] * group, axis=0)
    outs = []
    for n in range(B_KV_HEADS):
        kn = kb_ref[:, n * HEAD_DIM:(n + 1) * HEAD_DIM]
        vn = vb_ref[:, n * HEAD_DIM:(n + 1) * HEAD_DIM]
        qg = jnp.concatenate([q[:, (n * group + g) * HEAD_DIM:(n * group + g + 1) * HEAD_DIM]
                              for g in range(group)], axis=0)
        logits = _dot_t(qg, kn) + bias
        p = jnp.exp2(logits - jnp.max(logits, axis=-1, keepdims=True))
        den = jnp.sum(p, axis=-1, keepdims=True)
        og = _dot(p.astype(BF16), vn) / den
        outs.extend(og[g * tq:(g + 1) * tq] for g in range(group))
    o_ref[0] = jnp.concatenate(outs, axis=-1).astype(BF16)


def _attn_call(q, qi, wi, cache_k, cache_v, cache_ki, k, v, ki, layer, q_start):
    b, t, _ = q.shape
    past = cache_k.shape[2]
    assert past % BF16_SUBLANES == 0 and t % BF16_SUBLANES == 0
    n_keys = past + t
    lk = -(-n_keys // LANES) * LANES
    topk = min(TOPK_MAX, n_keys // 4)
    kern = functools.partial(_attn_kernel, tq=t, q_start=q_start, topk=topk)
    row3 = lambda bi, j: (bi, 0, 0)
    lay_bat4 = lambda bi, j: (layer, bi, 0, 0)
    return pl.pallas_call(
        kern,
        out_shape=jax.ShapeDtypeStruct((b, t, B_WIDTH), BF16),
        grid=(b, 1),
        in_specs=[
            pl.BlockSpec((1, t, B_WIDTH), row3),
            pl.BlockSpec((1, t, IDX_HEADS * IDX_DIM), row3),
            pl.BlockSpec((1, t, LANES), row3),
            pl.BlockSpec((None, 1, past, B_KV_WIDTH), lay_bat4),
            pl.BlockSpec((None, 1, past, B_KV_WIDTH), lay_bat4),
            pl.BlockSpec((None, 1, past, IDX_DIM), lay_bat4),
            pl.BlockSpec((None, 1, t, B_KV_WIDTH), lay_bat4),
            pl.BlockSpec((None, 1, t, B_KV_WIDTH), lay_bat4),
            pl.BlockSpec((None, 1, t, IDX_DIM), lay_bat4),
        ],
        out_specs=pl.BlockSpec((1, t, B_WIDTH), row3),
        scratch_shapes=[pltpu.VMEM((lk, B_KV_WIDTH), BF16), pltpu.VMEM((lk, B_KV_WIDTH), BF16),
                        pltpu.VMEM((lk, IDX_DIM), BF16),
                        pltpu.VMEM((t, lk), jnp.int32), pltpu.VMEM((t, lk), F32)],
        compiler_params=pltpu.CompilerParams(
            dimension_semantics=("arbitrary", "arbitrary"), vmem_limit_bytes=VMEM_LIMIT),
        name="dsa_attention",
    )(q, qi, wi, cache_k, cache_v, cache_ki, k, v, ki)


FOLD_ROWS = 64


def _col_reduce(x, op):
    r = x.shape[0]
    if r > FOLD_ROWS and r % FOLD_ROWS == 0:
        acc = x[:FOLD_ROWS]
        for c in range(1, r // FOLD_ROWS):
            acc = op(acc, x[c * FOLD_ROWS:(c + 1) * FOLD_ROWS])
        x, r = acc, FOLD_ROWS
    while r % 16 == 0:
        x = op(x[:r // 2], x[r // 2:])
        r //= 2
    return (jnp.sum if op is jnp.add else jnp.max)(x, axis=0, keepdims=True)


WORD_BITS = 32
INT_MIN = -2 ** 31
QK_ROWS = 256
FFN_MIN_ROWS = 256
MXU_COLS = 256


def _bit_transpose(words):
    a = list(words)
    j, m = WORD_BITS // 2, 0x0000FFFF
    while j:
        k = 0
        while k < WORD_BITS:
            t = (a[k] ^ lax.shift_right_logical(a[k + j], j)) & m
            a[k] = a[k] ^ t
            a[k + j] = a[k + j] ^ lax.shift_left(t, j)
            k = (k + j + 1) & ~j
        j >>= 1
        m = (m ^ (m << j)) & 0x7FFFFFFF if j else m
    return a


def _select_threshold(key_ref, planes_ref, topk, tq, side_work=()):
    lk = key_ref.shape[0]
    slab = lk // WORD_BITS
    for r0 in range(0, slab, SUBLANES):
        words = [key_ref[i * slab + r0:i * slab + r0 + SUBLANES, :] for i in range(WORD_BITS)]
        for b, plane in enumerate(_bit_transpose(words)):
            planes_ref[b, r0:r0 + SUBLANES, :] = plane
    count = lambda words: _col_reduce(lax.population_count(words), jnp.add)
    alive = jnp.full((slab, tq), -1, jnp.int32)
    left = jnp.full((1, tq), topk, jnp.int32)
    thr = jnp.zeros((1, tq), jnp.int32)
    steps = WORD_BITS // 2
    for step, hi_bit in enumerate(range(WORD_BITS - 1, 0, -2)):
        for piece in side_work[step * len(side_work) // steps:(step + 1) * len(side_work) // steps]:
            piece()
        p_hi = planes_ref[WORD_BITS - 1 - hi_bit]
        p_lo = planes_ref[WORD_BITS - hi_bit]
        if hi_bit == WORD_BITS - 1:
            p_hi = ~p_hi
        u1 = alive & p_hi
        u0 = alive ^ u1
        c11 = u1 & p_lo
        c10 = u1 ^ c11
        c01 = u0 & p_lo
        c00 = u0 ^ c01
        n11 = count(c11)
        n1x = n11 + count(c10)
        n_1 = n1x + count(c01)
        t11 = n11 >= left
        t1x = n1x >= left
        t_1 = n_1 >= left
        alive = jnp.where(t11, c11, jnp.where(t1x, c10, jnp.where(t_1, c01, c00)))
        left = left - jnp.where(t11, 0, jnp.where(t1x, n11, jnp.where(t_1, n1x, n_1)))
        hi_val = jnp.int32(1 << hi_bit if hi_bit < WORD_BITS - 1 else INT_MIN)
        lo_val = jnp.int32(1 << (hi_bit - 1))
        thr = thr | jnp.where(t1x, hi_val, 0) | jnp.where(t11 | (t_1 & ~t1x), lo_val, 0)
    return thr ^ INT_MIN, left


def _attn_t_kernel(*refs, tq, n_keys, q_start, q_block0, topk, aliased):
    q_ref, qi_ref, wi_ref, k_ref, v_ref, ki_ref = refs[:6]
    o_ref, key_ref, bias_ref, planes_ref, logit_ref = refs[7:] if aliased else refs[6:]
    lk = k_ref.shape[1]
    j = pl.program_id(1)
    qchunk = lax.shift_right_logical(
        q_start + (q_block0 + j) * tq + lax.broadcasted_iota(jnp.int32, (1, tq), 1), CHUNK_SHIFT)

    def allowed_at(k0, rows):
        kpos = k0 + lax.broadcasted_iota(jnp.int32, (rows, 1), 0)
        return (lax.shift_right_logical(kpos, CHUNK_SHIFT) <= qchunk) & (kpos < n_keys)

    kib = ki_ref[0].astype(BF16)
    qi = qi_ref[0]
    wi_t = wi_ref[0].T
    score = jnp.zeros((lk, tq), F32)
    for h in range(IDX_HEADS):
        d = _dot_t(kib, qi[:, h * IDX_DIM:(h + 1) * IDX_DIM])
        score = score + wi_t[IDX_DIM + h:IDX_DIM + h + 1, :] * jnp.maximum(d, 0.0)
    score = jnp.where(score == 0.0, 0.0, score)
    free = min(((q_start + q_block0 * tq) // CHUNK + 1) * CHUNK, n_keys, lk)
    if free < lk:
        masked = jnp.where(allowed_at(free, lk - free), score[free:], -jnp.inf)
        score = jnp.concatenate([score[:free], masked], axis=0) if free else masked

    bits = pltpu.bitcast(score, jnp.int32)
    key_ref[...] = bits ^ ((bits >> 31) & jnp.int32(0x7FFFFFFF))

    q = q_ref[0]
    group = B_HEADS // B_KV_HEADS
    qgs = [jnp.concatenate([q[:, (n * group + g) * HEAD_DIM:(n * group + g + 1) * HEAD_DIM]
                            for g in range(group)], axis=0) for n in range(B_KV_HEADS)]

    def logits_block(r0, n):
        kn = k_ref[0, r0:r0 + QK_ROWS, n * HEAD_DIM:(n + 1) * HEAD_DIM].astype(BF16)
        logit_ref[n, r0:r0 + QK_ROWS, :] = _dot_t(kn, qgs[n])

    side_work = [functools.partial(logits_block, r0, n)
                 for r0 in range(0, lk, QK_ROWS) for n in range(B_KV_HEADS)]
    thr, left = _select_threshold(key_ref, planes_ref, topk, tq, side_work)

    clamped = thr < KEY_MIN_FINITE
    thr = jnp.where(clamped, KEY_MIN_FINITE, thr)
    need = jnp.where(clamped, topk, left).astype(F32)
    tril = (lax.broadcasted_iota(jnp.int32, (LANES, LANES), 0)
            >= lax.broadcasted_iota(jnp.int32, (LANES, LANES), 1)).astype(F32).astype(BF16)
    seen = jnp.zeros((1, tq), F32)
    for c in range(lk // LANES):
        sl = slice(c * LANES, (c + 1) * LANES)
        key_c = key_ref[sl, :]
        eq_c = key_c == thr
        eq_f = eq_c.astype(F32)
        rank = _dot(tril, eq_f.astype(BF16)) + seen
        chosen = (key_c > thr) | (eq_c & (rank <= need))
        bias_ref[sl, :] = jnp.where(chosen, 0.0, -jnp.inf)
        seen = seen + _col_reduce(eq_f, jnp.add)

    vt = v_ref[0].T.astype(BF16)
    bias = bias_ref[...]
    ones_rows = jnp.ones((BF16_SUBLANES, lk), BF16)
    outs = []
    for n in range(B_KV_HEADS):
        vtn = jnp.concatenate([vt[n * HEAD_DIM:(n + 1) * HEAD_DIM, :], ones_rows], axis=0)
        ps = []
        for g in range(group):
            lg = logit_ref[n, :, g * tq:(g + 1) * tq] + bias
            ps.append(jnp.exp2(lg - _col_reduce(lg, jnp.maximum)).astype(BF16))
        og = _dot(vtn, jnp.concatenate(ps, axis=1))
        og = og[:HEAD_DIM] / og[HEAD_DIM:HEAD_DIM + 1]
        outs.extend(og[:, g * tq:(g + 1) * tq] for g in range(group))
    o_ref[0] = jnp.concatenate(outs, axis=0).T.astype(BF16)


def _attn_t_call(q, qi, wi, k, v, ki, prev, layer, n_keys, q_start, q_block0, n_q_blocks, lk, tq):
    b, t, _ = q.shape
    topk = min(TOPK_MAX, n_keys // 4)
    aliased = prev is not None
    kern = functools.partial(_attn_t_kernel, tq=tq, n_keys=n_keys, q_start=q_start,
                             q_block0=q_block0, topk=topk, aliased=aliased)
    row3 = lambda bi, j: (bi, q_block0 + j, 0)
    lay_bat4 = lambda bi, j: (layer, bi, 0, 0)
    in_specs = [
        pl.BlockSpec((1, tq, B_WIDTH), row3),
        pl.BlockSpec((1, tq, IDX_HEADS * IDX_DIM), row3),
        pl.BlockSpec((1, tq, LANES), row3),
        pl.BlockSpec((None, 1, lk, B_KV_WIDTH), lay_bat4),
        pl.BlockSpec((None, 1, lk, B_KV_WIDTH), lay_bat4),
        pl.BlockSpec((None, 1, lk, IDX_DIM), lay_bat4),
    ]
    args = [q, qi, wi, k, v, ki]
    if aliased:
        in_specs.append(pl.BlockSpec(memory_space=pl.ANY))
        args.append(prev)
    return pl.pallas_call(
        kern,
        out_shape=jax.ShapeDtypeStruct((b, t, B_WIDTH), BF16),
        grid=(b, n_q_blocks),
        in_specs=in_specs,
        out_specs=pl.BlockSpec((1, tq, B_WIDTH), row3),
        scratch_shapes=[pltpu.VMEM((lk, tq), jnp.int32), pltpu.VMEM((lk, tq), F32),
                        pltpu.VMEM((WORD_BITS, lk // WORD_BITS, tq), jnp.int32),
                        pltpu.VMEM((B_KV_HEADS, lk, B_HEADS // B_KV_HEADS * tq), F32)],
        input_output_aliases={6: 0} if aliased else {},
        compiler_params=pltpu.CompilerParams(
            dimension_semantics=("arbitrary", "arbitrary"), vmem_limit_bytes=VMEM_LIMIT),
        name="dsa_attention_t",
    )(*args)


def _attn_prompt(q, qi, wi, k, v, ki, layer, tq):
    t = q.shape[1]
    nqb = t // tq
    per_seg = max(1, WORD_BITS * SUBLANES // tq)
    assert nqb % per_seg == 0 and (per_seg * tq) % (WORD_BITS * SUBLANES) == 0
    bo = None
    for s in range(nqb // per_seg):
        bo = _attn_t_call(q, qi, wi, k, v, ki, bo, layer, t, 0, s * per_seg, per_seg,
                          (s + 1) * per_seg * tq, tq)
    return bo


def _ffn_kernel(x_ref, ac_ref, bo_ref, mod_ref, g2_ref, wo_ref, wg_ref, wu_ref, wd_ref,
                gf_ref, o_ref, *, d_model, ff_chunk, final):
    nb, tm, _ = x_ref.shape
    merge = lambda ref: ref[...].reshape(nb * tm, ref.shape[-1])
    x = merge(x_ref)
    mod = mod_ref[...]

    def mod_rows(i):
        m = mod[:, :, i * d_model:(i + 1) * d_model]
        return m[0] if nb == 1 else jnp.broadcast_to(m, (nb, tm, d_model)).reshape(nb * tm, d_model)

    ga1, sh2, sc2, ga2 = mod_rows(2), mod_rows(3), mod_rows(4), mod_rows(5)
    ac = merge(ac_ref)
    mix = (_dot(ac[:, :A_WIDTH], wo_ref[0:A_WIDTH, :])
           + _dot(merge(bo_ref), wo_ref[A_WIDTH:A_WIDTH + B_WIDTH, :])
           + _dot(ac[:, A_WIDTH:], wo_ref[A_WIDTH + B_WIDTH:, :]))
    x1 = x + ga1 * mix
    h2 = ((_rms(x1) * g2_ref[...]) * (1.0 + sc2) + sh2).astype(BF16)
    d_ff = wg_ref.shape[1]
    acc = jnp.zeros_like(x1)
    for c0 in range(0, d_ff, ff_chunk):
        gate = _dot(h2, wg_ref[:, c0:c0 + ff_chunk])
        up = _dot(h2, wu_ref[:, c0:c0 + ff_chunk])
        act = (gate * jax.nn.sigmoid(gate)) * up
        acc = acc + _dot(act.astype(BF16), wd_ref[c0:c0 + ff_chunk, :])
    x2 = x1 + ga2 * acc
    out = _rms(x2) * gf_ref[...] if final else x2
    o_ref[...] = out.reshape(nb, tm, d_model)


def _ffn_call(x, ac, bo, mod, layer, mod_row0, g2, wo, wg, wu, wd, gf, tm, final):
    b, t, d = x.shape
    d_ff = wg.shape[2]
    ff_chunk = MXU_COLS if d_ff % MXU_COLS == 0 else d_ff
    nb = max(1, min(b, FFN_MIN_ROWS // tm))
    assert b % nb == 0 and mod_row0 % nb == 0
    kern = functools.partial(_ffn_kernel, d_model=d, ff_chunk=ff_chunk, final=final)
    row3 = lambda bi, j: (bi, j, 0)
    lay3 = lambda bi, j: (layer, 0, 0)
    once = pl.Buffered(1)
    return pl.pallas_call(
        kern,
        out_shape=jax.ShapeDtypeStruct((b, t, d), F32),
        grid=(b // nb, t // tm),
        in_specs=[
            pl.BlockSpec((nb, tm, d), row3),
            pl.BlockSpec((nb, tm, A_WIDTH + C_WIDTH), row3),
            pl.BlockSpec((nb, tm, B_WIDTH), row3),
            pl.BlockSpec((None, nb, 1, N_MOD * d), lambda bi, j: (layer, mod_row0 // nb + bi, 0, 0)),
            pl.BlockSpec((None, 1, d), lay3),
            pl.BlockSpec((None, A_WIDTH + B_WIDTH + C_WIDTH, d), lay3, pipeline_mode=once),
            pl.BlockSpec((None, d, d_ff), lay3, pipeline_mode=once),
            pl.BlockSpec((None, d, d_ff), lay3, pipeline_mode=once),
            pl.BlockSpec((None, d_ff, d), lay3, pipeline_mode=once),
            pl.BlockSpec((1, d), lambda bi, j: (0, 0)),
        ],
        out_specs=pl.BlockSpec((nb, tm, d), row3),
        compiler_params=pltpu.CompilerParams(
            dimension_semantics=("arbitrary", "arbitrary"), vmem_limit_bytes=VMEM_LIMIT),
        name="out_proj_ffn",
    )(x, ac, bo, mod, g2, wo, wg, wu, wd, gf)


def _rope_tables(start, t):
    half = HEAD_DIM // 2
    inv = jnp.power(ROPE_THETA, -2.0 * jnp.arange(half, dtype=F32) / HEAD_DIM)
    ang = (start + jnp.arange(t)).astype(F32)[:, None] * inv[None, :]
    cos = jnp.cos(ang)
    sin = jnp.sin(ang)
    zero = jnp.zeros_like(sin)
    reps = LANES // HEAD_DIM
    return (jnp.tile(jnp.concatenate([cos, cos], axis=-1), (1, reps)),
            jnp.tile(jnp.concatenate([-sin, zero], axis=-1), (1, reps)),
            jnp.tile(jnp.concatenate([zero, sin], axis=-1), (1, reps)))


def _pad_w_in(w_in):
    n_head = OFF_KI + IDX_DIM + IDX_HEADS
    depth, d, _ = w_in.shape
    pad = jnp.zeros((depth, d, OFF_C - n_head), w_in.dtype)
    return jnp.concatenate([w_in[:, :, :n_head], pad, w_in[:, :, n_head:]], axis=-1).astype(BF16)


def _trunk(x, mod, mod_row0, cache, params, q_start, tm, tq):
    (g_norm1, g_norm2, w_in_p, w_spatial, b_spatial, w_dw, b_dw, g_cnorm,
     wo, wg, wu, wd, g_final) = params
    b, t, d = x.shape
    depth = w_in_p.shape[0]
    ln = min(t, A_CHUNK)
    tabs = _rope_tables(q_start, t)
    wsp = w_spatial[:, :, :ln, :ln]
    bsp_tab = jnp.repeat(jnp.swapaxes(b_spatial[:, :, :ln], 1, 2), A_HEAD_DIM, axis=-1)
    kv_stack, convs, avs = None, [], []
    if cache is not None:
        past = cache[0].shape[2]
        cache_bf16 = (cache[0].astype(BF16).reshape(depth, b, past, B_KV_WIDTH),
                      cache[1].astype(BF16).reshape(depth, b, past, B_KV_WIDTH),
                      cache[2].astype(BF16))
    for l in range(depth):
        if cache is None:
            cst = jnp.zeros((b, CTX_ROWS, C_WIDTH), F32)
        else:
            cst = jnp.pad(cache[3][l], ((0, 0), (CTX_PAD, 0), (0, 0)))
        q, ks, vs, qi, kis, wi, ac, av, cnew = _in_call(
            x, mod, l, mod_row0, g_norm1, w_in_p, tabs, wsp, bsp_tab, w_dw, b_dw, g_cnorm, cst,
            kv_stack, tm)
        kv_stack = (ks, vs, kis)
        if cache is None:
            bo = _attn_prompt(q, qi, wi, ks, vs, kis, l, tq)
        else:
            bo = _attn_call(q, qi, wi, *cache_bf16, ks, vs, kis, l, q_start)
        x = _ffn_call(x, ac, bo, mod, l, mod_row0, g_norm2, wo, wg, wu, wd, g_final,
                      tm, final=(l == depth - 1))
        convs.append(cnew[:, CTX_PAD:, :])
        avs.append(av)
    ks, vs, kis = kv_stack
    return (x, ks.reshape(depth, b, t, B_KV_HEADS, HEAD_DIM), vs.reshape(depth, b, t, B_KV_HEADS, HEAD_DIM),
            kis, jnp.stack(convs), jnp.stack(avs))


def kernel(x_prompt, x_sample, cache_k, cache_v, cache_kidx, state_conv, c_prompt, c_sample,
           w_ada, b_ada, g_norm1, g_norm2, w_in, w_spatial, b_spatial, w_dw, b_dw, g_cnorm,
           w_out, w_gate, w_up, w_down, g_final):
    depth, d = g_norm1.shape
    nb_p, t_p, _ = x_prompt.shape
    nb_s, t_s, _ = x_sample.shape
    past = cache_k.shape[2]

    rows = nb_p + nb_s
    rows_pad = -(-rows // 8) * 8
    c_all = jnp.pad(jnp.concatenate([c_prompt, c_sample], axis=0), ((0, rows_pad - rows), (0, 0)))
    mod = _mod_call(c_all, w_ada, b_ada).reshape(depth, rows_pad, 1, N_MOD * d)

    params = (
        g_norm1.reshape(depth, 1, d), g_norm2.reshape(depth, 1, d), _pad_w_in(w_in),
        w_spatial, b_spatial, w_dw, b_dw.reshape(depth, 1, C_WIDTH), g_cnorm.reshape(depth, 1, C_WIDTH),
        w_out.astype(BF16), w_gate.astype(BF16), w_up.astype(BF16), w_down.astype(BF16),
        g_final.reshape(1, d),
    )

    y_p, p_k, p_v, p_ki, p_conv, _ = _trunk(
        x_prompt, mod, 0, None, params, 0, tm=min(t_p, 512), tq=min(t_p, 256))
    y_s, s_k, s_v, s_ki, s_conv, s_av = _trunk(
        x_sample, mod, nb_p, (cache_k, cache_v, cache_kidx, state_conv), params, past,
        tm=t_s, tq=t_s)
    return (y_p, y_s, p_k, p_v, p_ki, p_conv, s_k, s_v, s_ki, s_conv, s_av)
```

```python
import functools

import jax
import jax.numpy as jnp
from jax import lax
from jax.experimental import pallas as pl
from jax.experimental.pallas import tpu as pltpu

CHUNK = 64
CHUNK_SHIFT = 6
assert 1 << CHUNK_SHIFT == CHUNK
A_HEADS = 4
A_HEAD_DIM = 64
A_WIDTH = A_HEADS * A_HEAD_DIM
A_CHUNK = 128
B_HEADS = 8
B_KV_HEADS = 2
HEAD_DIM = 64
B_WIDTH = B_HEADS * HEAD_DIM
B_KV_WIDTH = B_KV_HEADS * HEAD_DIM
IDX_HEADS = 4
IDX_DIM = 64
TOPK_MAX = 256
ROPE_THETA = 10000.0
C_GROUPS = 4
C_WIDTH = 256
CONV_WIDTH = 31
N_MOD = 6
EPS = 1e-6

LANES = 128
SUBLANES = 8
BF16_SUBLANES = 16
CTX_ROWS = 32
CTX_PAD = CTX_ROWS - (CONV_WIDTH - 1)
VMEM_LIMIT = 56 * 1024 * 1024

OFF_A = 0
OFF_Q = OFF_A + 2 * A_WIDTH
OFF_K = OFF_Q + B_WIDTH
OFF_V = OFF_K + B_KV_WIDTH
OFF_QI = OFF_V + B_KV_WIDTH
OFF_KI = OFF_QI + IDX_HEADS * IDX_DIM
OFF_C = OFF_KI + LANES
N_IN_PAD = OFF_C + 2 * C_WIDTH

LOG2E = 1.4426950408889634
KEY_MIN_FINITE = -2139095040

F32 = jnp.float32
BF16 = jnp.bfloat16


def _dot(a, b):
    return jnp.dot(a, b, preferred_element_type=F32)


def _dot_t(a, b):
    return lax.dot_general(a, b, (((1,), (1,)), ((), ())), preferred_element_type=F32)


def _rms(x):
    return x * lax.rsqrt(jnp.mean(x * x, axis=-1, keepdims=True) + EPS)


def _lane_group(n, group):
    shift = group.bit_length() - 1
    assert 1 << shift == group
    return lax.shift_right_logical(lax.broadcasted_iota(jnp.int32, (1, n), 1), shift)


def _group_standardize(y, group):
    n = y.shape[-1]
    gid = _lane_group(n, group)
    inv = 1.0 / group
    mean = jnp.zeros_like(y)
    for g in range(n // group):
        m = gid == g
        s = jnp.sum(jnp.where(m, y, 0.0), axis=-1, keepdims=True) * inv
        mean = jnp.where(m, s, mean)
    yc = y - mean
    sq = yc * yc
    var = jnp.zeros_like(y)
    for g in range(n // group):
        m = gid == g
        s = jnp.sum(jnp.where(m, sq, 0.0), axis=-1, keepdims=True) * inv
        var = jnp.where(m, s, var)
    return yc * lax.rsqrt(var + EPS)


def _rope(x, cos, sin_lo, sin_hi):
    parts = []
    for c in range(x.shape[-1] // LANES):
        xc = x[:, c * LANES:(c + 1) * LANES]
        up = pltpu.roll(xc, LANES - HEAD_DIM // 2, 1)
        down = pltpu.roll(xc, HEAD_DIM // 2, 1)
        parts.append(xc * cos + up * sin_lo + down * sin_hi)
    return parts[0] if len(parts) == 1 else jnp.concatenate(parts, axis=-1)


def _mod_kernel(c_ref, w_ref, b_ref, o_ref):
    c = c_ref[...]
    cond = c * jax.nn.sigmoid(c)
    o_ref[0] = _dot(cond.astype(BF16), w_ref[0].astype(BF16)) + b_ref[0]


def _mod_call(c_all, w_ada, b_ada):
    depth, d, n = w_ada.shape
    rows = c_all.shape[0]
    tn = d
    return pl.pallas_call(
        _mod_kernel,
        out_shape=jax.ShapeDtypeStruct((depth, rows, n), F32),
        grid=(depth, n // tn),
        in_specs=[
            pl.BlockSpec((rows, d), lambda l, j: (0, 0)),
            pl.BlockSpec((1, d, tn), lambda l, j: (l, 0, j)),
            pl.BlockSpec((1, 1, tn), lambda l, j: (l, 0, j)),
        ],
        out_specs=pl.BlockSpec((1, rows, tn), lambda l, j: (l, 0, j)),
        compiler_params=pltpu.CompilerParams(
            dimension_semantics=("arbitrary", "arbitrary"), vmem_limit_bytes=VMEM_LIMIT),
        name="adaln_mod",
    )(c_all, w_ada, b_ada.reshape(depth, 1, n))


N_IN_OPERANDS = 13


def _in_kernel(*refs, tm, ln, d_model, n_aliased):
    (x_ref, mod_ref, g1_ref, w_ref, cos_ref, slo_ref, shi_ref, wsp_ref, bsp_ref,
     wdw_ref, bdw_ref, gcn_ref, cst_ref) = refs[:N_IN_OPERANDS]
    (q_ref, k_ref, v_ref, qi_ref, ki_ref, wi_ref, ac_ref, av_ref, cnew_ref,
     hp_ref, sh_ref) = refs[N_IN_OPERANDS + n_aliased:]
    j = pl.program_id(1)
    x = x_ref[0]
    mod = mod_ref[0]
    sh1 = mod[:, 0:d_model]
    sc1 = mod[:, d_model:2 * d_model]
    h = (_rms(x) * g1_ref[...]) * (1.0 + sc1) + sh1
    z = _dot(h.astype(BF16), w_ref[...])
    part = lambda off, width: z[:, off:off + width]

    cos = cos_ref[...]
    slo = slo_ref[...]
    shi = shi_ref[...]
    q = _rope(part(OFF_Q, B_WIDTH), cos, slo, shi)
    q_ref[0] = (q * (LOG2E * HEAD_DIM ** -0.5)).astype(BF16)
    k_ref[0] = _rope(part(OFF_K, B_KV_WIDTH), cos, slo, shi)
    v_ref[0] = part(OFF_V, B_KV_WIDTH)
    qi_ref[0] = _rope(part(OFF_QI, IDX_HEADS * IDX_DIM), cos, slo, shi).astype(BF16)
    kiwi = part(OFF_KI, LANES)
    ki_ref[0] = _rope(kiwi, cos, slo, shi)[:, 0:IDX_DIM]
    wi_ref[0] = kiwi

    za = part(OFF_A, 2 * A_WIDTH)
    za = 0.5 * za * (1.0 + lax.erf(za * (2.0 ** -0.5)))
    u = za[:, 0:A_WIDTH]
    vn = _group_standardize(za[:, A_WIDTH:2 * A_WIDTH], A_HEAD_DIM)
    av_ref[0] = vn
    vb = vn.astype(BF16)
    row = lax.broadcasted_iota(jnp.int32, (ln, ln), 0)
    col = lax.broadcasted_iota(jnp.int32, (ln, ln), 1)
    head_of_lane = _lane_group(A_WIDTH, A_HEAD_DIM)
    wsp = [jnp.where(col <= row, wsp_ref[g], 0.0).astype(BF16) for g in range(A_HEADS)]
    gated = []
    for c in range(tm // ln):
        vc = vb[c * ln:(c + 1) * ln, :]
        s = jnp.zeros((ln, A_WIDTH), F32)
        for g in range(A_HEADS):
            s = jnp.where(head_of_lane == g, _dot(wsp[g], vc), s)
        gated.append(u[c * ln:(c + 1) * ln, :] * (s + bsp_ref[...]))
    a_out = gated[0] if len(gated) == 1 else jnp.concatenate(gated, axis=0)
    ac_ref[0, :, 0:A_WIDTH] = a_out.astype(BF16)

    zc = part(OFF_C, 2 * C_WIDTH)
    hc = zc[:, 0:C_WIDTH] * jax.nn.sigmoid(zc[:, C_WIDTH:2 * C_WIDTH])

    @pl.when(j == 0)
    def _():
        hp_ref[0:CTX_ROWS, :] = cst_ref[0]

    hp_ref[CTX_ROWS:CTX_ROWS + tm, :] = hc
    span = tm + CTX_ROWS - SUBLANES
    for s in range(1, SUBLANES):
        sh_ref[s - 1, 0:span, :] = hp_ref[s:s + span, :]
    rb = min(tm, 64)
    wdw = wdw_ref[...]
    conv = []
    for r0 in range(0, tm, rb):
        acc = jnp.zeros((rb, C_WIDTH), F32)
        for t in range(CONV_WIDTH):
            phase = (CTX_PAD + t) % SUBLANES
            base = CTX_PAD + t - phase + r0
            rows = hp_ref[base:base + rb, :] if phase == 0 else sh_ref[phase - 1, base:base + rb, :]
            acc = acc + rows * wdw[t:t + 1, :]
        conv.append(acc)
    y = (conv[0] if len(conv) == 1 else jnp.concatenate(conv, axis=0)) + bdw_ref[...]
    y = _group_standardize(y, C_WIDTH // C_GROUPS) * gcn_ref[...]
    ac_ref[0, :, A_WIDTH:A_WIDTH + C_WIDTH] = (y * jax.nn.sigmoid(y)).astype(BF16)

    tail = hp_ref[tm:tm + CTX_ROWS, :]
    cnew_ref[0] = tail
    hp_ref[0:CTX_ROWS, :] = tail


def _in_call(x, mod, layer, mod_row0, g1, w_in_p, tabs, wsp, bsp_tab, wdw, bdw, gcn, cst, kv_stack, tm):
    b, t, d = x.shape
    depth = w_in_p.shape[0]
    ln = min(t, A_CHUNK)
    cos, slo, shi = tabs
    n_aliased = 0 if kv_stack is None else len(kv_stack)
    kern = functools.partial(_in_kernel, tm=tm, ln=ln, d_model=d, n_aliased=n_aliased)
    row3 = lambda bi, j: (bi, j, 0)
    lay_row4 = lambda bi, j: (layer, bi, j, 0)
    lay3 = lambda bi, j: (layer, 0, 0)
    out_shape = (
        jax.ShapeDtypeStruct((b, t, B_WIDTH), BF16),
        jax.ShapeDtypeStruct((depth, b, t, B_KV_WIDTH), F32),
        jax.ShapeDtypeStruct((depth, b, t, B_KV_WIDTH), F32),
        jax.ShapeDtypeStruct((b, t, IDX_HEADS * IDX_DIM), BF16),
        jax.ShapeDtypeStruct((depth, b, t, IDX_DIM), F32),
        jax.ShapeDtypeStruct((b, t, LANES), F32),
        jax.ShapeDtypeStruct((b, t, A_WIDTH + C_WIDTH), BF16),
        jax.ShapeDtypeStruct((b, t, A_WIDTH), F32),
        jax.ShapeDtypeStruct((b, CTX_ROWS, C_WIDTH), F32),
    )
    out_specs = (
        pl.BlockSpec((1, tm, B_WIDTH), row3),
        pl.BlockSpec((None, 1, tm, B_KV_WIDTH), lay_row4),
        pl.BlockSpec((None, 1, tm, B_KV_WIDTH), lay_row4),
        pl.BlockSpec((1, tm, IDX_HEADS * IDX_DIM), row3),
        pl.BlockSpec((None, 1, tm, IDX_DIM), lay_row4),
        pl.BlockSpec((1, tm, LANES), row3),
        pl.BlockSpec((1, tm, A_WIDTH + C_WIDTH), row3),
        pl.BlockSpec((1, tm, A_WIDTH), row3),
        pl.BlockSpec((1, CTX_ROWS, C_WIDTH), lambda bi, j: (bi, 0, 0)),
    )
    kv_outputs = (1, 2, 4)
    in_specs = [
        pl.BlockSpec((1, tm, d), row3),
        pl.BlockSpec((None, 1, 1, N_MOD * d), lambda bi, j: (layer, mod_row0 + bi, 0, 0)),
        pl.BlockSpec((None, 1, d), lay3),
        pl.BlockSpec((None, d, N_IN_PAD), lay3),
        pl.BlockSpec((tm, LANES), lambda bi, j: (j, 0)),
        pl.BlockSpec((tm, LANES), lambda bi, j: (j, 0)),
        pl.BlockSpec((tm, LANES), lambda bi, j: (j, 0)),
        pl.BlockSpec((None, A_HEADS, ln, ln), lambda bi, j: (layer, 0, 0, 0)),
        pl.BlockSpec((None, ln, A_WIDTH), lay3),
        pl.BlockSpec((None, CONV_WIDTH, C_WIDTH), lay3),
        pl.BlockSpec((None, 1, C_WIDTH), lay3),
        pl.BlockSpec((None, 1, C_WIDTH), lay3),
        pl.BlockSpec((1, CTX_ROWS, C_WIDTH), lambda bi, j: (bi, 0, 0)),
    ]
    args = [x, mod, g1, w_in_p, cos, slo, shi, wsp, bsp_tab, wdw, bdw, gcn, cst]
    assert len(args) == N_IN_OPERANDS
    aliases = {}
    if kv_stack is not None:
        for arr, out_pos in zip(kv_stack, kv_outputs):
            aliases[len(args)] = out_pos
            in_specs.append(pl.BlockSpec(memory_space=pl.ANY))
            args.append(arr)
    return pl.pallas_call(
        kern,
        out_shape=out_shape,
        grid=(b, t // tm),
        in_specs=in_specs,
        out_specs=out_specs,
        scratch_shapes=[pltpu.VMEM((tm + CTX_ROWS, C_WIDTH), F32),
                        pltpu.VMEM((SUBLANES - 1, tm + CTX_ROWS, C_WIDTH), F32)],
        input_output_aliases=aliases,
        compiler_params=pltpu.CompilerParams(
            dimension_semantics=("arbitrary", "arbitrary"), vmem_limit_bytes=VMEM_LIMIT),
        name="in_proj_mixers_ac",
    )(*args)


def _attn_kernel(q_ref, qi_ref, wi_ref, ck_ref, cv_ref, cki_ref, k_ref, v_ref, ki_ref, o_ref,
                 kb_ref, vb_ref, kib_ref, key_ref, bias_ref, *, tq, q_start, topk):
    past = ck_ref.shape[1]
    n_keys = past + k_ref.shape[1]
    lk = kb_ref.shape[0]
    for dst, old, new in ((kb_ref, ck_ref, k_ref), (vb_ref, cv_ref, v_ref), (kib_ref, cki_ref, ki_ref)):
        dst[0:past, :] = old[0]
        dst[past:n_keys, :] = new[0].astype(BF16)
        dst[n_keys:lk, :] = jnp.zeros((lk - n_keys, dst.shape[1]), BF16)
    j = pl.program_id(1)
    qchunk = lax.shift_right_logical(
        q_start + j * tq + lax.broadcasted_iota(jnp.int32, (tq, 1), 0), CHUNK_SHIFT)

    def allowed_at(k0, width):
        kpos = k0 + lax.broadcasted_iota(jnp.int32, (1, width), 1)
        return (lax.shift_right_logical(kpos, CHUNK_SHIFT) <= qchunk) & (kpos < n_keys)

    kib = kib_ref[...]
    qi = qi_ref[0]
    wi = wi_ref[0]
    score = jnp.zeros((tq, lk), F32)
    for h in range(IDX_HEADS):
        d = _dot_t(qi[:, h * IDX_DIM:(h + 1) * IDX_DIM], kib)
        score = score + wi[:, IDX_DIM + h:IDX_DIM + h + 1] * jnp.maximum(d, 0.0)
    score = jnp.where(score == 0.0, 0.0, score)
    score = jnp.where(allowed_at(0, lk), score, -jnp.inf)

    bits = pltpu.bitcast(score, jnp.int32)
    key_ref[...] = bits ^ ((bits >> 31) & jnp.int32(0x7FFFFFFF))
    int_min = jnp.int32(-2 ** 31)

    def digit_step(i, thr):
        unit = lax.shift_left(jnp.int32(1), jnp.int32(30) - 2 * i)
        key = key_ref[...]
        for mult in (1, 2, 3):
            cand = thr + mult * unit if mult == 1 else cand + unit
            cnt = jnp.sum((key >= cand).astype(F32), axis=-1, keepdims=True)
            best = jnp.where(cnt >= topk, cand, thr if mult == 1 else best)
        return best

    thr = lax.fori_loop(0, 16, digit_step, jnp.full((tq, 1), int_min, jnp.int32))

    need = topk - jnp.sum((key_ref[...] > thr).astype(F32), axis=-1, keepdims=True)
    tri = (lax.broadcasted_iota(jnp.int32, (LANES, LANES), 0)
           <= lax.broadcasted_iota(jnp.int32, (LANES, LANES), 1)).astype(F32).astype(BF16)
    seen = jnp.zeros((tq, 1), F32)
    for c in range(lk // LANES):
        sl = slice(c * LANES, (c + 1) * LANES)
        key_c = key_ref[:, sl]
        eq_c = key_c == thr
        eq_f = eq_c.astype(F32)
        rank = _dot(eq_f.astype(BF16), tri) + seen
        chosen = ((key_c > thr) | (eq_c & (rank <= need))) & allowed_at(c * LANES, LANES)
        bias_ref[:, sl] = jnp.where(chosen, 0.0, -jnp.inf)
        seen = seen + jnp.sum(eq_f, axis=-1, keepdims=True)

    q = q_ref[0]
    group = B_HEADS // B_KV_HEADS
    bias = jnp.concatenate([bias_ref[...]] * group, axis=0)
    outs = []
    for n in range(B_KV_HEADS):
        kn = kb_ref[:, n * HEAD_DIM:(n + 1) * HEAD_DIM]
        vn = vb_ref[:, n * HEAD_DIM:(n + 1) * HEAD_DIM]
        qg = jnp.concatenate([q[:, (n * group + g) * HEAD_DIM:(n * group + g + 1) * HEAD_DIM]
                              for g in range(group)], axis=0)
        logits = _dot_t(qg, kn) + bias
        p = jnp.exp2(logits - jnp.max(logits, axis=-1, keepdims=True))
        den = jnp.sum(p, axis=-1, keepdims=True)
        og = _dot(p.astype(BF16), vn) / den
        outs.extend(og[g * tq:(g + 1) * tq] for g in range(group))
    o_ref[0] = jnp.concatenate(outs, axis=-1).astype(BF16)


def _attn_call(q, qi, wi, cache_k, cache_v, cache_ki, k, v, ki, layer, q_start):
    b, t, _ = q.shape
    past = cache_k.shape[2]
    assert past % BF16_SUBLANES == 0 and t % BF16_SUBLANES == 0
    n_keys = past + t
    lk = -(-n_keys // LANES) * LANES
    topk = min(TOPK_MAX, n_keys // 4)
    kern = functools.partial(_attn_kernel, tq=t, q_start=q_start, topk=topk)
    row3 = lambda bi, j: (bi, 0, 0)
    lay_bat4 = lambda bi, j: (layer, bi, 0, 0)
    return pl.pallas_call(
        kern,
        out_shape=jax.ShapeDtypeStruct((b, t, B_WIDTH), BF16),
        grid=(b, 1),
        in_specs=[
            pl.BlockSpec((1, t, B_WIDTH), row3),
            pl.BlockSpec((1, t, IDX_HEADS * IDX_DIM), row3),
            pl.BlockSpec((1, t, LANES), row3),
            pl.BlockSpec((None, 1, past, B_KV_WIDTH), lay_bat4),
            pl.BlockSpec((None, 1, past, B_KV_WIDTH), lay_bat4),
            pl.BlockSpec((None, 1, past, IDX_DIM), lay_bat4),
            pl.BlockSpec((None, 1, t, B_KV_WIDTH), lay_bat4),
            pl.BlockSpec((None, 1, t, B_KV_WIDTH), lay_bat4),
            pl.BlockSpec((None, 1, t, IDX_DIM), lay_bat4),
        ],
        out_specs=pl.BlockSpec((1, t, B_WIDTH), row3),
        scratch_shapes=[pltpu.VMEM((lk, B_KV_WIDTH), BF16), pltpu.VMEM((lk, B_KV_WIDTH), BF16),
                        pltpu.VMEM((lk, IDX_DIM), BF16),
                        pltpu.VMEM((t, lk), jnp.int32), pltpu.VMEM((t, lk), F32)],
        compiler_params=pltpu.CompilerParams(
            dimension_semantics=("arbitrary", "arbitrary"), vmem_limit_bytes=VMEM_LIMIT),
        name="dsa_attention",
    )(q, qi, wi, cache_k, cache_v, cache_ki, k, v, ki)


FOLD_ROWS = 64


def _col_reduce(x, op):
    r = x.shape[0]
    if r > FOLD_ROWS and r % FOLD_ROWS == 0:
        acc = x[:FOLD_ROWS]
        for c in range(1, r // FOLD_ROWS):
            acc = op(acc, x[c * FOLD_ROWS:(c + 1) * FOLD_ROWS])
        x, r = acc, FOLD_ROWS
    while r % 16 == 0:
        x = op(x[:r // 2], x[r // 2:])
        r //= 2
    return (jnp.sum if op is jnp.add else jnp.max)(x, axis=0, keepdims=True)


WORD_BITS = 32
INT_MIN = -2 ** 31
QK_ROWS = 256
FFN_MIN_ROWS = 256
MXU_COLS = 256


def _bit_transpose(words):
    a = list(words)
    j, m = WORD_BITS // 2, 0x0000FFFF
    while j:
        k = 0
        while k < WORD_BITS:
            t = (a[k] ^ lax.shift_right_logical(a[k + j], j)) & m
            a[k] = a[k] ^ t
            a[k + j] = a[k + j] ^ lax.shift_left(t, j)
            k = (k + j + 1) & ~j
        j >>= 1
        m = (m ^ (m << j)) & 0x7FFFFFFF if j else m
    return a


def _select_threshold(key_ref, planes_ref, topk, tq, side_work=()):
    lk = key_ref.shape[0]
    slab = lk // WORD_BITS
    for r0 in range(0, slab, SUBLANES):
        words = [key_ref[i * slab + r0:i * slab + r0 + SUBLANES, :] for i in range(WORD_BITS)]
        for b, plane in enumerate(_bit_transpose(words)):
            planes_ref[b, r0:r0 + SUBLANES, :] = plane
    count = lambda words: _col_reduce(lax.population_count(words), jnp.add)
    alive = jnp.full((slab, tq), -1, jnp.int32)
    left = jnp.full((1, tq), topk, jnp.int32)
    thr = jnp.zeros((1, tq), jnp.int32)
    steps = WORD_BITS // 2
    for step, hi_bit in enumerate(range(WORD_BITS - 1, 0, -2)):
        for piece in side_work[step * len(side_work) // steps:(step + 1) * len(side_work) // steps]:
            piece()
        p_hi = planes_ref[WORD_BITS - 1 - hi_bit]
        p_lo = planes_ref[WORD_BITS - hi_bit]
        if hi_bit == WORD_BITS - 1:
            p_hi = ~p_hi
        u1 = alive & p_hi
        u0 = alive ^ u1
        c11 = u1 & p_lo
        c10 = u1 ^ c11
        c01 = u0 & p_lo
        c00 = u0 ^ c01
        n11 = count(c11)
        n1x = n11 + count(c10)
        n_1 = n1x + count(c01)
        t11 = n11 >= left
        t1x = n1x >= left
        t_1 = n_1 >= left
        alive = jnp.where(t11, c11, jnp.where(t1x, c10, jnp.where(t_1, c01, c00)))
        left = left - jnp.where(t11, 0, jnp.where(t1x, n11, jnp.where(t_1, n1x, n_1)))
        hi_val = jnp.int32(1 << hi_bit if hi_bit < WORD_BITS - 1 else INT_MIN)
        lo_val = jnp.int32(1 << (hi_bit - 1))
        thr = thr | jnp.where(t1x, hi_val, 0) | jnp.where(t11 | (t_1 & ~t1x), lo_val, 0)
    return thr ^ INT_MIN, left


def _attn_t_kernel(*refs, tq, n_keys, q_start, q_block0, topk, aliased):
    q_ref, qi_ref, wi_ref, k_ref, v_ref, ki_ref = refs[:6]
    o_ref, key_ref, bias_ref, planes_ref, logit_ref = refs[7:] if aliased else refs[6:]
    lk = k_ref.shape[1]
    j = pl.program_id(1)
    qchunk = lax.shift_right_logical(
        q_start + (q_block0 + j) * tq + lax.broadcasted_iota(jnp.int32, (1, tq), 1), CHUNK_SHIFT)

    def allowed_at(k0, rows):
        kpos = k0 + lax.broadcasted_iota(jnp.int32, (rows, 1), 0)
        return (lax.shift_right_logical(kpos, CHUNK_SHIFT) <= qchunk) & (kpos < n_keys)

    kib = ki_ref[0].astype(BF16)
    qi = qi_ref[0]
    wi_t = wi_ref[0].T
    score = jnp.zeros((lk, tq), F32)
    for h in range(IDX_HEADS):
        d = _dot_t(kib, qi[:, h * IDX_DIM:(h + 1) * IDX_DIM])
        score = score + wi_t[IDX_DIM + h:IDX_DIM + h + 1, :] * jnp.maximum(d, 0.0)
    score = jnp.where(score == 0.0, 0.0, score)
    free = min(((q_start + q_block0 * tq) // CHUNK + 1) * CHUNK, n_keys, lk)
    if free < lk:
        masked = jnp.where(allowed_at(free, lk - free), score[free:], -jnp.inf)
        score = jnp.concatenate([score[:free], masked], axis=0) if free else masked

    bits = pltpu.bitcast(score, jnp.int32)
    key_ref[...] = bits ^ ((bits >> 31) & jnp.int32(0x7FFFFFFF))

    q = q_ref[0]
    group = B_HEADS // B_KV_HEADS
    qgs = [jnp.concatenate([q[:, (n * group + g) * HEAD_DIM:(n * group + g + 1) * HEAD_DIM]
                            for g in range(group)], axis=0) for n in range(B_KV_HEADS)]

    def logits_block(r0, n):
        kn = k_ref[0, r0:r0 + QK_ROWS, n * HEAD_DIM:(n + 1) * HEAD_DIM].astype(BF16)
        logit_ref[n, r0:r0 + QK_ROWS, :] = _dot_t(kn, qgs[n])

    side_work = [functools.partial(logits_block, r0, n)
                 for r0 in range(0, lk, QK_ROWS) for n in range(B_KV_HEADS)]
    if lk <= topk:
        for piece in side_work:
            piece()
        bias_ref[...] = jnp.where(key_ref[...] >= KEY_MIN_FINITE, 0.0, -jnp.inf)
    else:
        thr, left = _select_threshold(key_ref, planes_ref, topk, tq, side_work)

        clamped = thr < KEY_MIN_FINITE
        thr = jnp.where(clamped, KEY_MIN_FINITE, thr)
        need = jnp.where(clamped, topk, left).astype(F32)
        tril = (lax.broadcasted_iota(jnp.int32, (LANES, LANES), 0)
                >= lax.broadcasted_iota(jnp.int32, (LANES, LANES), 1)).astype(F32).astype(BF16)
        seen = jnp.zeros((1, tq), F32)
        for c in range(lk // LANES):
            sl = slice(c * LANES, (c + 1) * LANES)
            key_c = key_ref[sl, :]
            eq_c = key_c == thr
            eq_f = eq_c.astype(F32)
            rank = _dot(tril, eq_f.astype(BF16)) + seen
            chosen = (key_c > thr) | (eq_c & (rank <= need))
            bias_ref[sl, :] = jnp.where(chosen, 0.0, -jnp.inf)
            seen = seen + _col_reduce(eq_f, jnp.add)

    vt = v_ref[0].T.astype(BF16)
    bias = bias_ref[...]
    ones_rows = jnp.ones((BF16_SUBLANES, lk), BF16)
    outs = []
    for n in range(B_KV_HEADS):
        vtn = jnp.concatenate([vt[n * HEAD_DIM:(n + 1) * HEAD_DIM, :], ones_rows], axis=0)
        ps = []
        for g in range(group):
            lg = logit_ref[n, :, g * tq:(g + 1) * tq] + bias
            ps.append(jnp.exp2(lg - _col_reduce(lg, jnp.maximum)).astype(BF16))
        og = _dot(vtn, jnp.concatenate(ps, axis=1))
        og = og[:HEAD_DIM] / og[HEAD_DIM:HEAD_DIM + 1]
        outs.extend(og[:, g * tq:(g + 1) * tq] for g in range(group))
    o_ref[0] = jnp.concatenate(outs, axis=0).T.astype(BF16)


def _attn_t_call(q, qi, wi, k, v, ki, prev, layer, n_keys, q_start, q_block0, n_q_blocks, lk, tq):
    b, t, _ = q.shape
    topk = min(TOPK_MAX, n_keys // 4)
    aliased = prev is not None
    kern = functools.partial(_attn_t_kernel, tq=tq, n_keys=n_keys, q_start=q_start,
                             q_block0=q_block0, topk=topk, aliased=aliased)
    row3 = lambda bi, j: (bi, q_block0 + j, 0)
    lay_bat4 = lambda bi, j: (layer, bi, 0, 0)
    in_specs = [
        pl.BlockSpec((1, tq, B_WIDTH), row3),
        pl.BlockSpec((1, tq, IDX_HEADS * IDX_DIM), row3),
        pl.BlockSpec((1, tq, LANES), row3),
        pl.BlockSpec((None, 1, lk, B_KV_WIDTH), lay_bat4),
        pl.BlockSpec((None, 1, lk, B_KV_WIDTH), lay_bat4),
        pl.BlockSpec((None, 1, lk, IDX_DIM), lay_bat4),
    ]
    args = [q, qi, wi, k, v, ki]
    if aliased:
        in_specs.append(pl.BlockSpec(memory_space=pl.ANY))
        args.append(prev)
    return pl.pallas_call(
        kern,
        out_shape=jax.ShapeDtypeStruct((b, t, B_WIDTH), BF16),
        grid=(b, n_q_blocks),
        in_specs=in_specs,
        out_specs=pl.BlockSpec((1, tq, B_WIDTH), row3),
        scratch_shapes=[pltpu.VMEM((lk, tq), jnp.int32), pltpu.VMEM((lk, tq), F32),
                        pltpu.VMEM((WORD_BITS, lk // WORD_BITS, tq), jnp.int32),
                        pltpu.VMEM((B_KV_HEADS, lk, B_HEADS // B_KV_HEADS * tq), F32)],
        input_output_aliases={6: 0} if aliased else {},
        compiler_params=pltpu.CompilerParams(
            dimension_semantics=("arbitrary", "arbitrary"), vmem_limit_bytes=VMEM_LIMIT),
        name="dsa_attention_t",
    )(*args)


def _attn_prompt(q, qi, wi, k, v, ki, layer, tq):
    t = q.shape[1]
    nqb = t // tq
    per_seg = max(1, WORD_BITS * SUBLANES // tq)
    assert nqb % per_seg == 0 and (per_seg * tq) % (WORD_BITS * SUBLANES) == 0
    bo = None
    for s in range(nqb // per_seg):
        bo = _attn_t_call(q, qi, wi, k, v, ki, bo, layer, t, 0, s * per_seg, per_seg,
                          (s + 1) * per_seg * tq, tq)
    return bo


def _ffn_kernel(x_ref, ac_ref, bo_ref, mod_ref, g2_ref, wo_ref, wg_ref, wu_ref, wd_ref,
                gf_ref, o_ref, *, d_model, ff_chunk, final):
    nb, tm, _ = x_ref.shape
    merge = lambda ref: ref[...].reshape(nb * tm, ref.shape[-1])
    x = merge(x_ref)
    mod = mod_ref[...]

    def mod_rows(i):
        m = mod[:, :, i * d_model:(i + 1) * d_model]
        return m[0] if nb == 1 else jnp.broadcast_to(m, (nb, tm, d_model)).reshape(nb * tm, d_model)

    ga1, sh2, sc2, ga2 = mod_rows(2), mod_rows(3), mod_rows(4), mod_rows(5)
    ac = merge(ac_ref)
    mix = (_dot(ac[:, :A_WIDTH], wo_ref[0:A_WIDTH, :])
           + _dot(merge(bo_ref), wo_ref[A_WIDTH:A_WIDTH + B_WIDTH, :])
           + _dot(ac[:, A_WIDTH:], wo_ref[A_WIDTH + B_WIDTH:, :]))
    x1 = x + ga1 * mix
    h2 = ((_rms(x1) * g2_ref[...]) * (1.0 + sc2) + sh2).astype(BF16)
    d_ff = wg_ref.shape[1]
    acc = jnp.zeros_like(x1)
    for c0 in range(0, d_ff, ff_chunk):
        gate = _dot(h2, wg_ref[:, c0:c0 + ff_chunk])
        up = _dot(h2, wu_ref[:, c0:c0 + ff_chunk])
        act = (gate * jax.nn.sigmoid(gate)) * up
        acc = acc + _dot(act.astype(BF16), wd_ref[c0:c0 + ff_chunk, :])
    x2 = x1 + ga2 * acc
    out = _rms(x2) * gf_ref[...] if final else x2
    o_ref[...] = out.reshape(nb, tm, d_model)


def _ffn_call(x, ac, bo, mod, layer, mod_row0, g2, wo, wg, wu, wd, gf, tm, final):
    b, t, d = x.shape
    d_ff = wg.shape[2]
    ff_chunk = MXU_COLS if d_ff % MXU_COLS == 0 else d_ff
    nb = max(1, min(b, FFN_MIN_ROWS // tm))
    assert b % nb == 0 and mod_row0 % nb == 0
    kern = functools.partial(_ffn_kernel, d_model=d, ff_chunk=ff_chunk, final=final)
    row3 = lambda bi, j: (bi, j, 0)
    lay3 = lambda bi, j: (layer, 0, 0)
    once = pl.Buffered(1)
    return pl.pallas_call(
        kern,
        out_shape=jax.ShapeDtypeStruct((b, t, d), F32),
        grid=(b // nb, t // tm),
        in_specs=[
            pl.BlockSpec((nb, tm, d), row3),
            pl.BlockSpec((nb, tm, A_WIDTH + C_WIDTH), row3),
            pl.BlockSpec((nb, tm, B_WIDTH), row3),
            pl.BlockSpec((None, nb, 1, N_MOD * d), lambda bi, j: (layer, mod_row0 // nb + bi, 0, 0)),
            pl.BlockSpec((None, 1, d), lay3),
            pl.BlockSpec((None, A_WIDTH + B_WIDTH + C_WIDTH, d), lay3, pipeline_mode=once),
            pl.BlockSpec((None, d, d_ff), lay3, pipeline_mode=once),
            pl.BlockSpec((None, d, d_ff), lay3, pipeline_mode=once),
            pl.BlockSpec((None, d_ff, d), lay3, pipeline_mode=once),
            pl.BlockSpec((1, d), lambda bi, j: (0, 0)),
        ],
        out_specs=pl.BlockSpec((nb, tm, d), row3),
        compiler_params=pltpu.CompilerParams(
            dimension_semantics=("arbitrary", "arbitrary"), vmem_limit_bytes=VMEM_LIMIT),
        name="out_proj_ffn",
    )(x, ac, bo, mod, g2, wo, wg, wu, wd, gf)


def _rope_tables(start, t):
    half = HEAD_DIM // 2
    inv = jnp.power(ROPE_THETA, -2.0 * jnp.arange(half, dtype=F32) / HEAD_DIM)
    ang = (start + jnp.arange(t)).astype(F32)[:, None] * inv[None, :]
    cos = jnp.cos(ang)
    sin = jnp.sin(ang)
    zero = jnp.zeros_like(sin)
    reps = LANES // HEAD_DIM
    return (jnp.tile(jnp.concatenate([cos, cos], axis=-1), (1, reps)),
            jnp.tile(jnp.concatenate([-sin, zero], axis=-1), (1, reps)),
            jnp.tile(jnp.concatenate([zero, sin], axis=-1), (1, reps)))


def _pad_w_in(w_in):
    n_head = OFF_KI + IDX_DIM + IDX_HEADS
    depth, d, _ = w_in.shape
    pad = jnp.zeros((depth, d, OFF_C - n_head), w_in.dtype)
    return jnp.concatenate([w_in[:, :, :n_head], pad, w_in[:, :, n_head:]], axis=-1).astype(BF16)


def _trunk(x, mod, mod_row0, cache, params, q_start, tm, tq):
    (g_norm1, g_norm2, w_in_p, w_spatial, b_spatial, w_dw, b_dw, g_cnorm,
     wo, wg, wu, wd, g_final) = params
    b, t, d = x.shape
    depth = w_in_p.shape[0]
    ln = min(t, A_CHUNK)
    tabs = _rope_tables(q_start, t)
    wsp = w_spatial[:, :, :ln, :ln]
    bsp_tab = jnp.repeat(jnp.swapaxes(b_spatial[:, :, :ln], 1, 2), A_HEAD_DIM, axis=-1)
    kv_stack, convs, avs = None, [], []
    if cache is not None:
        past = cache[0].shape[2]
        cache_bf16 = (cache[0].astype(BF16).reshape(depth, b, past, B_KV_WIDTH),
                      cache[1].astype(BF16).reshape(depth, b, past, B_KV_WIDTH),
                      cache[2].astype(BF16))
    for l in range(depth):
        if cache is None:
            cst = jnp.zeros((b, CTX_ROWS, C_WIDTH), F32)
        else:
            cst = jnp.pad(cache[3][l], ((0, 0), (CTX_PAD, 0), (0, 0)))
        q, ks, vs, qi, kis, wi, ac, av, cnew = _in_call(
            x, mod, l, mod_row0, g_norm1, w_in_p, tabs, wsp, bsp_tab, w_dw, b_dw, g_cnorm, cst,
            kv_stack, tm)
        kv_stack = (ks, vs, kis)
        if cache is None:
            bo = _attn_prompt(q, qi, wi, ks, vs, kis, l, tq)
        else:
            bo = _attn_call(q, qi, wi, *cache_bf16, ks, vs, kis, l, q_start)
        x = _ffn_call(x, ac, bo, mod, l, mod_row0, g_norm2, wo, wg, wu, wd, g_final,
                      tm, final=(l == depth - 1))
        convs.append(cnew[:, CTX_PAD:, :])
        avs.append(av)
    ks, vs, kis = kv_stack
    return (x, ks.reshape(depth, b, t, B_KV_HEADS, HEAD_DIM), vs.reshape(depth, b, t, B_KV_HEADS, HEAD_DIM),
            kis, jnp.stack(convs), jnp.stack(avs))


def kernel(x_prompt, x_sample, cache_k, cache_v, cache_kidx, state_conv, c_prompt, c_sample,
           w_ada, b_ada, g_norm1, g_norm2, w_in, w_spatial, b_spatial, w_dw, b_dw, g_cnorm,
           w_out, w_gate, w_up, w_down, g_final):
    depth, d = g_norm1.shape
    nb_p, t_p, _ = x_prompt.shape
    nb_s, t_s, _ = x_sample.shape
    past = cache_k.shape[2]

    rows = nb_p + nb_s
    rows_pad = -(-rows // 8) * 8
    c_all = jnp.pad(jnp.concatenate([c_prompt, c_sample], axis=0), ((0, rows_pad - rows), (0, 0)))
    mod = _mod_call(c_all, w_ada, b_ada).reshape(depth, rows_pad, 1, N_MOD * d)

    params = (
        g_norm1.reshape(depth, 1, d), g_norm2.reshape(depth, 1, d), _pad_w_in(w_in),
        w_spatial, b_spatial, w_dw, b_dw.reshape(depth, 1, C_WIDTH), g_cnorm.reshape(depth, 1, C_WIDTH),
        w_out.astype(BF16), w_gate.astype(BF16), w_up.astype(BF16), w_down.astype(BF16),
        g_final.reshape(1, d),
    )

    y_p, p_k, p_v, p_ki, p_conv, _ = _trunk(
        x_prompt, mod, 0, None, params, 0, tm=min(t_p, 512), tq=min(t_p, 256))
    y_s, s_k, s_v, s_ki, s_conv, s_av = _trunk(
        x_sample, mod, nb_p, (cache_k, cache_v, cache_kidx, state_conv), params, past,
        tm=t_s, tq=t_s)
    return (y_p, y_s, p_k, p_v, p_ki, p_conv, s_k, s_v, s_ki, s_conv, s_av)
```

```python
import functools

import jax
import jax.numpy as jnp
from jax import lax
from jax.experimental import pallas as pl
from jax.experimental.pallas import tpu as pltpu

CHUNK = 64
CHUNK_SHIFT = 6
assert 1 << CHUNK_SHIFT == CHUNK
A_HEADS = 4
A_HEAD_DIM = 64
A_WIDTH = A_HEADS * A_HEAD_DIM
A_CHUNK = 128
B_HEADS = 8
B_KV_HEADS = 2
HEAD_DIM = 64
B_WIDTH = B_HEADS * HEAD_DIM
B_KV_WIDTH = B_KV_HEADS * HEAD_DIM
IDX_HEADS = 4
IDX_DIM = 64
TOPK_MAX = 256
ROPE_THETA = 10000.0
C_GROUPS = 4
C_WIDTH = 256
CONV_WIDTH = 31
N_MOD = 6
EPS = 1e-6

LANES = 128
SUBLANES = 8
BF16_SUBLANES = 16
CTX_ROWS = 32
CTX_PAD = CTX_ROWS - (CONV_WIDTH - 1)
VMEM_LIMIT = 56 * 1024 * 1024
VMEM_FLOOR = 16 * 1024 * 1024


def _vmem_request(estimate_bytes):
    return int(min(VMEM_LIMIT, max(VMEM_FLOOR, estimate_bytes)))

OFF_A = 0
OFF_Q = OFF_A + 2 * A_WIDTH
OFF_K = OFF_Q + B_WIDTH
OFF_V = OFF_K + B_KV_WIDTH
OFF_QI = OFF_V + B_KV_WIDTH
OFF_KI = OFF_QI + IDX_HEADS * IDX_DIM
OFF_C = OFF_KI + LANES
N_IN_PAD = OFF_C + 2 * C_WIDTH

LOG2E = 1.4426950408889634
KEY_MIN_FINITE = -2139095040

F32 = jnp.float32
BF16 = jnp.bfloat16


def _dot(a, b):
    return jnp.dot(a, b, preferred_element_type=F32)


def _dot_t(a, b):
    return lax.dot_general(a, b, (((1,), (1,)), ((), ())), preferred_element_type=F32)


def _rms(x):
    return x * lax.rsqrt(jnp.mean(x * x, axis=-1, keepdims=True) + EPS)


def _lane_group(n, group):
    shift = group.bit_length() - 1
    assert 1 << shift == group
    return lax.shift_right_logical(lax.broadcasted_iota(jnp.int32, (1, n), 1), shift)


def _group_standardize(y, group):
    n = y.shape[-1]
    gid = _lane_group(n, group)
    inv = 1.0 / group
    mean = jnp.zeros_like(y)
    for g in range(n // group):
        m = gid == g
        s = jnp.sum(jnp.where(m, y, 0.0), axis=-1, keepdims=True) * inv
        mean = jnp.where(m, s, mean)
    yc = y - mean
    sq = yc * yc
    var = jnp.zeros_like(y)
    for g in range(n // group):
        m = gid == g
        s = jnp.sum(jnp.where(m, sq, 0.0), axis=-1, keepdims=True) * inv
        var = jnp.where(m, s, var)
    return yc * lax.rsqrt(var + EPS)


def _rope(x, cos, sin_lo, sin_hi):
    parts = []
    for c in range(x.shape[-1] // LANES):
        xc = x[:, c * LANES:(c + 1) * LANES]
        up = pltpu.roll(xc, LANES - HEAD_DIM // 2, 1)
        down = pltpu.roll(xc, HEAD_DIM // 2, 1)
        parts.append(xc * cos + up * sin_lo + down * sin_hi)
    return parts[0] if len(parts) == 1 else jnp.concatenate(parts, axis=-1)


def _mod_kernel(c_ref, w_ref, b_ref, o_ref):
    c = c_ref[...]
    cond = c * jax.nn.sigmoid(c)
    o_ref[0] = _dot(cond.astype(BF16), w_ref[0].astype(BF16)) + b_ref[0]


def _mod_call(c_all, w_ada, b_ada):
    depth, d, n = w_ada.shape
    rows = c_all.shape[0]
    tn = d
    return pl.pallas_call(
        _mod_kernel,
        out_shape=jax.ShapeDtypeStruct((depth, rows, n), F32),
        grid=(depth, n // tn),
        in_specs=[
            pl.BlockSpec((rows, d), lambda l, j: (0, 0)),
            pl.BlockSpec((1, d, tn), lambda l, j: (l, 0, j)),
            pl.BlockSpec((1, 1, tn), lambda l, j: (l, 0, j)),
        ],
        out_specs=pl.BlockSpec((1, rows, tn), lambda l, j: (l, 0, j)),
        compiler_params=pltpu.CompilerParams(
            dimension_semantics=("arbitrary", "arbitrary"),
            vmem_limit_bytes=_vmem_request(2 * 4 * (d * tn + rows * (d + 2 * tn)))),
        name="adaln_mod",
    )(c_all, w_ada, b_ada.reshape(depth, 1, n))


N_IN_OPERANDS = 13


def _in_kernel(*refs, tm, ln, d_model, n_aliased):
    (x_ref, mod_ref, g1_ref, w_ref, cos_ref, slo_ref, shi_ref, wsp_ref, bsp_ref,
     wdw_ref, bdw_ref, gcn_ref, cst_ref) = refs[:N_IN_OPERANDS]
    (q_ref, k_ref, v_ref, qi_ref, ki_ref, wi_ref, ac_ref, av_ref, cnew_ref,
     hp_ref, sh_ref) = refs[N_IN_OPERANDS + n_aliased:]
    j = pl.program_id(1)
    x = x_ref[0]
    mod = mod_ref[0]
    sh1 = mod[:, 0:d_model]
    sc1 = mod[:, d_model:2 * d_model]
    h = (_rms(x) * g1_ref[...]) * (1.0 + sc1) + sh1
    z = _dot(h.astype(BF16), w_ref[...])
    part = lambda off, width: z[:, off:off + width]

    cos = cos_ref[...]
    slo = slo_ref[...]
    shi = shi_ref[...]
    q = _rope(part(OFF_Q, B_WIDTH), cos, slo, shi)
    q_ref[0] = (q * (LOG2E * HEAD_DIM ** -0.5)).astype(BF16)
    k_ref[0] = _rope(part(OFF_K, B_KV_WIDTH), cos, slo, shi)
    v_ref[0] = part(OFF_V, B_KV_WIDTH)
    qi_ref[0] = _rope(part(OFF_QI, IDX_HEADS * IDX_DIM), cos, slo, shi).astype(BF16)
    kiwi = part(OFF_KI, LANES)
    ki_ref[0] = _rope(kiwi, cos, slo, shi)[:, 0:IDX_DIM]
    wi_ref[0] = kiwi

    za = part(OFF_A, 2 * A_WIDTH)
    za = 0.5 * za * (1.0 + lax.erf(za * (2.0 ** -0.5)))
    u = za[:, 0:A_WIDTH]
    vn = _group_standardize(za[:, A_WIDTH:2 * A_WIDTH], A_HEAD_DIM)
    av_ref[0] = vn
    vb = vn.astype(BF16)
    row = lax.broadcasted_iota(jnp.int32, (ln, ln), 0)
    col = lax.broadcasted_iota(jnp.int32, (ln, ln), 1)
    head_of_lane = _lane_group(A_WIDTH, A_HEAD_DIM)
    wsp = [jnp.where(col <= row, wsp_ref[g], 0.0).astype(BF16) for g in range(A_HEADS)]
    gated = []
    for c in range(tm // ln):
        vc = vb[c * ln:(c + 1) * ln, :]
        s = jnp.zeros((ln, A_WIDTH), F32)
        for g in range(A_HEADS):
            s = jnp.where(head_of_lane == g, _dot(wsp[g], vc), s)
        gated.append(u[c * ln:(c + 1) * ln, :] * (s + bsp_ref[...]))
    a_out = gated[0] if len(gated) == 1 else jnp.concatenate(gated, axis=0)
    ac_ref[0, :, 0:A_WIDTH] = a_out.astype(BF16)

    zc = part(OFF_C, 2 * C_WIDTH)
    hc = zc[:, 0:C_WIDTH] * jax.nn.sigmoid(zc[:, C_WIDTH:2 * C_WIDTH])

    @pl.when(j == 0)
    def _():
        hp_ref[0:CTX_ROWS, :] = cst_ref[0]

    hp_ref[CTX_ROWS:CTX_ROWS + tm, :] = hc
    span = tm + CTX_ROWS - SUBLANES
    for s in range(1, SUBLANES):
        sh_ref[s - 1, 0:span, :] = hp_ref[s:s + span, :]
    rb = min(tm, 64)
    wdw = wdw_ref[...]
    conv = []
    for r0 in range(0, tm, rb):
        acc = jnp.zeros((rb, C_WIDTH), F32)
        for t in range(CONV_WIDTH):
            phase = (CTX_PAD + t) % SUBLANES
            base = CTX_PAD + t - phase + r0
            rows = hp_ref[base:base + rb, :] if phase == 0 else sh_ref[phase - 1, base:base + rb, :]
            acc = acc + rows * wdw[t:t + 1, :]
        conv.append(acc)
    y = (conv[0] if len(conv) == 1 else jnp.concatenate(conv, axis=0)) + bdw_ref[...]
    y = _group_standardize(y, C_WIDTH // C_GROUPS) * gcn_ref[...]
    ac_ref[0, :, A_WIDTH:A_WIDTH + C_WIDTH] = (y * jax.nn.sigmoid(y)).astype(BF16)

    tail = hp_ref[tm:tm + CTX_ROWS, :]
    cnew_ref[0] = tail
    hp_ref[0:CTX_ROWS, :] = tail


def _in_call(x, mod, layer, mod_row0, g1, w_in_p, tabs, wsp, bsp_tab, wdw, bdw, gcn, cst, kv_stack, tm):
    b, t, d = x.shape
    depth = w_in_p.shape[0]
    ln = min(t, A_CHUNK)
    cos, slo, shi = tabs
    n_aliased = 0 if kv_stack is None else len(kv_stack)
    kern = functools.partial(_in_kernel, tm=tm, ln=ln, d_model=d, n_aliased=n_aliased)
    row3 = lambda bi, j: (bi, j, 0)
    lay_row4 = lambda bi, j: (layer, bi, j, 0)
    lay3 = lambda bi, j: (layer, 0, 0)
    out_shape = (
        jax.ShapeDtypeStruct((b, t, B_WIDTH), BF16),
        jax.ShapeDtypeStruct((depth, b, t, B_KV_WIDTH), F32),
        jax.ShapeDtypeStruct((depth, b, t, B_KV_WIDTH), F32),
        jax.ShapeDtypeStruct((b, t, IDX_HEADS * IDX_DIM), BF16),
        jax.ShapeDtypeStruct((depth, b, t, IDX_DIM), F32),
        jax.ShapeDtypeStruct((b, t, LANES), F32),
        jax.ShapeDtypeStruct((b, t, A_WIDTH + C_WIDTH), BF16),
        jax.ShapeDtypeStruct((b, t, A_WIDTH), F32),
        jax.ShapeDtypeStruct((b, CTX_ROWS, C_WIDTH), F32),
    )
    out_specs = (
        pl.BlockSpec((1, tm, B_WIDTH), row3),
        pl.BlockSpec((None, 1, tm, B_KV_WIDTH), lay_row4),
        pl.BlockSpec((None, 1, tm, B_KV_WIDTH), lay_row4),
        pl.BlockSpec((1, tm, IDX_HEADS * IDX_DIM), row3),
        pl.BlockSpec((None, 1, tm, IDX_DIM), lay_row4),
        pl.BlockSpec((1, tm, LANES), row3),
        pl.BlockSpec((1, tm, A_WIDTH + C_WIDTH), row3),
        pl.BlockSpec((1, tm, A_WIDTH), row3),
        pl.BlockSpec((1, CTX_ROWS, C_WIDTH), lambda bi, j: (bi, 0, 0)),
    )
    kv_outputs = (1, 2, 4)
    in_specs = [
        pl.BlockSpec((1, tm, d), row3),
        pl.BlockSpec((None, 1, 1, N_MOD * d), lambda bi, j: (layer, mod_row0 + bi, 0, 0)),
        pl.BlockSpec((None, 1, d), lay3),
        pl.BlockSpec((None, d, N_IN_PAD), lay3),
        pl.BlockSpec((tm, LANES), lambda bi, j: (j, 0)),
        pl.BlockSpec((tm, LANES), lambda bi, j: (j, 0)),
        pl.BlockSpec((tm, LANES), lambda bi, j: (j, 0)),
        pl.BlockSpec((None, A_HEADS, ln, ln), lambda bi, j: (layer, 0, 0, 0)),
        pl.BlockSpec((None, ln, A_WIDTH), lay3),
        pl.BlockSpec((None, CONV_WIDTH, C_WIDTH), lay3),
        pl.BlockSpec((None, 1, C_WIDTH), lay3),
        pl.BlockSpec((None, 1, C_WIDTH), lay3),
        pl.BlockSpec((1, CTX_ROWS, C_WIDTH), lambda bi, j: (bi, 0, 0)),
    ]
    args = [x, mod, g1, w_in_p, cos, slo, shi, wsp, bsp_tab, wdw, bdw, gcn, cst]
    assert len(args) == N_IN_OPERANDS
    aliases = {}
    if kv_stack is not None:
        for arr, out_pos in zip(kv_stack, kv_outputs):
            aliases[len(args)] = out_pos
            in_specs.append(pl.BlockSpec(memory_space=pl.ANY))
            args.append(arr)
    return pl.pallas_call(
        kern,
        out_shape=out_shape,
        grid=(b, t // tm),
        in_specs=in_specs,
        out_specs=out_specs,
        scratch_shapes=[pltpu.VMEM((tm + CTX_ROWS, C_WIDTH), F32),
                        pltpu.VMEM((SUBLANES - 1, tm + CTX_ROWS, C_WIDTH), F32)],
        input_output_aliases=aliases,
        compiler_params=pltpu.CompilerParams(
            dimension_semantics=("arbitrary", "arbitrary"), vmem_limit_bytes=VMEM_LIMIT),
        name="in_proj_mixers_ac",
    )(*args)


def _attn_kernel(q_ref, qi_ref, wi_ref, ck_ref, cv_ref, cki_ref, k_ref, v_ref, ki_ref, o_ref,
                 kb_ref, vb_ref, kib_ref, key_ref, bias_ref, *, tq, q_start, topk):
    past = ck_ref.shape[1]
    n_keys = past + k_ref.shape[1]
    lk = kb_ref.shape[0]
    for dst, old, new in ((kb_ref, ck_ref, k_ref), (vb_ref, cv_ref, v_ref), (kib_ref, cki_ref, ki_ref)):
        dst[0:past, :] = old[0]
        dst[past:n_keys, :] = new[0].astype(BF16)
        dst[n_keys:lk, :] = jnp.zeros((lk - n_keys, dst.shape[1]), BF16)
    j = pl.program_id(1)
    qchunk = lax.shift_right_logical(
        q_start + j * tq + lax.broadcasted_iota(jnp.int32, (tq, 1), 0), CHUNK_SHIFT)

    def allowed_at(k0, width):
        kpos = k0 + lax.broadcasted_iota(jnp.int32, (1, width), 1)
        return (lax.shift_right_logical(kpos, CHUNK_SHIFT) <= qchunk) & (kpos < n_keys)

    kib = kib_ref[...]
    qi = qi_ref[0]
    wi = wi_ref[0]
    score = jnp.zeros((tq, lk), F32)
    for h in range(IDX_HEADS):
        d = _dot_t(qi[:, h * IDX_DIM:(h + 1) * IDX_DIM], kib)
        score = score + wi[:, IDX_DIM + h:IDX_DIM + h + 1] * jnp.maximum(d, 0.0)
    score = jnp.where(score == 0.0, 0.0, score)
    score = jnp.where(allowed_at(0, lk), score, -jnp.inf)

    bits = pltpu.bitcast(score, jnp.int32)
    key_ref[...] = bits ^ ((bits >> 31) & jnp.int32(0x7FFFFFFF))
    int_min = jnp.int32(-2 ** 31)

    def digit_step(i, thr):
        unit = lax.shift_left(jnp.int32(1), jnp.int32(30) - 2 * i)
        key = key_ref[...]
        for mult in (1, 2, 3):
            cand = thr + mult * unit if mult == 1 else cand + unit
            cnt = jnp.sum((key >= cand).astype(F32), axis=-1, keepdims=True)
            best = jnp.where(cnt >= topk, cand, thr if mult == 1 else best)
        return best

    thr = lax.fori_loop(0, 16, digit_step, jnp.full((tq, 1), int_min, jnp.int32))

    need = topk - jnp.sum((key_ref[...] > thr).astype(F32), axis=-1, keepdims=True)
    tri = (lax.broadcasted_iota(jnp.int32, (LANES, LANES), 0)
           <= lax.broadcasted_iota(jnp.int32, (LANES, LANES), 1)).astype(F32).astype(BF16)
    seen = jnp.zeros((tq, 1), F32)
    for c in range(lk // LANES):
        sl = slice(c * LANES, (c + 1) * LANES)
        key_c = key_ref[:, sl]
        eq_c = key_c == thr
        eq_f = eq_c.astype(F32)
        rank = _dot(eq_f.astype(BF16), tri) + seen
        chosen = ((key_c > thr) | (eq_c & (rank <= need))) & allowed_at(c * LANES, LANES)
        bias_ref[:, sl] = jnp.where(chosen, 0.0, -jnp.inf)
        seen = seen + jnp.sum(eq_f, axis=-1, keepdims=True)

    q = q_ref[0]
    group = B_HEADS // B_KV_HEADS
    bias = jnp.concatenate([bias_ref[...]] * group, axis=0)
    outs = []
    for n in range(B_KV_HEADS):
        kn = kb_ref[:, n * HEAD_DIM:(n + 1) * HEAD_DIM]
        vn = vb_ref[:, n * HEAD_DIM:(n + 1) * HEAD_DIM]
        qg = jnp.concatenate([q[:, (n * group + g) * HEAD_DIM:(n * group + g + 1) * HEAD_DIM]
                              for g in range(group)], axis=0)
        logits = _dot_t(qg, kn) + bias
        p = jnp.exp2(logits - jnp.max(logits, axis=-1, keepdims=True))
        den = jnp.sum(p, axis=-1, keepdims=True)
        og = _dot(p.astype(BF16), vn) / den
        outs.extend(og[g * tq:(g + 1) * tq] for g in range(group))
    o_ref[0] = jnp.concatenate(outs, axis=-1).astype(BF16)


def _attn_call(q, qi, wi, cache_k, cache_v, cache_ki, k, v, ki, layer, q_start):
    b, t, _ = q.shape
    past = cache_k.shape[2]
    assert past % BF16_SUBLANES == 0 and t % BF16_SUBLANES == 0
    n_keys = past + t
    lk = -(-n_keys // LANES) * LANES
    topk = min(TOPK_MAX, n_keys // 4)
    kern = functools.partial(_attn_kernel, tq=t, q_start=q_start, topk=topk)
    row3 = lambda bi, j: (bi, 0, 0)
    lay_bat4 = lambda bi, j: (layer, bi, 0, 0)
    return pl.pallas_call(
        kern,
        out_shape=jax.ShapeDtypeStruct((b, t, B_WIDTH), BF16),
        grid=(b, 1),
        in_specs=[
            pl.BlockSpec((1, t, B_WIDTH), row3),
            pl.BlockSpec((1, t, IDX_HEADS * IDX_DIM), row3),
            pl.BlockSpec((1, t, LANES), row3),
            pl.BlockSpec((None, 1, past, B_KV_WIDTH), lay_bat4),
            pl.BlockSpec((None, 1, past, B_KV_WIDTH), lay_bat4),
            pl.BlockSpec((None, 1, past, IDX_DIM), lay_bat4),
            pl.BlockSpec((None, 1, t, B_KV_WIDTH), lay_bat4),
            pl.BlockSpec((None, 1, t, B_KV_WIDTH), lay_bat4),
            pl.BlockSpec((None, 1, t, IDX_DIM), lay_bat4),
        ],
        out_specs=pl.BlockSpec((1, t, B_WIDTH), row3),
        scratch_shapes=[pltpu.VMEM((lk, B_KV_WIDTH), BF16), pltpu.VMEM((lk, B_KV_WIDTH), BF16),
                        pltpu.VMEM((lk, IDX_DIM), BF16),
                        pltpu.VMEM((t, lk), jnp.int32), pltpu.VMEM((t, lk), F32)],
        compiler_params=pltpu.CompilerParams(
            dimension_semantics=("arbitrary", "arbitrary"), vmem_limit_bytes=VMEM_LIMIT),
        name="dsa_attention",
    )(q, qi, wi, cache_k, cache_v, cache_ki, k, v, ki)


FOLD_ROWS = 64


def _col_reduce(x, op):
    r = x.shape[0]
    if r > FOLD_ROWS and r % FOLD_ROWS == 0:
        acc = x[:FOLD_ROWS]
        for c in range(1, r // FOLD_ROWS):
            acc = op(acc, x[c * FOLD_ROWS:(c + 1) * FOLD_ROWS])
        x, r = acc, FOLD_ROWS
    while r % 16 == 0:
        x = op(x[:r // 2], x[r // 2:])
        r //= 2
    return (jnp.sum if op is jnp.add else jnp.max)(x, axis=0, keepdims=True)


WORD_BITS = 32
INT_MIN = -2 ** 31
QK_ROWS = 256
FFN_MIN_ROWS = 256
MXU_COLS = 256


def _bit_transpose(words):
    a = list(words)
    j, m = WORD_BITS // 2, 0x0000FFFF
    while j:
        k = 0
        while k < WORD_BITS:
            t = (a[k] ^ lax.shift_right_logical(a[k + j], j)) & m
            a[k] = a[k] ^ t
            a[k + j] = a[k + j] ^ lax.shift_left(t, j)
            k = (k + j + 1) & ~j
        j >>= 1
        m = (m ^ (m << j)) & 0x7FFFFFFF if j else m
    return a


def _select_threshold(key_ref, planes_ref, topk, tq, side_work=()):
    lk = key_ref.shape[0]
    slab = lk // WORD_BITS
    for r0 in range(0, slab, SUBLANES):
        words = [key_ref[i * slab + r0:i * slab + r0 + SUBLANES, :] for i in range(WORD_BITS)]
        for b, plane in enumerate(_bit_transpose(words)):
            planes_ref[b, r0:r0 + SUBLANES, :] = plane
    count = lambda words: _col_reduce(lax.population_count(words), jnp.add)
    alive = jnp.full((slab, tq), -1, jnp.int32)
    left = jnp.full((1, tq), topk, jnp.int32)
    thr = jnp.zeros((1, tq), jnp.int32)
    steps = WORD_BITS // 2
    for step, hi_bit in enumerate(range(WORD_BITS - 1, 0, -2)):
        for piece in side_work[step * len(side_work) // steps:(step + 1) * len(side_work) // steps]:
            piece()
        p_hi = planes_ref[WORD_BITS - 1 - hi_bit]
        p_lo = planes_ref[WORD_BITS - hi_bit]
        if hi_bit == WORD_BITS - 1:
            p_hi = ~p_hi
        u1 = alive & p_hi
        u0 = alive ^ u1
        c11 = u1 & p_lo
        c10 = u1 ^ c11
        c01 = u0 & p_lo
        c00 = u0 ^ c01
        n11 = count(c11)
        n1x = n11 + count(c10)
        n_1 = n1x + count(c01)
        t11 = n11 >= left
        t1x = n1x >= left
        t_1 = n_1 >= left
        alive = jnp.where(t11, c11, jnp.where(t1x, c10, jnp.where(t_1, c01, c00)))
        left = left - jnp.where(t11, 0, jnp.where(t1x, n11, jnp.where(t_1, n1x, n_1)))
        hi_val = jnp.int32(1 << hi_bit if hi_bit < WORD_BITS - 1 else INT_MIN)
        lo_val = jnp.int32(1 << (hi_bit - 1))
        thr = thr | jnp.where(t1x, hi_val, 0) | jnp.where(t11 | (t_1 & ~t1x), lo_val, 0)
    return thr ^ INT_MIN, left


def _attn_t_kernel(*refs, tq, n_keys, q_start, q_block0, topk, aliased):
    q_ref, qi_ref, wi_ref, k_ref, v_ref, ki_ref = refs[:6]
    o_ref, key_ref, bias_ref, planes_ref, logit_ref = refs[7:] if aliased else refs[6:]
    lk = k_ref.shape[1]
    j = pl.program_id(1)
    qchunk = lax.shift_right_logical(
        q_start + (q_block0 + j) * tq + lax.broadcasted_iota(jnp.int32, (1, tq), 1), CHUNK_SHIFT)

    def allowed_at(k0, rows):
        kpos = k0 + lax.broadcasted_iota(jnp.int32, (rows, 1), 0)
        return (lax.shift_right_logical(kpos, CHUNK_SHIFT) <= qchunk) & (kpos < n_keys)

    kib = ki_ref[0].astype(BF16)
    qi = qi_ref[0]
    wi_t = wi_ref[0].T
    score = jnp.zeros((lk, tq), F32)
    for h in range(IDX_HEADS):
        d = _dot_t(kib, qi[:, h * IDX_DIM:(h + 1) * IDX_DIM])
        score = score + wi_t[IDX_DIM + h:IDX_DIM + h + 1, :] * jnp.maximum(d, 0.0)
    score = jnp.where(score == 0.0, 0.0, score)
    free = min(((q_start + q_block0 * tq) // CHUNK + 1) * CHUNK, n_keys, lk)
    if free < lk:
        masked = jnp.where(allowed_at(free, lk - free), score[free:], -jnp.inf)
        score = jnp.concatenate([score[:free], masked], axis=0) if free else masked

    bits = pltpu.bitcast(score, jnp.int32)
    key_ref[...] = bits ^ ((bits >> 31) & jnp.int32(0x7FFFFFFF))

    q = q_ref[0]
    group = B_HEADS // B_KV_HEADS
    qgs = [jnp.concatenate([q[:, (n * group + g) * HEAD_DIM:(n * group + g + 1) * HEAD_DIM]
                            for g in range(group)], axis=0) for n in range(B_KV_HEADS)]

    def logits_block(r0, n):
        kn = k_ref[0, r0:r0 + QK_ROWS, n * HEAD_DIM:(n + 1) * HEAD_DIM].astype(BF16)
        logit_ref[n, r0:r0 + QK_ROWS, :] = _dot_t(kn, qgs[n])

    side_work = [functools.partial(logits_block, r0, n)
                 for r0 in range(0, lk, QK_ROWS) for n in range(B_KV_HEADS)]
    if lk <= topk:
        for piece in side_work:
            piece()
        bias_ref[...] = jnp.where(key_ref[...] >= KEY_MIN_FINITE, 0.0, -jnp.inf)
    else:
        thr, left = _select_threshold(key_ref, planes_ref, topk, tq, side_work)

        clamped = thr < KEY_MIN_FINITE
        thr = jnp.where(clamped, KEY_MIN_FINITE, thr)
        need = jnp.where(clamped, topk, left).astype(F32)
        tril = (lax.broadcasted_iota(jnp.int32, (LANES, LANES), 0)
                >= lax.broadcasted_iota(jnp.int32, (LANES, LANES), 1)).astype(F32).astype(BF16)
        seen = jnp.zeros((1, tq), F32)
        for c in range(lk // LANES):
            sl = slice(c * LANES, (c + 1) * LANES)
            key_c = key_ref[sl, :]
            eq_c = key_c == thr
            eq_f = eq_c.astype(F32)
            rank = _dot(tril, eq_f.astype(BF16)) + seen
            chosen = (key_c > thr) | (eq_c & (rank <= need))
            bias_ref[sl, :] = jnp.where(chosen, 0.0, -jnp.inf)
            seen = seen + _col_reduce(eq_f, jnp.add)

    vt = v_ref[0].T.astype(BF16)
    bias = bias_ref[...]
    ones_rows = jnp.ones((BF16_SUBLANES, lk), BF16)
    outs = []
    for n in range(B_KV_HEADS):
        vtn = jnp.concatenate([vt[n * HEAD_DIM:(n + 1) * HEAD_DIM, :], ones_rows], axis=0)
        ps = []
        for g in range(group):
            lg = logit_ref[n, :, g * tq:(g + 1) * tq] + bias
            ps.append(jnp.exp2(lg - _col_reduce(lg, jnp.maximum)).astype(BF16))
        og = _dot(vtn, jnp.concatenate(ps, axis=1))
        og = og[:HEAD_DIM] / og[HEAD_DIM:HEAD_DIM + 1]
        outs.extend(og[:, g * tq:(g + 1) * tq] for g in range(group))
    o_ref[0] = jnp.concatenate(outs, axis=0).T.astype(BF16)


def _attn_t_call(q, qi, wi, k, v, ki, prev, layer, n_keys, q_start, q_block0, n_q_blocks, lk, tq):
    b, t, _ = q.shape
    topk = min(TOPK_MAX, n_keys // 4)
    aliased = prev is not None
    kern = functools.partial(_attn_t_kernel, tq=tq, n_keys=n_keys, q_start=q_start,
                             q_block0=q_block0, topk=topk, aliased=aliased)
    row3 = lambda bi, j: (bi, q_block0 + j, 0)
    lay_bat4 = lambda bi, j: (layer, bi, 0, 0)
    in_specs = [
        pl.BlockSpec((1, tq, B_WIDTH), row3),
        pl.BlockSpec((1, tq, IDX_HEADS * IDX_DIM), row3),
        pl.BlockSpec((1, tq, LANES), row3),
        pl.BlockSpec((None, 1, lk, B_KV_WIDTH), lay_bat4),
        pl.BlockSpec((None, 1, lk, B_KV_WIDTH), lay_bat4),
        pl.BlockSpec((None, 1, lk, IDX_DIM), lay_bat4),
    ]
    args = [q, qi, wi, k, v, ki]
    if aliased:
        in_specs.append(pl.BlockSpec(memory_space=pl.ANY))
        args.append(prev)
    return pl.pallas_call(
        kern,
        out_shape=jax.ShapeDtypeStruct((b, t, B_WIDTH), BF16),
        grid=(b, n_q_blocks),
        in_specs=in_specs,
        out_specs=pl.BlockSpec((1, tq, B_WIDTH), row3),
        scratch_shapes=[pltpu.VMEM((lk, tq), jnp.int32), pltpu.VMEM((lk, tq), F32),
                        pltpu.VMEM((WORD_BITS, lk // WORD_BITS, tq), jnp.int32),
                        pltpu.VMEM((B_KV_HEADS, lk, B_HEADS // B_KV_HEADS * tq), F32)],
        input_output_aliases={6: 0} if aliased else {},
        compiler_params=pltpu.CompilerParams(
            dimension_semantics=("arbitrary", "arbitrary"),
            vmem_limit_bytes=_vmem_request(lk * tq * 4 * (B_HEADS + 3 + B_HEADS)
                                           + 2 * lk * 4 * (2 * B_KV_WIDTH + IDX_DIM))),
        name="dsa_attention_t",
    )(*args)


def _attn_prompt(q, qi, wi, k, v, ki, layer, tq):
    t = q.shape[1]
    nqb = t // tq
    per_seg = max(1, WORD_BITS * SUBLANES // tq)
    assert nqb % per_seg == 0 and (per_seg * tq) % (WORD_BITS * SUBLANES) == 0
    bo = None
    for s in range(nqb // per_seg):
        bo = _attn_t_call(q, qi, wi, k, v, ki, bo, layer, t, 0, s * per_seg, per_seg,
                          (s + 1) * per_seg * tq, tq)
    return bo


def _ffn_kernel(x_ref, ac_ref, bo_ref, mod_ref, g2_ref, wo_ref, wg_ref, wu_ref, wd_ref,
                gf_ref, o_ref, *, d_model, ff_chunk, final):
    nb, tm, _ = x_ref.shape
    merge = lambda ref: ref[...].reshape(nb * tm, ref.shape[-1])
    x = merge(x_ref)
    mod = mod_ref[...]

    def mod_rows(i):
        m = mod[:, :, i * d_model:(i + 1) * d_model]
        return m[0] if nb == 1 else jnp.broadcast_to(m, (nb, tm, d_model)).reshape(nb * tm, d_model)

    ga1, sh2, sc2, ga2 = mod_rows(2), mod_rows(3), mod_rows(4), mod_rows(5)
    ac = merge(ac_ref)
    mix = (_dot(ac[:, :A_WIDTH], wo_ref[0:A_WIDTH, :])
           + _dot(merge(bo_ref), wo_ref[A_WIDTH:A_WIDTH + B_WIDTH, :])
           + _dot(ac[:, A_WIDTH:], wo_ref[A_WIDTH + B_WIDTH:, :]))
    x1 = x + ga1 * mix
    h2 = ((_rms(x1) * g2_ref[...]) * (1.0 + sc2) + sh2).astype(BF16)
    d_ff = wg_ref.shape[1]
    acc = jnp.zeros_like(x1)
    for c0 in range(0, d_ff, ff_chunk):
        gate = _dot(h2, wg_ref[:, c0:c0 + ff_chunk])
        up = _dot(h2, wu_ref[:, c0:c0 + ff_chunk])
        act = (gate * jax.nn.sigmoid(gate)) * up
        acc = acc + _dot(act.astype(BF16), wd_ref[c0:c0 + ff_chunk, :])
    x2 = x1 + ga2 * acc
    out = _rms(x2) * gf_ref[...] if final else x2
    o_ref[...] = out.reshape(nb, tm, d_model)


def _ffn_call(x, ac, bo, mod, layer, mod_row0, g2, wo, wg, wu, wd, gf, tm, final):
    b, t, d = x.shape
    d_ff = wg.shape[2]
    ff_chunk = MXU_COLS if d_ff % MXU_COLS == 0 else d_ff
    nb = max(1, min(b, FFN_MIN_ROWS // tm))
    assert b % nb == 0 and mod_row0 % nb == 0
    kern = functools.partial(_ffn_kernel, d_model=d, ff_chunk=ff_chunk, final=final)
    row3 = lambda bi, j: (bi, j, 0)
    lay3 = lambda bi, j: (layer, 0, 0)
    once = pl.Buffered(1)
    return pl.pallas_call(
        kern,
        out_shape=jax.ShapeDtypeStruct((b, t, d), F32),
        grid=(b // nb, t // tm),
        in_specs=[
            pl.BlockSpec((nb, tm, d), row3),
            pl.BlockSpec((nb, tm, A_WIDTH + C_WIDTH), row3),
            pl.BlockSpec((nb, tm, B_WIDTH), row3),
            pl.BlockSpec((None, nb, 1, N_MOD * d), lambda bi, j: (layer, mod_row0 // nb + bi, 0, 0)),
            pl.BlockSpec((None, 1, d), lay3),
            pl.BlockSpec((None, A_WIDTH + B_WIDTH + C_WIDTH, d), lay3, pipeline_mode=once),
            pl.BlockSpec((None, d, d_ff), lay3, pipeline_mode=once),
            pl.BlockSpec((None, d, d_ff), lay3, pipeline_mode=once),
            pl.BlockSpec((None, d_ff, d), lay3, pipeline_mode=once),
            pl.BlockSpec((1, d), lambda bi, j: (0, 0)),
        ],
        out_specs=pl.BlockSpec((nb, tm, d), row3),
        compiler_params=pltpu.CompilerParams(
            dimension_semantics=("arbitrary", "arbitrary"), vmem_limit_bytes=VMEM_LIMIT),
        name="out_proj_ffn",
    )(x, ac, bo, mod, g2, wo, wg, wu, wd, gf)


def _rope_tables(start, t):
    half = HEAD_DIM // 2
    inv = jnp.power(ROPE_THETA, -2.0 * jnp.arange(half, dtype=F32) / HEAD_DIM)
    ang = (start + jnp.arange(t)).astype(F32)[:, None] * inv[None, :]
    cos = jnp.cos(ang)
    sin = jnp.sin(ang)
    zero = jnp.zeros_like(sin)
    reps = LANES // HEAD_DIM
    return (jnp.tile(jnp.concatenate([cos, cos], axis=-1), (1, reps)),
            jnp.tile(jnp.concatenate([-sin, zero], axis=-1), (1, reps)),
            jnp.tile(jnp.concatenate([zero, sin], axis=-1), (1, reps)))


def _pad_w_in(w_in):
    n_head = OFF_KI + IDX_DIM + IDX_HEADS
    depth, d, _ = w_in.shape
    pad = jnp.zeros((depth, d, OFF_C - n_head), w_in.dtype)
    return jnp.concatenate([w_in[:, :, :n_head], pad, w_in[:, :, n_head:]], axis=-1).astype(BF16)


def _trunk(x, mod, mod_row0, cache, params, q_start, tm, tq):
    (g_norm1, g_norm2, w_in_p, w_spatial, b_spatial, w_dw, b_dw, g_cnorm,
     wo, wg, wu, wd, g_final) = params
    b, t, d = x.shape
    depth = w_in_p.shape[0]
    ln = min(t, A_CHUNK)
    tabs = _rope_tables(q_start, t)
    wsp = w_spatial[:, :, :ln, :ln]
    bsp_tab = jnp.repeat(jnp.swapaxes(b_spatial[:, :, :ln], 1, 2), A_HEAD_DIM, axis=-1)
    kv_stack, convs, avs = None, [], []
    if cache is not None:
        past = cache[0].shape[2]
        cache_bf16 = (cache[0].astype(BF16).reshape(depth, b, past, B_KV_WIDTH),
                      cache[1].astype(BF16).reshape(depth, b, past, B_KV_WIDTH),
                      cache[2].astype(BF16))
    for l in range(depth):
        if cache is None:
            cst = jnp.zeros((b, CTX_ROWS, C_WIDTH), F32)
        else:
            cst = jnp.pad(cache[3][l], ((0, 0), (CTX_PAD, 0), (0, 0)))
        q, ks, vs, qi, kis, wi, ac, av, cnew = _in_call(
            x, mod, l, mod_row0, g_norm1, w_in_p, tabs, wsp, bsp_tab, w_dw, b_dw, g_cnorm, cst,
            kv_stack, tm)
        kv_stack = (ks, vs, kis)
        if cache is None:
            bo = _attn_prompt(q, qi, wi, ks, vs, kis, l, tq)
        else:
            bo = _attn_call(q, qi, wi, *cache_bf16, ks, vs, kis, l, q_start)
        x = _ffn_call(x, ac, bo, mod, l, mod_row0, g_norm2, wo, wg, wu, wd, g_final,
                      tm, final=(l == depth - 1))
        convs.append(cnew[:, CTX_PAD:, :])
        avs.append(av)
    ks, vs, kis = kv_stack
    return (x, ks.reshape(depth, b, t, B_KV_HEADS, HEAD_DIM), vs.reshape(depth, b, t, B_KV_HEADS, HEAD_DIM),
            kis, jnp.stack(convs), jnp.stack(avs))


def kernel(x_prompt, x_sample, cache_k, cache_v, cache_kidx, state_conv, c_prompt, c_sample,
           w_ada, b_ada, g_norm1, g_norm2, w_in, w_spatial, b_spatial, w_dw, b_dw, g_cnorm,
           w_out, w_gate, w_up, w_down, g_final):
    depth, d = g_norm1.shape
    nb_p, t_p, _ = x_prompt.shape
    nb_s, t_s, _ = x_sample.shape
    past = cache_k.shape[2]

    rows = nb_p + nb_s
    rows_pad = -(-rows // 8) * 8
    c_all = jnp.pad(jnp.concatenate([c_prompt, c_sample], axis=0), ((0, rows_pad - rows), (0, 0)))
    mod = _mod_call(c_all, w_ada, b_ada).reshape(depth, rows_pad, 1, N_MOD * d)

    params = (
        g_norm1.reshape(depth, 1, d), g_norm2.reshape(depth, 1, d), _pad_w_in(w_in),
        w_spatial, b_spatial, w_dw, b_dw.reshape(depth, 1, C_WIDTH), g_cnorm.reshape(depth, 1, C_WIDTH),
        w_out.astype(BF16), w_gate.astype(BF16), w_up.astype(BF16), w_down.astype(BF16),
        g_final.reshape(1, d),
    )

    y_p, p_k, p_v, p_ki, p_conv, _ = _trunk(
        x_prompt, mod, 0, None, params, 0, tm=min(t_p, 512), tq=min(t_p, 256))
    y_s, s_k, s_v, s_ki, s_conv, s_av = _trunk(
        x_sample, mod, nb_p, (cache_k, cache_v, cache_kidx, state_conv), params, past,
        tm=t_s, tq=t_s)
    return (y_p, y_s, p_k, p_v, p_ki, p_conv, s_k, s_v, s_ki, s_conv, s_av)
```
